```python
import math
import jax, jax.numpy as jnp
from jax import lax
import numpy as np

D_MODEL = 1024
BATCH = 8
SEQ = 4096
DEPTH = 4

N_MIXERS = 4
N_RET = (DEPTH + 3) // 4
N_CONV = (DEPTH + 2) // 4
N_NSA = (DEPTH + 1) // 4
N_POOL = DEPTH // 4

ROPE_THETA = 10000.0
DN_ALPHA = (2.0 * DEPTH) ** 0.25
DN_BETA = (8.0 * DEPTH) ** -0.25
LN_EPS = 1e-5
NEG = -1e30

RET_HEADS = 4
RET_DK = D_MODEL // RET_HEADS
RET_DV = 2 * D_MODEL // RET_HEADS
RET_CHUNK = 128

CONV_WIDTH = 31

NSA_HEADS = 16
NSA_KV_GROUPS = 2
NSA_HEAD_DIM = D_MODEL // NSA_HEADS
NSA_CMP_BLOCK = 32
NSA_CMP_STRIDE = 16
NSA_CMP_HIDDEN = 256
NSA_SEL_BLOCK = 64
NSA_TOP_N = 16
NSA_WINDOW = 512
NSA_Q_BLOCK = 64
NSA_FORCE = 1e6
NSA_IN_WIDTH = NSA_HEADS * NSA_HEAD_DIM + 6 * NSA_KV_GROUPS * NSA_HEAD_DIM + 3 * NSA_HEADS

POOL_WINDOWS = (2, 4, 8, 16)
POOL_GROUP = D_MODEL // len(POOL_WINDOWS)

N_EXPERTS = 32
TOP_K = 4
D_FF = D_MODEL
SWIGLU_LIMIT = 7.0
SWIGLU_ALPHA = 1.702
MOE_BLOCK = 256

kernel_name = 'hybrid_ret_conv_nsa_pool_moe_trunk'


def layer_norm(x, g, b):
    xf = x.astype(jnp.float32)
    mu = xf.mean(-1, keepdims=True)
    var = jnp.square(xf - mu).mean(-1, keepdims=True)
    return ((xf - mu) * lax.rsqrt(var + LN_EPS)).astype(x.dtype) * g + b


def rope(x, positions):
    d = x.shape[-1]
    inv = ROPE_THETA ** (-jnp.arange(0, d, 2, dtype=jnp.float32) / d)
    ang = positions.astype(jnp.float32)[:, :, None, None] * inv
    cos, sin = jnp.cos(ang).astype(x.dtype), jnp.sin(ang).astype(x.dtype)
    x1, x2 = x[..., : d // 2], x[..., d // 2:]
    return jnp.concatenate([x1 * cos - x2 * sin, x1 * sin + x2 * cos], axis=-1)


def retention_mixer(h, positions, w_in, gn_g, gn_b, w_out):
    B, S, _ = h.shape
    H, dk, dv, C = RET_HEADS, RET_DK, RET_DV, RET_CHUNK
    n = S // C
    dt = h.dtype
    q, k, v, g = jnp.split(h @ w_in, [H * dk, 2 * H * dk, 2 * H * dk + H * dv], axis=-1)
    q = rope(q.reshape(B, S, H, dk), positions)
    k = rope(k.reshape(B, S, H, dk), positions) * (dk ** -0.5)
    v = v.reshape(B, S, H, dv)
    log_gamma = jnp.log1p(-(2.0 ** (-5.0 - jnp.arange(H, dtype=jnp.float32))))
    i = jnp.arange(C, dtype=jnp.float32)
    rel = i[:, None] - i[None, :]
    decay_intra = jnp.where(rel >= 0, jnp.exp(log_gamma[:, None, None] * jnp.maximum(rel, 0.0)), 0.0).astype(dt)
    decay_q = jnp.exp(log_gamma[:, None] * (i + 1.0)).astype(dt)[..., None]
    decay_k = jnp.exp(log_gamma[:, None] * (C - 1.0 - i)).astype(dt)[..., None]
    decay_chunk = jnp.exp(log_gamma * C).astype(dt)[:, None, None]

    def chunks(t):
        return t.reshape(B, n, C, H, t.shape[-1]).transpose(1, 0, 3, 2, 4)

    def step(state, qkv):
        qc, kc, vc = qkv
        o = jnp.einsum('bhij,bhjv->bhiv', jnp.einsum('bhid,bhjd->bhij', qc, kc) * decay_intra, vc)
        o = o + jnp.einsum('bhid,bhdv->bhiv', qc * decay_q, state)
        state = state * decay_chunk + jnp.einsum('bhjd,bhjv->bhdv', kc * decay_k, vc)
        return state, o

    _, o = lax.scan(step, jnp.zeros((B, H, dk, dv), dt), (chunks(q), chunks(k), chunks(v)))
    o = o.transpose(1, 0, 3, 2, 4).reshape(B, S, H, dv)
    of = o.astype(jnp.float32)
    mu = of.mean(-1, keepdims=True)
    var = jnp.square(of - mu).mean(-1, keepdims=True)
    o = ((of - mu) * lax.rsqrt(var + LN_EPS)).astype(dt).reshape(B, S, H * dv) * gn_g + gn_b
    return (jax.nn.silu(g) * o) @ w_out


def conv_mixer(h, w_pw1, b_pw1, w_dw, b_dw, ln_g, ln_b, w_pw2, b_pw2):
    a, gate = jnp.split(h @ w_pw1 + b_pw1, 2, axis=-1)
    u = a * jax.nn.sigmoid(gate)
    u = jnp.pad(u, ((0, 0), (CONV_WIDTH - 1, 0), (0, 0)))
    u = lax.conv_general_dilated(u, w_dw[:, None, :], window_strides=(1,), padding='VALID',
                                 dimension_numbers=('NWC', 'WIO', 'NWC'),
                                 feature_group_count=D_MODEL) + b_dw
    u = jax.nn.silu(layer_norm(u, ln_g, ln_b))
    return u @ w_pw2 + b_pw2


def _compress(t, pos_emb, w1, w2, win_idx):
    blk = t[:, win_idx] + pos_emb[:, None, :]
    b_, n, l, g_, d = blk.shape
    blk = blk.transpose(0, 3, 1, 2, 4).reshape(b_, g_, n, l * d)
    return jax.nn.gelu(blk @ w1) @ w2


def nsa_mixer(h, positions, w_in, gate_b, cmp_pos_k, cmp_pos_v, cmp_k_w1, cmp_k_w2,
              cmp_v_w1, cmp_v_w2, w_out):
    B, S, _ = h.shape
    H, G, dh = NSA_HEADS, NSA_KV_GROUPS, NSA_HEAD_DIM
    R = H // G
    Qb, W, ls = NSA_Q_BLOCK, NSA_WINDOW, NSA_SEL_BLOCK
    dt = h.dtype
    sizes = [H * dh] + [G * dh] * 6 + [3 * H]
    q, kc, vc, ks, vs, kw, vw, gl = jnp.split(h @ w_in, np.cumsum(sizes)[:-1].tolist(), axis=-1)
    q = rope(q.reshape(B, S, H, dh), positions) * (dh ** -0.5)
    kc = rope(kc.reshape(B, S, G, dh), positions)
    ks = rope(ks.reshape(B, S, G, dh), positions)
    kw = rope(kw.reshape(B, S, G, dh), positions)
    vc, vs, vw = (t.reshape(B, S, G, dh) for t in (vc, vs, vw))
    gates = jax.nn.sigmoid(gl + gate_b).reshape(B, S, H, 3)

    n_cmp = (S - NSA_CMP_BLOCK) // NSA_CMP_STRIDE + 1
    cmp_start = np.arange(n_cmp) * NSA_CMP_STRIDE
    win_idx = cmp_start[:, None] + np.arange(NSA_CMP_BLOCK)[None, :]
    k_cmp = _compress(kc, cmp_pos_k, cmp_k_w1, cmp_k_w2, win_idx)
    v_cmp = _compress(vc, cmp_pos_v, cmp_v_w1, cmp_v_w2, win_idx)
    cmp_end = jnp.asarray(cmp_start + NSA_CMP_BLOCK - 1)
    n_sel = S // ls
    j_np = np.arange(n_sel)
    overlap = jnp.asarray(((cmp_start[:, None] < (j_np[None, :] + 1) * ls) &
                           (cmp_start[:, None] + NSA_CMP_BLOCK > j_np[None, :] * ls)).astype(np.float32))
    k_sel_n = min(NSA_TOP_N, n_sel)
    ks_blocks = ks.reshape(B, n_sel, ls, G, dh).transpose(0, 3, 1, 2, 4)
    vs_blocks = vs.reshape(B, n_sel, ls, G, dh).transpose(0, 3, 1, 2, 4)
    kw_pad = jnp.pad(kw, ((0, 0), (W, 0), (0, 0), (0, 0)))
    vw_pad = jnp.pad(vw, ((0, 0), (W, 0), (0, 0), (0, 0)))
    b_ix = jnp.arange(B)[:, None, None, None]
    g_ix = jnp.arange(G)[None, :, None, None]
    j = jnp.arange(n_sel)

    def q_block(qb):
        t0 = qb * Qb
        t = t0 + jnp.arange(Qb)
        qi = lax.dynamic_slice_in_dim(q, t0, Qb, axis=1).reshape(B, Qb, G, R, dh)
        s_c = jnp.einsum('bqgrd,bgnd->bgrqn', qi, k_cmp).astype(jnp.float32)
        mask_c = cmp_end[None, :] <= t[:, None]
        p_c = jnp.where(mask_c, jax.nn.softmax(jnp.where(mask_c, s_c, NEG), axis=-1), 0.0)
        o_c = jnp.einsum('bgrqn,bgnd->bqgrd', p_c.astype(dt), v_cmp)
        imp = jnp.einsum('bgrqn,nj->bgqj', p_c, overlap)
        cur = t // ls
        forced = (j[None, :] == 0) | (j[None, :] == cur[:, None]) | (j[None, :] == cur[:, None] - 1)
        score = jnp.where(j[None, :] * ls <= t[:, None], jnp.where(forced, NSA_FORCE, imp), NEG)
        top_s, top_i = lax.top_k(score, k_sel_n)
        k_g = ks_blocks[b_ix, g_ix, top_i]
        v_g = vs_blocks[b_ix, g_ix, top_i]
        key_pos = top_i[..., None] * ls + jnp.arange(ls)
        mask_s = (top_s > 0.5 * NEG)[..., None] & (key_pos <= t[None, None, :, None, None])
        s_s = jnp.einsum('bqgrd,bgqnld->bgrqnl', qi, k_g).astype(jnp.float32)
        s_s = jnp.where(mask_s[:, :, None], s_s, NEG)
        p_s = jax.nn.softmax(s_s.reshape(B, G, R, Qb, k_sel_n * ls), axis=-1).reshape(s_s.shape)
        o_s = jnp.einsum('bgrqnl,bgqnld->bqgrd', p_s.astype(dt), v_g)
        kwin = lax.dynamic_slice_in_dim(kw_pad, t0, Qb + W, axis=1)
        vwin = lax.dynamic_slice_in_dim(vw_pad, t0, Qb + W, axis=1)
        kpos = t0 - W + jnp.arange(Qb + W)
        mask_w = (kpos[None, :] <= t[:, None]) & (kpos[None, :] > t[:, None] - W) & (kpos[None, :] >= 0)
        s_w = jnp.einsum('bqgrd,bkgd->bgrqk', qi, kwin).astype(jnp.float32)
        p_w = jax.nn.softmax(jnp.where(mask_w, s_w, NEG), axis=-1)
        o_w = jnp.einsum('bgrqk,bkgd->bqgrd', p_w.astype(dt), vwin)
        gt = lax.dynamic_slice_in_dim(gates, t0, Qb, axis=1).reshape(B, Qb, G, R, 3)
        o = gt[..., 0:1] * o_c + gt[..., 1:2] * o_s + gt[..., 2:3] * o_w
        return o.reshape(B, Qb, H * dh)

    out = lax.map(q_block, jnp.arange(S // Qb))
    out = out.transpose(1, 0, 2, 3).reshape(B, S, H * dh)
    return out @ w_out


def pool_mixer(h, w, b, scale):
    B, S, D = h.shape
    cs = jnp.pad(jnp.cumsum(h.astype(jnp.float32), axis=1), ((0, 0), (1, 0), (0, 0)))
    t = jnp.arange(S)
    groups = []
    for gi, win in enumerate(POOL_WINDOWS):
        sl = slice(gi * POOL_GROUP, (gi + 1) * POOL_GROUP)
        lo = jnp.maximum(t + 1 - win, 0)
        cnt = (t + 1 - lo).astype(jnp.float32)[None, :, None]
        groups.append((cs[:, 1:, sl] - cs[:, lo, sl]) / cnt)
    pooled = jnp.concatenate(groups, axis=-1).astype(h.dtype) - h
    y = jnp.einsum('bsgc,gcd->bsgd', pooled.reshape(B, S, len(POOL_WINDOWS), POOL_GROUP), w)
    return (y.reshape(B, S, D) + b) * scale


def moe(h, router_w, router_b, w1, b1, w2, b2):
    B, S, D = h.shape
    T = B * S
    xf = h.reshape(T, D)
    logits = (xf @ router_w + router_b).astype(jnp.float32)
    top_v, top_i = lax.top_k(logits, TOP_K)
    gate = jax.nn.softmax(top_v, axis=-1).astype(h.dtype)
    flat_e = top_i.reshape(-1)
    flat_tok = jnp.arange(T * TOP_K, dtype=jnp.int32) // TOP_K
    flat_w = gate.reshape(-1)
    order = jnp.argsort(flat_e)
    e_sorted = flat_e[order]
    counts = jnp.bincount(flat_e, length=N_EXPERTS)
    starts = jnp.cumsum(counts) - counts
    blocks_per = (counts + MOE_BLOCK - 1) // MOE_BLOCK
    block_end = jnp.cumsum(blocks_per)
    pad_start = (block_end - blocks_per) * MOE_BLOCK
    dest = pad_start[e_sorted] + (jnp.arange(T * TOP_K) - starts[e_sorted])
    n_blocks = -(-(T * TOP_K) // MOE_BLOCK) + N_EXPERTS
    slot_tok = jnp.full((n_blocks * MOE_BLOCK,), T, jnp.int32).at[dest].set(flat_tok[order])
    slot_w = jnp.zeros((n_blocks * MOE_BLOCK,), h.dtype).at[dest].set(flat_w[order])
    block_expert = jnp.minimum(jnp.searchsorted(block_end, jnp.arange(n_blocks), side='right'), N_EXPERTS - 1)
    x_pad = jnp.concatenate([xf, jnp.zeros((1, D), h.dtype)], axis=0)

    def expert_block(args):
        tok, e = args
        gu = x_pad[tok] @ w1[e] + b1[e]
        g_, up = gu[:, :D_FF], gu[:, D_FF:]
        g_ = jnp.minimum(g_, SWIGLU_LIMIT)
        up = jnp.clip(up, -SWIGLU_LIMIT, SWIGLU_LIMIT)
        return ((up + 1.0) * (g_ * jax.nn.sigmoid(SWIGLU_ALPHA * g_))) @ w2[e] + b2[e]

    out = lax.map(expert_block, (slot_tok.reshape(n_blocks, MOE_BLOCK), block_expert))
    y = jnp.zeros((T + 1, D), h.dtype).at[slot_tok].add(out.reshape(-1, D) * slot_w[:, None])
    return y[:T].reshape(B, S, D)


def _w(key, shape, fan_in, gain=1.0):
    return jax.random.normal(key, shape, jnp.float32) * (gain * fan_in ** -0.5)


def _gain(key, shape):
    return 1.0 + 0.02 * jax.random.normal(key, shape, jnp.float32)


def _bias(key, shape, s=0.02):
    return s * jax.random.normal(key, shape, jnp.float32)


def setup_inputs(seed: int = 0) -> dict:
    key = jax.random.key(seed)
    ks = iter(jax.random.split(key, 64))
    D, E, F = D_MODEL, N_EXPERTS, D_FF
    H, dk, dv = RET_HEADS, RET_DK, RET_DV
    Hn, G, dh = NSA_HEADS, NSA_KV_GROUPS, NSA_HEAD_DIM
    offs = jax.random.randint(next(ks), (BATCH, 1), 0, 1024, dtype=jnp.int32)
    return {
        'x': jax.random.normal(next(ks), (BATCH, SEQ, D), jnp.float32),
        'c': jax.random.normal(next(ks), (BATCH, D), jnp.float32),
        'positions': offs + jnp.arange(SEQ, dtype=jnp.int32)[None, :],
        'ada_w': _w(next(ks), (DEPTH, D, 6 * D), D, 0.2),
        'ada_b': _bias(next(ks), (DEPTH, 6 * D), 0.01),
        'ln1_g': _gain(next(ks), (DEPTH, D)),
        'ln1_b': _bias(next(ks), (DEPTH, D)),
        'ln2_g': _gain(next(ks), (DEPTH, D)),
        'ln2_b': _bias(next(ks), (DEPTH, D)),
        'router_w': _w(next(ks), (DEPTH, D, E), D),
        'router_b': _bias(next(ks), (DEPTH, E), 0.01),
        'moe_w1': _w(next(ks), (DEPTH, E, D, 2 * F), D),
        'moe_b1': _bias(next(ks), (DEPTH, E, 2 * F), 0.01),
        'moe_w2': _w(next(ks), (DEPTH, E, F, D), F, DN_BETA),
        'moe_b2': _bias(next(ks), (DEPTH, E, D), 0.01),
        'ret_w_in': _w(next(ks), (N_RET, D, 2 * H * dk + 2 * H * dv), D),
        'ret_gn_g': _gain(next(ks), (N_RET, H * dv)),
        'ret_gn_b': _bias(next(ks), (N_RET, H * dv)),
        'ret_w_out': _w(next(ks), (N_RET, H * dv, D), H * dv, DN_BETA),
        'conv_w_pw1': _w(next(ks), (N_CONV, D, 2 * D), D),
        'conv_b_pw1': _bias(next(ks), (N_CONV, 2 * D)),
        'conv_w_dw': _w(next(ks), (N_CONV, CONV_WIDTH, D), CONV_WIDTH),
        'conv_b_dw': _bias(next(ks), (N_CONV, D)),
        'conv_ln_g': _gain(next(ks), (N_CONV, D)),
        'conv_ln_b': _bias(next(ks), (N_CONV, D)),
        'conv_w_pw2': _w(next(ks), (N_CONV, D, D), D, DN_BETA),
        'conv_b_pw2': _bias(next(ks), (N_CONV, D)),
        'nsa_w_in': _w(next(ks), (N_NSA, D, NSA_IN_WIDTH), D),
        'nsa_gate_b': _bias(next(ks), (N_NSA, 3 * Hn), 0.1),
        'nsa_cmp_pos_k': _bias(next(ks), (N_NSA, NSA_CMP_BLOCK, dh), 0.1),
        'nsa_cmp_pos_v': _bias(next(ks), (N_NSA, NSA_CMP_BLOCK, dh), 0.1),
        'nsa_cmp_k_w1': _w(next(ks), (N_NSA, NSA_CMP_BLOCK * dh, NSA_CMP_HIDDEN), NSA_CMP_BLOCK * dh),
        'nsa_cmp_k_w2': _w(next(ks), (N_NSA, NSA_CMP_HIDDEN, dh), NSA_CMP_HIDDEN),
        'nsa_cmp_v_w1': _w(next(ks), (N_NSA, NSA_CMP_BLOCK * dh, NSA_CMP_HIDDEN), NSA_CMP_BLOCK * dh),
        'nsa_cmp_v_w2': _w(next(ks), (N_NSA, NSA_CMP_HIDDEN, dh), NSA_CMP_HIDDEN),
        'nsa_w_out': _w(next(ks), (N_NSA, Hn * dh, D), Hn * dh, DN_BETA),
        'pool_w': _w(next(ks), (N_POOL, len(POOL_WINDOWS), POOL_GROUP, POOL_GROUP), POOL_GROUP, DN_BETA),
        'pool_b': _bias(next(ks), (N_POOL, D)),
        'pool_scale': _gain(next(ks), (N_POOL, D)),
    }


def reference(x, c, positions, ada_w, ada_b, ln1_g, ln1_b, ln2_g, ln2_b, router_w, router_b,
              moe_w1, moe_b1, moe_w2, moe_b2, ret_w_in, ret_gn_g, ret_gn_b, ret_w_out,
              conv_w_pw1, conv_b_pw1, conv_w_dw, conv_b_dw, conv_ln_g, conv_ln_b, conv_w_pw2,
              conv_b_pw2, nsa_w_in, nsa_gate_b, nsa_cmp_pos_k, nsa_cmp_pos_v, nsa_cmp_k_w1,
              nsa_cmp_k_w2, nsa_cmp_v_w1, nsa_cmp_v_w2, nsa_w_out, pool_w, pool_b, pool_scale):
    c_act = jax.nn.silu(c)
    for i in range(DEPTH):
        kind, j = i % N_MIXERS, i // N_MIXERS
        mod = c_act @ ada_w[i] + ada_b[i]
        sh1, sc1, g1, sh2, sc2, g2 = (m[:, None, :] for m in jnp.split(mod, 6, axis=-1))
        h = x * (1.0 + sc1) + sh1
        if kind == 0:
            y = retention_mixer(h, positions, ret_w_in[j], ret_gn_g[j], ret_gn_b[j], ret_w_out[j])
        elif kind == 1:
            y = conv_mixer(h, conv_w_pw1[j], conv_b_pw1[j], conv_w_dw[j], conv_b_dw[j],
                           conv_ln_g[j], conv_ln_b[j], conv_w_pw2[j], conv_b_pw2[j])
        elif kind == 2:
            y = nsa_mixer(h, positions, nsa_w_in[j], nsa_gate_b[j], nsa_cmp_pos_k[j], nsa_cmp_pos_v[j],
                          nsa_cmp_k_w1[j], nsa_cmp_k_w2[j], nsa_cmp_v_w1[j], nsa_cmp_v_w2[j], nsa_w_out[j])
        else:
            y = pool_mixer(h, pool_w[j], pool_b[j], pool_scale[j])
        x = layer_norm(DN_ALPHA * x + (1.0 + g1) * y, ln1_g[i], ln1_b[i])
        h = x * (1.0 + sc2) + sh2
        y = moe(h, router_w[i], router_b[i], moe_w1[i], moe_b1[i], moe_w2[i], moe_b2[i])
        x = layer_norm(DN_ALPHA * x + (1.0 + g2) * y, ln2_g[i], ln2_b[i])
    return x
```

```python
import functools
import math

import jax
import jax.numpy as jnp
import numpy as np
from jax import lax
from jax.experimental import pallas as pl
from jax.experimental.pallas import tpu as pltpu

F32 = jnp.float32
BF16 = jnp.bfloat16
I32 = jnp.int32

ROPE_BASE = 10000.0
EPS = 1e-5
NEG_BIG = -1e30

RET_H = 4
RET_C = 128
CONV_W = 31
NSA_H = 16
NSA_G = 2
NSA_DH = 64
NSA_CMP_LEN = 32
NSA_CMP_STEP = 16
NSA_SEL_LEN = 64
NSA_TOPN = 16
NSA_WIN = 512
NSA_FORCED = 1e6
POOL_WINS = (2, 4, 8, 16)
MOE_E = 32
MOE_K = 4
MOE_BLK = 256
GLU_LIMIT = 7.0
GLU_ALPHA = 1.702

V7X_VMEM_LIMIT = 56 * 1024 * 1024
LANES = 128


def _params(sem):
    return pltpu.CompilerParams(dimension_semantics=sem, vmem_limit_bytes=V7X_VMEM_LIMIT)


def _ln_rows(v, g, b):
    mu = jnp.mean(v, axis=-1, keepdims=True)
    d = v - mu
    var = jnp.mean(d * d, axis=-1, keepdims=True)
    return d * lax.rsqrt(var + EPS) * g + b


def _sigmoid(v):
    return 1.0 / (1.0 + jnp.exp(-v))


def _dot(a, b):
    return jnp.dot(a, b, preferred_element_type=F32)


def _dot_nt(a, b):
    return lax.dot_general(a, b, (((1,), (1,)), ((), ())), preferred_element_type=F32)


def _dot_tn(a, b):
    return lax.dot_general(a, b, (((0,), (0,)), ((), ())), preferred_element_type=F32)


def _ada_kernel(c_ref, w_ref, b_ref, o_ref):
    c = c_ref[...]
    ca = c * _sigmoid(c)
    o_ref[0] = _dot(ca.astype(BF16), w_ref[0].astype(BF16)) + b_ref[0]


def ada_modulation(c, ada_w, ada_b):
    L, D, N = ada_w.shape
    B = c.shape[0]
    tn = 2048 if N % 2048 == 0 else N
    return pl.pallas_call(
        _ada_kernel,
        grid=(L, N // tn),
        in_specs=[pl.BlockSpec((B, D), lambda l, j: (0, 0)),
                  pl.BlockSpec((1, D, tn), lambda l, j: (l, 0, j)),
                  pl.BlockSpec((1, 1, tn), lambda l, j: (l, 0, j))],
        out_specs=pl.BlockSpec((1, B, tn), lambda l, j: (l, 0, j)),
        out_shape=jax.ShapeDtypeStruct((L, B, N), F32),
        compiler_params=_params(("parallel", "parallel")),
        name="ada_modulation",
    )(c, ada_w, ada_b.reshape(L, 1, N))


def _mm_kernel(*refs, has_mod, has_bias, epi, alpha):
    it = iter(refs)
    x_ref = next(it)
    if has_mod:
        sc_ref, sh_ref = next(it), next(it)
    w_ref = next(it)
    if has_bias:
        b_ref = next(it)
    if epi == "resln":
        res_ref, gate_ref, g_ref, bb_ref = next(it), next(it), next(it), next(it)
    o_ref = next(it)
    x = x_ref[...]
    if has_mod:
        x = x.astype(F32) * (1.0 + sc_ref[0]) + sh_ref[0]
    acc = _dot(x.astype(BF16), w_ref[...])
    if has_bias:
        acc = acc + b_ref[...]
    if epi == "resln":
        v = alpha * res_ref[...] + (1.0 + gate_ref[0]) * acc
        acc = _ln_rows(v, g_ref[...], bb_ref[...])
    o_ref[...] = acc.astype(o_ref.dtype)


def fused_matmul(x, w, *, seq, mod=None, bias=None, resln=None, alpha=1.0, out_dtype=F32, tm=512, tn=None, name="mm"):
    T, K = x.shape
    N = w.shape[1]
    tn = N if tn is None else tn
    assert T % tm == 0 and seq % tm == 0 and N % tn == 0
    nb = seq // tm
    args, specs = [x], [pl.BlockSpec((tm, K), lambda i, j: (i, 0))]
    if mod is not None:
        for m in mod:
            args.append(m)
            specs.append(pl.BlockSpec((1, 1, K), lambda i, j: (i // nb, 0, 0)))
    args.append(w)
    specs.append(pl.BlockSpec((K, tn), lambda i, j: (0, j)))
    if bias is not None:
        args.append(bias.reshape(1, N))
        specs.append(pl.BlockSpec((1, tn), lambda i, j: (0, j)))
    epi = "none"
    if resln is not None:
        assert tn == N
        res, gate, g, b = resln
        epi = "resln"
        args += [res, gate, g.reshape(1, N), b.reshape(1, N)]
        specs += [pl.BlockSpec((tm, N), lambda i, j: (i, 0)),
                  pl.BlockSpec((1, 1, N), lambda i, j: (i // nb, 0, 0)),
                  pl.BlockSpec((1, N), lambda i, j: (0, 0)),
                  pl.BlockSpec((1, N), lambda i, j: (0, 0))]
    kern = functools.partial(_mm_kernel, has_mod=mod is not None, has_bias=bias is not None, epi=epi, alpha=alpha)
    return pl.pallas_call(
        kern,
        grid=(T // tm, N // tn),
        in_specs=specs,
        out_specs=pl.BlockSpec((tm, tn), lambda i, j: (i, j)),
        out_shape=jax.ShapeDtypeStruct((T, N), out_dtype),
        compiler_params=_params(("parallel", "parallel")),
        name=name,
    )(*args)


def _ret_proj_kernel(x_ref, sc_ref, sh_ref, w_ref, cos_ref, sin_ref, o_ref, *, dk, n_qk, n_v, kscale):
    j = pl.program_id(1)
    h = x_ref[...] * (1.0 + sc_ref[0]) + sh_ref[0]
    acc = _dot(h.astype(BF16), w_ref[...])
    half = dk // 2

    def rope(scale):
        cos, sin = cos_ref[...], sin_ref[...]
        for hh in range(acc.shape[1] // dk):
            x1 = acc[:, hh * dk:hh * dk + half]
            x2 = acc[:, hh * dk + half:(hh + 1) * dk]
            o_ref[:, hh * dk:hh * dk + half] = ((x1 * cos - x2 * sin) * scale).astype(o_ref.dtype)
            o_ref[:, hh * dk + half:(hh + 1) * dk] = ((x1 * sin + x2 * cos) * scale).astype(o_ref.dtype)

    @pl.when(j < n_qk)
    def _():
        rope(1.0)

    @pl.when(jnp.logical_and(j >= n_qk, j < 2 * n_qk))
    def _():
        rope(kscale)

    @pl.when(jnp.logical_and(j >= 2 * n_qk, j < 2 * n_qk + n_v))
    def _():
        o_ref[...] = acc.astype(o_ref.dtype)

    @pl.when(j >= 2 * n_qk + n_v)
    def _():
        o_ref[...] = (acc * _sigmoid(acc)).astype(o_ref.dtype)


def ret_projection(x, sc, sh, w_in, cos, sin, *, seq, tm=512, tn=1024):
    T, K = x.shape
    N = w_in.shape[1]
    hdk = N // 6
    dk = hdk // RET_H
    assert hdk % tn == 0 and tn % dk == 0
    nb = seq // tm
    kern = functools.partial(_ret_proj_kernel, dk=dk, n_qk=hdk // tn, n_v=2 * hdk // tn, kscale=dk ** -0.5)
    return pl.pallas_call(
        kern,
        grid=(T // tm, N // tn),
        in_specs=[pl.BlockSpec((tm, K), lambda i, j: (i, 0)),
                  pl.BlockSpec((1, 1, K), lambda i, j: (i // nb, 0, 0)),
                  pl.BlockSpec((1, 1, K), lambda i, j: (i // nb, 0, 0)),
                  pl.BlockSpec((K, tn), lambda i, j: (0, j)),
                  pl.BlockSpec((tm, dk // 2), lambda i, j: (i, 0)),
                  pl.BlockSpec((tm, dk // 2), lambda i, j: (i, 0))],
        out_specs=pl.BlockSpec((tm, tn), lambda i, j: (i, j)),
        out_shape=jax.ShapeDtypeStruct((T, N), BF16),
        compiler_params=_params(("parallel", "parallel")),
        name="ret_projection",
    )(x, sc, sh, w_in, cos, sin)


def _retention_kernel(q_ref, k_ref, v_ref, sg_ref, din_ref, dq_ref, dkk_ref, dch_ref, gg_ref, gb_ref,
                      o_ref, state_ref, *, H, C, dk, dv):
    @pl.when(pl.program_id(1) == 0)
    def _():
        state_ref[...] = jnp.zeros_like(state_ref)

    n_chunks = q_ref.shape[0] // C
    for c in range(n_chunks):
        rows = slice(c * C, (c + 1) * C)
        for hh in range(H):
            qc = q_ref[rows, hh * dk:(hh + 1) * dk]
            kc = k_ref[rows, hh * dk:(hh + 1) * dk]
            vc = v_ref[rows, hh * dv:(hh + 1) * dv]
            st = state_ref[hh]
            a = _dot_nt(qc, kc) * din_ref[hh]
            o = _dot(a.astype(BF16), vc)
            qd = (qc.astype(F32) * dq_ref[hh]).astype(BF16)
            o = o + _dot(qd, st.astype(BF16))
            kd = (kc.astype(F32) * dkk_ref[hh]).astype(BF16)
            state_ref[hh] = st * dch_ref[hh, :, 0:1] + _dot_tn(kd, vc)
            mu = jnp.mean(o, axis=-1, keepdims=True)
            d = o - mu
            var = jnp.mean(d * d, axis=-1, keepdims=True)
            on = d * lax.rsqrt(var + EPS) * gg_ref[:, hh * dv:(hh + 1) * dv] + gb_ref[:, hh * dv:(hh + 1) * dv]
            sg = sg_ref[rows, hh * dv:(hh + 1) * dv].astype(F32)
            o_ref[rows, hh * dv:(hh + 1) * dv] = (sg * on).astype(o_ref.dtype)


def retention_core(p, gn_g, gn_b, *, batch, seq, tb=512):
    T, N = p.shape
    H, C = RET_H, RET_C
    hdk = N // 6
    dk, dv = hdk // H, 2 * hdk // H
    hdv = H * dv
    nb = seq // tb
    log_gamma = jnp.log1p(-(2.0 ** (-5.0 - jnp.arange(H, dtype=F32))))
    i = jnp.arange(C, dtype=F32)
    rel = i[:, None] - i[None, :]
    d_intra = jnp.where(rel >= 0, jnp.exp(log_gamma[:, None, None] * jnp.maximum(rel, 0.0)), 0.0).astype(F32)
    d_q = jnp.exp(log_gamma[:, None] * (i + 1.0)).astype(F32)[..., None]
    d_k = jnp.exp(log_gamma[:, None] * (C - 1.0 - i)).astype(F32)[..., None]
    d_ch = jnp.broadcast_to(jnp.exp(log_gamma * C).astype(F32)[:, None, None], (H, 1, LANES))
    kern = functools.partial(_retention_kernel, H=H, C=C, dk=dk, dv=dv)
    return pl.pallas_call(
        kern,
        grid=(batch, nb),
        in_specs=[pl.BlockSpec((tb, hdk), lambda b, s: (b * nb + s, 0)),
                  pl.BlockSpec((tb, hdk), lambda b, s: (b * nb + s, 1)),
                  pl.BlockSpec((tb, hdv), lambda b, s: (b * nb + s, 1)),
                  pl.BlockSpec((tb, hdv), lambda b, s: (b * nb + s, 2)),
                  pl.BlockSpec((H, C, C), lambda b, s: (0, 0, 0)),
                  pl.BlockSpec((H, C, 1), lambda b, s: (0, 0, 0)),
                  pl.BlockSpec((H, C, 1), lambda b, s: (0, 0, 0)),
                  pl.BlockSpec((H, 1, LANES), lambda b, s: (0, 0, 0)),
                  pl.BlockSpec((1, hdv), lambda b, s: (0, 0)),
                  pl.BlockSpec((1, hdv), lambda b, s: (0, 0))],
        out_specs=pl.BlockSpec((tb, hdv), lambda b, s: (b * nb + s, 0)),
        out_shape=jax.ShapeDtypeStruct((T, hdv), BF16),
        scratch_shapes=[pltpu.VMEM((H, dk, dv), F32)],
        compiler_params=_params(("parallel", "arbitrary")),
        name="retention_core",
    )(p, p, p, p, d_intra, d_q, d_k, d_ch, gn_g.reshape(1, hdv), gn_b.reshape(1, hdv))


def _glu_kernel(x_ref, sc_ref, sh_ref, wa_ref, wg_ref, ba_ref, bg_ref, o_ref):
    h = (x_ref[...] * (1.0 + sc_ref[0]) + sh_ref[0]).astype(BF16)
    a = _dot(h, wa_ref[...]) + ba_ref[...]
    g = _dot(h, wg_ref[...]) + bg_ref[...]
    o_ref[...] = a * _sigmoid(g)


def conv_glu(x, sc, sh, w_pw1, b_pw1, *, seq, tm=512, tn=512):
    T, K = x.shape
    N = w_pw1.shape[1] // 2
    nb = seq // tm
    ng = N // tn
    b2 = b_pw1.reshape(1, 2 * N)
    return pl.pallas_call(
        _glu_kernel,
        grid=(T // tm, ng),
        in_specs=[pl.BlockSpec((tm, K), lambda i, j: (i, 0)),
                  pl.BlockSpec((1, 1, K), lambda i, j: (i // nb, 0, 0)),
                  pl.BlockSpec((1, 1, K), lambda i, j: (i // nb, 0, 0)),
                  pl.BlockSpec((K, tn), lambda i, j: (0, j)),
                  pl.BlockSpec((K, tn), lambda i, j: (0, j + ng)),
                  pl.BlockSpec((1, tn), lambda i, j: (0, j)),
                  pl.BlockSpec((1, tn), lambda i, j: (0, j + ng))],
        out_specs=pl.BlockSpec((tm, tn), lambda i, j: (i, j)),
        out_shape=jax.ShapeDtypeStruct((T, N), F32),
        compiler_params=_params(("parallel", "parallel")),
        name="conv_glu",
    )(x, sc, sh, w_pw1, w_pw1, b2, b2)


def _conv_tail_kernel(u_ref, up_ref, wdw_ref, bdw_ref, lg_ref, lb_ref, w2_ref, b2_ref,
                      res_ref, gate_ref, g_ref, bb_ref, o_ref, win_ref, *, nb, halo, width, alpha):
    tm = u_ref.shape[0]
    first = (pl.program_id(0) % nb) == 0
    prev = up_ref[...]
    win_ref[0:halo, :] = jnp.where(first, jnp.zeros_like(prev), prev)
    win_ref[halo:halo + tm, :] = u_ref[...]
    acc = jnp.zeros(u_ref.shape, F32)
    off = halo - (width - 1)
    for k in range(width):
        acc = acc + wdw_ref[k:k + 1, :] * win_ref[off + k:off + k + tm, :]
    acc = acc + bdw_ref[...]
    v = _ln_rows(acc, lg_ref[...], lb_ref[...])
    v = v * _sigmoid(v)
    y = _dot(v.astype(BF16), w2_ref[...]) + b2_ref[...]
    z = alpha * res_ref[...] + (1.0 + gate_ref[0]) * y
    o_ref[...] = _ln_rows(z, g_ref[...], bb_ref[...])


def conv_tail(u, w_dw, b_dw, ln_g, ln_b, w_pw2, b_pw2, res, gate, g, b, *, seq, alpha, tm=512):
    T, D = u.shape
    nb = seq // tm
    halo = 32
    assert CONV_W - 1 <= halo and tm % halo == 0
    r = tm // halo
    wpad = jnp.zeros((halo, D), F32).at[:CONV_W].set(w_dw)
    kern = functools.partial(_conv_tail_kernel, nb=nb, halo=halo, width=CONV_W, alpha=alpha)
    vec = lambda a: a.reshape(1, D)
    one = pl.BlockSpec((1, D), lambda i: (0, 0))
    return pl.pallas_call(
        kern,
        grid=(T // tm,),
        in_specs=[pl.BlockSpec((tm, D), lambda i: (i, 0)),
                  pl.BlockSpec((halo, D), lambda i: (jnp.maximum(i * r - 1, 0), 0)),
                  pl.BlockSpec((halo, D), lambda i: (0, 0)),
                  one, one, one,
                  pl.BlockSpec((D, D), lambda i: (0, 0)),
                  one,
                  pl.BlockSpec((tm, D), lambda i: (i, 0)),
                  pl.BlockSpec((1, 1, D), lambda i: (i // nb, 0, 0)),
                  one, one],
        out_specs=pl.BlockSpec((tm, D), lambda i: (i, 0)),
        out_shape=jax.ShapeDtypeStruct((T, D), F32),
        scratch_shapes=[pltpu.VMEM((halo + tm, D), F32)],
        compiler_params=_params(("parallel",)),
        name="conv_tail",
    )(u, u, wpad, vec(b_dw), vec(ln_g), vec(ln_b), w_pw2, vec(b_pw2), res, gate, vec(g), vec(b))


def _pool_kernel(x_ref, xp_ref, sc_ref, sh_ref, w_ref, pb_ref, ps_ref, gate_ref, g_ref, bb_ref,
                 o_ref, win_ref, *, nb, halo, wins, alpha):
    tm, D = x_ref.shape
    gw = D // len(wins)
    i = pl.program_id(0)
    first = (i % nb) == 0
    x = x_ref[...]
    sc, sh = sc_ref[0], sh_ref[0]
    hp = xp_ref[...] * (1.0 + sc) + sh
    win_ref[0:halo, :] = jnp.where(first, jnp.zeros_like(hp), hp)
    win_ref[halo:halo + tm, :] = x * (1.0 + sc) + sh
    t = (i % nb) * tm + lax.broadcasted_iota(I32, (tm, gw), 0)
    ys = []
    for gi, wn in enumerate(wins):
        cols = slice(gi * gw, (gi + 1) * gw)
        s = win_ref[halo:halo + tm, cols]
        hcur = s
        for k in range(1, wn):
            s = s + win_ref[halo - k:halo - k + tm, cols]
        cnt = jnp.minimum(t + 1, wn).astype(F32)
        pooled = s / cnt - hcur
        ys.append(_dot(pooled.astype(BF16), w_ref[gi]))
    y = (jnp.concatenate(ys, axis=-1) + pb_ref[...]) * ps_ref[...]
    z = alpha * x + (1.0 + gate_ref[0]) * y
    o_ref[...] = _ln_rows(z, g_ref[...], bb_ref[...])


def pool_layer(x, sc, sh, pool_w, pool_b, pool_scale, gate, g, b, *, seq, alpha, tm=512):
    T, D = x.shape
    nb = seq // tm
    halo = 16
    assert max(POOL_WINS) <= halo and tm % halo == 0
    r = tm // halo
    G, gw, _ = pool_w.shape
    kern = functools.partial(_pool_kernel, nb=nb, halo=halo, wins=POOL_WINS, alpha=alpha)
    vec = lambda a: a.reshape(1, D)
    one = pl.BlockSpec((1, D), lambda i: (0, 0))
    per_b = pl.BlockSpec((1, 1, D), lambda i: (i // nb, 0, 0))
    return pl.pallas_call(
        kern,
        grid=(T // tm,),
        in_specs=[pl.BlockSpec((tm, D), lambda i: (i, 0)),
                  pl.BlockSpec((halo, D), lambda i: (jnp.maximum(i * r - 1, 0), 0)),
                  per_b, per_b,
                  pl.BlockSpec((G, gw, gw), lambda i: (0, 0, 0)),
                  one, one, per_b, one, one],
        out_specs=pl.BlockSpec((tm, D), lambda i: (i, 0)),
        out_shape=jax.ShapeDtypeStruct((T, D), F32),
        scratch_shapes=[pltpu.VMEM((halo + tm, D), F32)],
        compiler_params=_params(("parallel",)),
        name="pool_layer",
    )(x, x, sc, sh, pool_w.astype(BF16), vec(pool_b), vec(pool_scale), gate, vec(g), vec(b))


def _rope64(x, cos, sin_signed, lane):
    partner = jnp.where((lane & (NSA_DH - 1)) < NSA_DH // 2, pltpu.roll(x, LANES - NSA_DH // 2, 1), pltpu.roll(x, NSA_DH // 2, 1))
    return x * cos + partner * sin_signed


def _nsa_proj_kernel(x_ref, sc_ref, sh_ref, w_ref, cos_ref, sin_ref, gb_ref, q_ref, kv_ref, gt_ref, *, nq, qscale):
    h = x_ref[...] * (1.0 + sc_ref[0]) + sh_ref[0]
    acc = _dot(h.astype(BF16), w_ref[...])
    cos, sin = cos_ref[...], sin_ref[...]
    lane = lax.broadcasted_iota(I32, cos.shape, 1)
    for s in range(nq):
        blk = acc[:, s * LANES:(s + 1) * LANES]
        q_ref[:, s * LANES:(s + 1) * LANES] = (_rope64(blk, cos, sin, lane) * qscale).astype(q_ref.dtype)
    for s in range(6):
        blk = acc[:, (nq + s) * LANES:(nq + s + 1) * LANES]
        if s % 2 == 0:
            blk = _rope64(blk, cos, sin, lane)
        kv_ref[:, s * LANES:(s + 1) * LANES] = blk.astype(kv_ref.dtype)
    gl = acc[:, (nq + 6) * LANES:(nq + 7) * LANES]
    gt_ref[...] = _sigmoid(gl + gb_ref[...])


def nsa_projection(x, sc, sh, w_in_p, cos, sin_signed, gate_b_p, *, seq, tm=512):
    T, K = x.shape
    Np = w_in_p.shape[1]
    nq = NSA_H * NSA_DH // LANES
    nb = seq // tm
    kern = functools.partial(_nsa_proj_kernel, nq=nq, qscale=NSA_DH ** -0.5)
    per_b = pl.BlockSpec((1, 1, K), lambda i: (i // nb, 0, 0))
    return pl.pallas_call(
        kern,
        grid=(T // tm,),
        in_specs=[pl.BlockSpec((tm, K), lambda i: (i, 0)), per_b, per_b,
                  pl.BlockSpec((K, Np), lambda i: (0, 0)),
                  pl.BlockSpec((tm, LANES), lambda i: (i, 0)),
                  pl.BlockSpec((tm, LANES), lambda i: (i, 0)),
                  pl.BlockSpec((1, LANES), lambda i: (0, 0))],
        out_specs=[pl.BlockSpec((tm, nq * LANES), lambda i: (i, 0)),
                   pl.BlockSpec((tm, 6 * LANES), lambda i: (i, 0)),
                   pl.BlockSpec((tm, LANES), lambda i: (i, 0))],
        out_shape=[jax.ShapeDtypeStruct((T, nq * LANES), BF16),
                   jax.ShapeDtypeStruct((T, 6 * LANES), BF16),
                   jax.ShapeDtypeStruct((T, LANES), F32)],
        compiler_params=_params(("parallel",)),
        name="nsa_projection",
    )(x, sc, sh, w_in_p, cos, sin_signed, gate_b_p)


def _gelu_tanh(v):
    return 0.5 * v * (1.0 + jnp.tanh(math.sqrt(2.0 / math.pi) * (v + 0.044715 * v * v * v)))


def _compress_kernel(xk_ref, xv_ref, pk_ref, pv_ref, kw1_ref, kw2_ref, vw1_ref, vw2_ref, ok_ref, ov_ref):
    def one(x_ref, pe_ref, w1_ref, w2_ref, o_ref):
        x = x_ref[0].astype(F32)
        n = x.shape[0]
        half = x.shape[1]
        a = _dot((x + pe_ref[0:1, :]).astype(BF16), w1_ref[0:half, :])
        b = _dot((x + pe_ref[1:2, :]).astype(BF16), w1_ref[half:2 * half, :])
        pre = a + pltpu.roll(b, n - 1, 0)
        o_ref[0] = _dot(_gelu_tanh(pre).astype(BF16), w2_ref[...]).astype(o_ref.dtype)

    one(xk_ref, pk_ref, kw1_ref, kw2_ref, ok_ref)
    one(xv_ref, pv_ref, vw1_ref, vw2_ref, ov_ref)


def nsa_compress(kc_chunks, vc_chunks, pos_k, pos_v, k_w1, k_w2, v_w1, v_w2):
    BG, n, width = kc_chunks.shape
    hid = k_w1.shape[1]
    dh = k_w2.shape[1]
    full = lambda a: pl.BlockSpec(a.shape, lambda i: (0,) * a.ndim)
    pk = pos_k.reshape(2, width)
    pv = pos_v.reshape(2, width)
    ws = [k_w1.astype(BF16), k_w2.astype(BF16), v_w1.astype(BF16), v_w2.astype(BF16)]
    del hid
    return pl.pallas_call(
        _compress_kernel,
        grid=(BG,),
        in_specs=[pl.BlockSpec((1, n, width), lambda i: (i, 0, 0)),
                  pl.BlockSpec((1, n, width), lambda i: (i, 0, 0)),
                  full(pk), full(pv)] + [full(w) for w in ws],
        out_specs=[pl.BlockSpec((1, n, dh), lambda i: (i, 0, 0)),
                   pl.BlockSpec((1, n, dh), lambda i: (i, 0, 0))],
        out_shape=[jax.ShapeDtypeStruct((BG, n, dh), BF16), jax.ShapeDtypeStruct((BG, n, dh), BF16)],
        compiler_params=_params(("parallel",)),
        name="nsa_compress",
    )(kc_chunks, vc_chunks, pk, pv, *ws)


def _nsa_attn_kernel(q_ref, gt_ref, kc_ref, vcl_ref, vch_ref, ks_ref, vsl_ref, vsh_ref, kw_ref, vwl_ref, vwh_ref,
                     ov_ref, o_ref, qs_ref, m_ref, l_ref, ae_ref, ao_ref, res_ref,
                     *, qt, ck, n_sel, top_n, sel_shift, cmp_len, cmp_step, window):
    g = pl.program_id(1)
    i = pl.program_id(2)
    t0 = i * qt
    R = q_ref.shape[2] // NSA_DH
    rows = R * qt
    half_rows = rows // 2
    lane = lax.broadcasted_iota(I32, (qt, LANES), 1)
    lane_f = lane.astype(F32)

    for p in range(R // 2):
        slab = q_ref[0, :, p * LANES:(p + 1) * LANES]
        qs_ref[p * qt:(p + 1) * qt, :] = jnp.where(lane < NSA_DH, slab, jnp.zeros_like(slab))
        qs_ref[half_rows + p * qt:half_rows + (p + 1) * qt, :] = jnp.where(lane >= NSA_DH, slab, jnp.zeros_like(slab))
    qs = qs_ref[...]
    tq = t0 + lax.broadcasted_iota(I32, (qt, 1), 0)
    tq_all = jnp.concatenate([tq] * R, axis=0)

    ncmp = kc_ref.shape[1]
    s = _dot_nt(qs, kc_ref[0])
    cmp_end = lax.broadcasted_iota(I32, (rows, ncmp), 1) * cmp_step + (cmp_len - 1)
    valid_c = cmp_end <= tq_all
    s = jnp.where(valid_c, s, NEG_BIG)
    mx = jnp.max(s, axis=-1, keepdims=True)
    e = jnp.exp(s - mx)
    pc = jnp.where(valid_c, e / jnp.sum(e, axis=-1, keepdims=True), 0.0)
    pcb = pc.astype(BF16)
    res_ref[0, 0:half_rows, :] = _dot(pcb[0:half_rows], vcl_ref[0])
    res_ref[0, half_rows:rows, :] = _dot(pcb[half_rows:], vch_ref[0])
    psum = pc[0:qt]
    for r in range(1, R):
        psum = psum + pc[r * qt:(r + 1) * qt]
    imp = _dot(psum.astype(BF16), ov_ref[...])

    cur = lax.shift_right_logical(tq, sel_shift)
    dcur = cur - lane
    forced = (lane == 0) | (dcur == 0) | (dcur == 1)
    score = jnp.where(lax.shift_left(lane, sel_shift) <= tq, jnp.where(forced, NSA_FORCED, imp), NEG_BIG)
    score = jnp.where(lane < n_sel, score, -jnp.inf)
    sel = jnp.zeros((qt, LANES), F32)
    for _ in range(top_n):
        mx = jnp.max(score, axis=-1, keepdims=True)
        idx = jnp.min(jnp.where(score == mx, lane_f, float(LANES)), axis=-1, keepdims=True)
        hit = lane_f == idx
        sel = jnp.where(hit, jnp.where(mx > 0.5 * NEG_BIG, 1.0, sel), sel)
        score = jnp.where(hit, -jnp.inf, score)
    selb = sel.astype(BF16)

    def attend(k_ref, vl_ref, vh_ref, c_lo, c_hi, mask_fn, slot):
        m_ref[...] = jnp.full(m_ref.shape, NEG_BIG, F32)
        l_ref[...] = jnp.zeros(l_ref.shape, F32)
        ae_ref[...] = jnp.zeros(ae_ref.shape, F32)
        ao_ref[...] = jnp.zeros(ao_ref.shape, F32)

        def body(c, carry):
            start = pl.multiple_of(c * ck, ck)
            allowed = mask_fn(start)
            sc = _dot_nt(qs, k_ref[0, pl.ds(start, ck), :])
            sc = jnp.where(allowed, sc, NEG_BIG)
            m_old = m_ref[...]
            m_new = jnp.maximum(m_old, jnp.max(sc, axis=-1, keepdims=True))
            alpha = jnp.exp(m_old - m_new)
            p = jnp.where(allowed, jnp.exp(sc - m_new), 0.0)
            l_ref[...] = alpha * l_ref[...] + jnp.sum(p, axis=-1, keepdims=True)
            m_ref[...] = m_new
            pb = p.astype(BF16)
            ae_ref[...] = alpha[0:half_rows] * ae_ref[...] + _dot(pb[0:half_rows], vl_ref[0, pl.ds(start, ck), :])
            ao_ref[...] = alpha[half_rows:] * ao_ref[...] + _dot(pb[half_rows:], vh_ref[0, pl.ds(start, ck), :])
            return carry

        lax.fori_loop(c_lo, c_hi, body, 0)
        inv = 1.0 / l_ref[...]
        res_ref[slot, 0:half_rows, :] = ae_ref[...] * inv[0:half_rows]
        res_ref[slot, half_rows:rows, :] = ao_ref[...] * inv[half_rows:]

    def sel_mask(start):
        jrow = lax.broadcasted_iota(I32, (LANES, ck), 0)
        jcol = lax.shift_right_logical(start + lax.broadcasted_iota(I32, (LANES, ck), 1), sel_shift)
        expand = jnp.where(jrow == jcol, 1.0, 0.0).astype(BF16)
        picked = _dot(selb, expand)
        picked_all = jnp.concatenate([picked] * R, axis=0)
        dist = tq_all - (start + lax.broadcasted_iota(I32, (rows, ck), 1))
        return jnp.where(dist >= 0, picked_all, 0.0) > 0.5

    def win_mask(start):
        dist = tq_all - (start + lax.broadcasted_iota(I32, (rows, ck), 1))
        return jnp.where(dist >= 0, dist, window) < window

    c_hi = (t0 + qt + ck - 1) // ck
    attend(ks_ref, vsl_ref, vsh_ref, 0, c_hi, sel_mask, 1)
    attend(kw_ref, vwl_ref, vwh_ref, jnp.maximum(t0 - window + 1, 0) // ck, c_hi, win_mask, 2)

    gt = gt_ref[...]
    for p in range(R // 2):
        acc = jnp.zeros((qt, LANES), F32)
        for par in range(2):
            hidx = g * R + 2 * p + par
            rsl = slice(par * half_rows + p * qt, par * half_rows + (p + 1) * qt)
            for br in range(3):
                col = jnp.sum(jnp.where(lane == hidx * 3 + br, gt, 0.0), axis=-1, keepdims=True)
                acc = acc + col * res_ref[br, rsl, :]
        o_ref[0, :, p * LANES:(p + 1) * LANES] = acc.astype(o_ref.dtype)


def nsa_attention(q, gates, kcd, vcl, vch, ksd, vsl, vsh, kwd, vwl, vwh, *, batch, seq, qt=256, ck=512):
    G = NSA_G
    R = NSA_H // G
    gq = R * NSA_DH
    n_sel = seq // NSA_SEL_LEN
    ncmp = kcd.shape[1]
    nq = seq // qt
    ov = np.zeros((ncmp, LANES), np.float32)
    for n in range(ncmp):
        for j in range(n_sel):
            if n * NSA_CMP_STEP < (j + 1) * NSA_SEL_LEN and n * NSA_CMP_STEP + NSA_CMP_LEN > j * NSA_SEL_LEN:
                ov[n, j] = 1.0
    n_valid = (seq - NSA_CMP_LEN) // NSA_CMP_STEP + 1
    ov[n_valid:] = 0.0
    kern = functools.partial(_nsa_attn_kernel, qt=qt, ck=ck, n_sel=n_sel, top_n=min(NSA_TOPN, n_sel),
                             sel_shift=int(math.log2(NSA_SEL_LEN)), cmp_len=NSA_CMP_LEN, cmp_step=NSA_CMP_STEP, window=NSA_WIN)
    cmp_spec = pl.BlockSpec((1, ncmp, LANES), lambda b, g, i: (b * G + g, 0, 0))
    seq_spec = pl.BlockSpec((1, seq, LANES), lambda b, g, i: (b * G + g, 0, 0))
    rows = R * qt
    return pl.pallas_call(
        kern,
        grid=(batch, G, nq),
        in_specs=[pl.BlockSpec((1, qt, gq), lambda b, g, i: (b, i, g)),
                  pl.BlockSpec((qt, LANES), lambda b, g, i: (b * nq + i, 0)),
                  cmp_spec, cmp_spec, cmp_spec,
                  seq_spec, seq_spec, seq_spec, seq_spec, seq_spec, seq_spec,
                  pl.BlockSpec((ncmp, LANES), lambda b, g, i: (0, 0))],
        out_specs=pl.BlockSpec((1, qt, gq), lambda b, g, i: (b, i, g)),
        out_shape=jax.ShapeDtypeStruct((batch, seq, NSA_H * NSA_DH), BF16),
        scratch_shapes=[pltpu.VMEM((rows, LANES), BF16),
                        pltpu.VMEM((rows, 1), F32), pltpu.VMEM((rows, 1), F32),
                        pltpu.VMEM((rows // 2, LANES), F32), pltpu.VMEM((rows // 2, LANES), F32),
                        pltpu.VMEM((3, rows, LANES), F32)],
        compiler_params=_params(("parallel", "parallel", "arbitrary")),
        name="nsa_attention",
    )(q, gates, kcd, vcl, vch, ksd, vsl, vsh, kwd, vwl, vwh, jnp.asarray(ov, BF16))


def _router_kernel(x_ref, sc_ref, sh_ref, w_ref, b_ref, h_ref, idx_ref, gate_ref, *, top_k):
    h = (x_ref[...] * (1.0 + sc_ref[0]) + sh_ref[0]).astype(BF16)
    h_ref[...] = h
    logits = _dot(h, w_ref[...]) + b_ref[...]
    lane = lax.broadcasted_iota(I32, logits.shape, 1)
    lane_f = lane.astype(F32)
    idx_out = jnp.zeros(logits.shape, I32)
    val_out = jnp.full(logits.shape, NEG_BIG, F32)
    for k in range(top_k):
        mx = jnp.max(logits, axis=-1, keepdims=True)
        idx = jnp.min(jnp.where(logits == mx, lane_f, float(LANES)), axis=-1, keepdims=True)
        idx_out = jnp.where(lane == k, idx.astype(I32), idx_out)
        val_out = jnp.where(lane == k, mx, val_out)
        logits = jnp.where(lane_f == idx, -jnp.inf, logits)
    e = jnp.exp(val_out - jnp.max(val_out, axis=-1, keepdims=True))
    e = jnp.where(lane < top_k, e, 0.0)
    idx_ref[...] = idx_out
    gate_ref[...] = e / jnp.sum(e, axis=-1, keepdims=True)


def moe_router(x, sc, sh, router_w, router_b, *, seq, tm=512):
    T, D = x.shape
    E = router_w.shape[1]
    nb = seq // tm
    wp = jnp.zeros((D, LANES), BF16).at[:, :E].set(router_w.astype(BF16))
    bp = jnp.full((1, LANES), NEG_BIG, F32).at[0, :E].set(router_b)
    per_b = pl.BlockSpec((1, 1, D), lambda i: (i // nb, 0, 0))
    return pl.pallas_call(
        functools.partial(_router_kernel, top_k=MOE_K),
        grid=(T // tm,),
        in_specs=[pl.BlockSpec((tm, D), lambda i: (i, 0)), per_b, per_b,
                  pl.BlockSpec((D, LANES), lambda i: (0, 0)),
                  pl.BlockSpec((1, LANES), lambda i: (0, 0))],
        out_specs=[pl.BlockSpec((tm, D), lambda i: (i, 0)),
                   pl.BlockSpec((tm, LANES), lambda i: (i, 0)),
                   pl.BlockSpec((tm, LANES), lambda i: (i, 0))],
        out_shape=[jax.ShapeDtypeStruct((T, D), BF16),
                   jax.ShapeDtypeStruct((T, LANES), I32),
                   jax.ShapeDtypeStruct((T, LANES), F32)],
        compiler_params=_params(("parallel",)),
        name="moe_router",
    )(x, sc, sh, wp, bp)


def _expert_kernel(be_ref, nu_ref, x_ref, sw_ref, w1_ref, b1_ref, w2_ref, b2_ref, o_ref, *, dff):
    i = pl.program_id(0)

    @pl.when(i < nu_ref[0])
    def _():
        gu = _dot(x_ref[...], w1_ref[0]) + b1_ref[0]
        gg = jnp.minimum(gu[:, :dff], GLU_LIMIT)
        up = jnp.clip(gu[:, dff:], -GLU_LIMIT, GLU_LIMIT)
        act = (up + 1.0) * (gg * _sigmoid(GLU_ALPHA * gg))
        y = _dot(act.astype(BF16), w2_ref[0]) + b2_ref[0]
        o_ref[...] = (y * sw_ref[...]).astype(o_ref.dtype)

    @pl.when(i >= nu_ref[0])
    def _():
        o_ref[...] = jnp.zeros_like(o_ref)


def moe_experts(xs, slot_w, block_expert, n_used, w1, b1, w2, b2):
    n_slots, D = xs.shape
    E, _, two_f = w1.shape
    nblk = n_slots // MOE_BLK
    grid_spec = pltpu.PrefetchScalarGridSpec(
        num_scalar_prefetch=2,
        grid=(nblk,),
        in_specs=[pl.BlockSpec((MOE_BLK, D), lambda i, be, nu: (i, 0)),
                  pl.BlockSpec((MOE_BLK, 1), lambda i, be, nu: (i, 0)),
                  pl.BlockSpec((1, D, two_f), lambda i, be, nu: (be[i], 0, 0)),
                  pl.BlockSpec((1, 1, two_f), lambda i, be, nu: (be[i], 0, 0)),
                  pl.BlockSpec((1, two_f // 2, D), lambda i, be, nu: (be[i], 0, 0)),
                  pl.BlockSpec((1, 1, D), lambda i, be, nu: (be[i], 0, 0))],
        out_specs=pl.BlockSpec((MOE_BLK, D), lambda i, be, nu: (i, 0)),
    )
    return pl.pallas_call(
        functools.partial(_expert_kernel, dff=two_f // 2),
        grid_spec=grid_spec,
        out_shape=jax.ShapeDtypeStruct((n_slots, D), F32),
        compiler_params=_params(("arbitrary",)),
        name="moe_experts",
    )(block_expert, n_used, xs, slot_w.reshape(n_slots, 1), w1, b1.reshape(E, 1, two_f), w2, b2.reshape(E, 1, D))


def _combine_kernel(y_ref, res_ref, gate_ref, g_ref, bb_ref, o_ref, *, alpha):
    y = y_ref[:, 0, :]
    for k in range(1, y_ref.shape[1]):
        y = y + y_ref[:, k, :]
    z = alpha * res_ref[...] + (1.0 + gate_ref[0]) * y
    o_ref[...] = _ln_rows(z, g_ref[...], bb_ref[...])


def moe_combine(y4, res, gate, g, b, *, seq, alpha, tm=256):
    T, K, D = y4.shape
    nb = seq // tm
    one = pl.BlockSpec((1, D), lambda i: (0, 0))
    return pl.pallas_call(
        functools.partial(_combine_kernel, alpha=alpha),
        grid=(T // tm,),
        in_specs=[pl.BlockSpec((tm, K, D), lambda i: (i, 0, 0)),
                  pl.BlockSpec((tm, D), lambda i: (i, 0)),
                  pl.BlockSpec((1, 1, D), lambda i: (i // nb, 0, 0)), one, one],
        out_specs=pl.BlockSpec((tm, D), lambda i: (i, 0)),
        out_shape=jax.ShapeDtypeStruct((T, D), F32),
        compiler_params=_params(("parallel",)),
        name="moe_combine",
    )(y4, res, gate, g.reshape(1, D), b.reshape(1, D))


def moe_layer(x, sc, sh, gate, ln_g, ln_b, router_w, router_b, w1, b1, w2, b2, *, seq, alpha):
    T, D = x.shape
    E, K = MOE_E, MOE_K
    h, idx_p, gate_p = moe_router(x, sc, sh, router_w, router_b, seq=seq)
    top_i = idx_p[:, :K]
    top_g = gate_p[:, :K]
    flat_e = top_i.reshape(-1)
    pair = jnp.arange(T * K, dtype=I32)
    sorted_keys = jnp.sort(flat_e * (T * K) + pair)
    order = sorted_keys % (T * K)
    counts = jnp.sum(flat_e[:, None] == jnp.arange(E, dtype=I32)[None, :], axis=0, dtype=I32)
    starts = jnp.cumsum(counts) - counts
    blocks_per = (counts + MOE_BLK - 1) // MOE_BLK
    block_end = jnp.cumsum(blocks_per)
    pad_start = (block_end - blocks_per) * MOE_BLK
    n_blocks = -(-(T * K) // MOE_BLK) + E
    n_slots = n_blocks * MOE_BLK
    block_expert = jnp.minimum(jnp.searchsorted(block_end, jnp.arange(n_blocks, dtype=I32), side="right"), E - 1).astype(I32)
    slot = jnp.arange(n_slots, dtype=I32)
    se = block_expert[slot // MOE_BLK]
    r = slot - pad_start[se]
    valid = (r >= 0) & (r < counts[se]) & (slot // MOE_BLK < block_end[-1])
    src = jnp.clip(starts[se] + r, 0, T * K - 1)
    spair = order[src]
    slot_tok = jnp.where(valid, spair // K, 0)
    slot_w = jnp.where(valid, top_g.reshape(-1)[spair], 0.0)
    pair_slot = jnp.zeros((T * K,), I32).at[jnp.where(valid, spair, T * K)].set(slot, mode="drop")
    xs = jnp.take(h, slot_tok, axis=0)
    out = moe_experts(xs, slot_w, block_expert, block_end[-1:].astype(I32), w1, b1, w2, b2)
    y4 = jnp.take(out, pair_slot, axis=0).reshape(T, K, D)
    return moe_combine(y4, x, gate, ln_g, ln_b, seq=seq, alpha=alpha)


def _rope_tables(positions, d):
    inv = ROPE_BASE ** (-jnp.arange(0, d, 2, dtype=F32) / d)
    ang = positions.astype(F32).reshape(-1)[:, None] * inv
    return jnp.cos(ang), jnp.sin(ang)


def kernel(x, c, positions, ada_w, ada_b, ln1_g, ln1_b, ln2_g, ln2_b, router_w, router_b, moe_w1, moe_b1, moe_w2, moe_b2, ret_w_in, ret_gn_g, ret_gn_b, ret_w_out, conv_w_pw1, conv_b_pw1, conv_w_dw, conv_b_dw, conv_ln_g, conv_ln_b, conv_w_pw2, conv_b_pw2, nsa_w_in, nsa_gate_b, nsa_cmp_pos_k, nsa_cmp_pos_v, nsa_cmp_k_w1, nsa_cmp_k_w2, nsa_cmp_v_w1, nsa_cmp_v_w2, nsa_w_out, pool_w, pool_b, pool_scale):
    B, S, D = x.shape
    T = B * S
    depth = ada_w.shape[0]
    alpha = (2.0 * depth) ** 0.25
    mod = ada_modulation(c, ada_w, ada_b)
    xf = x.reshape(T, D)
    for i in range(depth):
        kind, j = i % 4, i // 4
        sh1, sc1, g1, sh2, sc2, g2 = (mod[i, :, k * D:(k + 1) * D].reshape(B, 1, D) for k in range(6))
        res1 = (g1, ln1_g[i], ln1_b[i])
        if kind == 0:
            dk = ret_w_in.shape[2] // 6 // RET_H
            cos, sin = _rope_tables(positions, dk)
            p = ret_projection(xf, sc1, sh1, ret_w_in[j].astype(BF16), cos, sin, seq=S)
            o = retention_core(p, ret_gn_g[j], ret_gn_b[j], batch=B, seq=S)
            xf = fused_matmul(o, ret_w_out[j].astype(BF16), seq=S, resln=(xf,) + res1, alpha=alpha, name="ret_out")
        elif kind == 1:
            u = conv_glu(xf, sc1, sh1, conv_w_pw1[j].astype(BF16), conv_b_pw1[j], seq=S)
            xf = conv_tail(u, conv_w_dw[j], conv_b_dw[j], conv_ln_g[j], conv_ln_b[j], conv_w_pw2[j].astype(BF16),
                           conv_b_pw2[j], xf, *res1, seq=S, alpha=alpha)
        elif kind == 2:
            G, dh, H = NSA_G, NSA_DH, NSA_H
            cos, sin = _rope_tables(positions, dh)
            cos_p = jnp.tile(cos, (1, LANES // (dh // 2)))
            sin_p = jnp.tile(jnp.concatenate([-sin, sin], axis=1), (1, LANES // dh))
            n_in = nsa_w_in.shape[2]
            n_pad = -(-n_in // LANES) * LANES
            w_p = jnp.zeros((D, n_pad), BF16).at[:, :n_in].set(nsa_w_in[j].astype(BF16))
            gb_p = jnp.zeros((1, LANES), F32).at[0, :3 * H].set(nsa_gate_b[j])
            q, kv, gates = nsa_projection(xf, sc1, sh1, w_p, cos_p, sin_p, gb_p, seq=S)
            kv = kv.reshape(B, S, 6, G, dh).transpose(2, 0, 3, 1, 4).reshape(6, B * G, S, dh)
            kc, vc, ks, vs, kw, vw = (kv[k] for k in range(6))
            chunk = lambda a: a.reshape(B * G, S // NSA_CMP_STEP, NSA_CMP_STEP * dh)
            k_cmp, v_cmp = nsa_compress(chunk(kc), chunk(vc), nsa_cmp_pos_k[j], nsa_cmp_pos_v[j], nsa_cmp_k_w1[j],
                                        nsa_cmp_k_w2[j], nsa_cmp_v_w1[j], nsa_cmp_v_w2[j])
            dup = lambda a: jnp.concatenate([a, a], axis=-1)
            lo = lambda a: jnp.concatenate([a, jnp.zeros_like(a)], axis=-1)
            hi = lambda a: jnp.concatenate([jnp.zeros_like(a), a], axis=-1)
            o = nsa_attention(q.reshape(B, S, H * dh), gates, dup(k_cmp), lo(v_cmp), hi(v_cmp), dup(ks), lo(vs), hi(vs),
                              dup(kw), lo(vw), hi(vw), batch=B, seq=S)
            xf = fused_matmul(o.reshape(T, H * dh), nsa_w_out[j].astype(BF16), seq=S, resln=(xf,) + res1, alpha=alpha,
                              name="nsa_out")
        else:
            xf = pool_layer(xf, sc1, sh1, pool_w[j], pool_b[j], pool_scale[j], *res1, seq=S, alpha=alpha)
        xf = moe_layer(xf, sc2, sh2, g2, ln2_g[i], ln2_b[i], router_w[i], router_b[i], moe_w1[i].astype(BF16), moe_b1[i],
                       moe_w2[i].astype(BF16), moe_b2[i], seq=S, alpha=alpha)
    return xf.reshape(B, S, D)
```

```python
import functools
import math

import jax
import jax.numpy as jnp
import numpy as np
from jax import lax
from jax.experimental import pallas as pl
from jax.experimental.pallas import tpu as pltpu

F32 = jnp.float32
BF16 = jnp.bfloat16
I32 = jnp.int32

ROPE_BASE = 10000.0
EPS = 1e-5
NEG_BIG = -1e30

RET_H = 4
RET_C = 128
CONV_W = 31
NSA_H = 16
NSA_G = 2
NSA_DH = 64
NSA_CMP_LEN = 32
NSA_CMP_STEP = 16
NSA_SEL_LEN = 64
NSA_TOPN = 16
NSA_WIN = 512
NSA_FORCED = 1e6
POOL_WINS = (2, 4, 8, 16)
MOE_E = 32
MOE_K = 4
MOE_BLK = 256
GLU_LIMIT = 7.0
GLU_ALPHA = 1.702

V7X_VMEM_LIMIT = 56 * 1024 * 1024
LANES = 128


def _params(sem):
    return pltpu.CompilerParams(dimension_semantics=sem, vmem_limit_bytes=V7X_VMEM_LIMIT)


def _ln_rows(v, g, b):
    mu = jnp.mean(v, axis=-1, keepdims=True)
    d = v - mu
    var = jnp.mean(d * d, axis=-1, keepdims=True)
    return d * lax.rsqrt(var + EPS) * g + b


def _sigmoid(v):
    return 1.0 / (1.0 + jnp.exp(-v))


def _dot(a, b):
    return jnp.dot(a, b, preferred_element_type=F32)


def _dot_nt(a, b):
    return lax.dot_general(a, b, (((1,), (1,)), ((), ())), preferred_element_type=F32)


def _dot_tn(a, b):
    return lax.dot_general(a, b, (((0,), (0,)), ((), ())), preferred_element_type=F32)


def _ada_kernel(c_ref, w_ref, b_ref, o_ref):
    c = c_ref[...]
    ca = c * _sigmoid(c)
    o_ref[0] = _dot(ca.astype(BF16), w_ref[0].astype(BF16)) + b_ref[0]


def ada_modulation(c, ada_w, ada_b):
    L, D, N = ada_w.shape
    B = c.shape[0]
    tn = 2048 if N % 2048 == 0 else N
    return pl.pallas_call(
        _ada_kernel,
        grid=(L, N // tn),
        in_specs=[pl.BlockSpec((B, D), lambda l, j: (0, 0)),
                  pl.BlockSpec((1, D, tn), lambda l, j: (l, 0, j)),
                  pl.BlockSpec((1, 1, tn), lambda l, j: (l, 0, j))],
        out_specs=pl.BlockSpec((1, B, tn), lambda l, j: (l, 0, j)),
        out_shape=jax.ShapeDtypeStruct((L, B, N), F32),
        compiler_params=_params(("parallel", "parallel")),
        name="ada_modulation",
    )(c, ada_w, ada_b.reshape(L, 1, N))


def _mm_kernel(*refs, has_mod, has_bias, epi, alpha):
    it = iter(refs)
    x_ref = next(it)
    if has_mod:
        sc_ref, sh_ref = next(it), next(it)
    w_ref = next(it)
    if has_bias:
        b_ref = next(it)
    if epi == "resln":
        res_ref, gate_ref, g_ref, bb_ref = next(it), next(it), next(it), next(it)
    o_ref = next(it)
    x = x_ref[...]
    if has_mod:
        x = x.astype(F32) * (1.0 + sc_ref[0]) + sh_ref[0]
    acc = _dot(x.astype(BF16), w_ref[...])
    if has_bias:
        acc = acc + b_ref[...]
    if epi == "resln":
        v = alpha * res_ref[...] + (1.0 + gate_ref[0]) * acc
        acc = _ln_rows(v, g_ref[...], bb_ref[...])
    o_ref[...] = acc.astype(o_ref.dtype)


def fused_matmul(x, w, *, seq, mod=None, bias=None, resln=None, alpha=1.0, out_dtype=F32, tm=512, tn=None, name="mm"):
    T, K = x.shape
    N = w.shape[1]
    tn = N if tn is None else tn
    assert T % tm == 0 and seq % tm == 0 and N % tn == 0
    nb = seq // tm
    args, specs = [x], [pl.BlockSpec((tm, K), lambda i, j: (i, 0))]
    if mod is not None:
        for m in mod:
            args.append(m)
            specs.append(pl.BlockSpec((1, 1, K), lambda i, j: (i // nb, 0, 0)))
    args.append(w)
    specs.append(pl.BlockSpec((K, tn), lambda i, j: (0, j)))
    if bias is not None:
        args.append(bias.reshape(1, N))
        specs.append(pl.BlockSpec((1, tn), lambda i, j: (0, j)))
    epi = "none"
    if resln is not None:
        assert tn == N
        res, gate, g, b = resln
        epi = "resln"
        args += [res, gate, g.reshape(1, N), b.reshape(1, N)]
        specs += [pl.BlockSpec((tm, N), lambda i, j: (i, 0)),
                  pl.BlockSpec((1, 1, N), lambda i, j: (i // nb, 0, 0)),
                  pl.BlockSpec((1, N), lambda i, j: (0, 0)),
                  pl.BlockSpec((1, N), lambda i, j: (0, 0))]
    kern = functools.partial(_mm_kernel, has_mod=mod is not None, has_bias=bias is not None, epi=epi, alpha=alpha)
    return pl.pallas_call(
        kern,
        grid=(T // tm, N // tn),
        in_specs=specs,
        out_specs=pl.BlockSpec((tm, tn), lambda i, j: (i, j)),
        out_shape=jax.ShapeDtypeStruct((T, N), out_dtype),
        compiler_params=_params(("parallel", "parallel")),
        name=name,
    )(*args)


def _ret_proj_kernel(x_ref, sc_ref, sh_ref, w_ref, cos_ref, sin_ref, o_ref, *, dk, n_qk, n_v, kscale):
    j = pl.program_id(1)
    h = x_ref[...] * (1.0 + sc_ref[0]) + sh_ref[0]
    acc = _dot(h.astype(BF16), w_ref[...])
    half = dk // 2

    def rope(scale):
        cos, sin = cos_ref[...], sin_ref[...]
        for hh in range(acc.shape[1] // dk):
            x1 = acc[:, hh * dk:hh * dk + half]
            x2 = acc[:, hh * dk + half:(hh + 1) * dk]
            o_ref[:, hh * dk:hh * dk + half] = ((x1 * cos - x2 * sin) * scale).astype(o_ref.dtype)
            o_ref[:, hh * dk + half:(hh + 1) * dk] = ((x1 * sin + x2 * cos) * scale).astype(o_ref.dtype)

    @pl.when(j < n_qk)
    def _():
        rope(1.0)

    @pl.when(jnp.logical_and(j >= n_qk, j < 2 * n_qk))
    def _():
        rope(kscale)

    @pl.when(jnp.logical_and(j >= 2 * n_qk, j < 2 * n_qk + n_v))
    def _():
        o_ref[...] = acc.astype(o_ref.dtype)

    @pl.when(j >= 2 * n_qk + n_v)
    def _():
        o_ref[...] = (acc * _sigmoid(acc)).astype(o_ref.dtype)


def ret_projection(x, sc, sh, w_in, cos, sin, *, seq, tm=512, tn=1024):
    T, K = x.shape
    N = w_in.shape[1]
    hdk = N // 6
    dk = hdk // RET_H
    assert hdk % tn == 0 and tn % dk == 0
    nb = seq // tm
    kern = functools.partial(_ret_proj_kernel, dk=dk, n_qk=hdk // tn, n_v=2 * hdk // tn, kscale=dk ** -0.5)
    return pl.pallas_call(
        kern,
        grid=(T // tm, N // tn),
        in_specs=[pl.BlockSpec((tm, K), lambda i, j: (i, 0)),
                  pl.BlockSpec((1, 1, K), lambda i, j: (i // nb, 0, 0)),
                  pl.BlockSpec((1, 1, K), lambda i, j: (i // nb, 0, 0)),
                  pl.BlockSpec((K, tn), lambda i, j: (0, j)),
                  pl.BlockSpec((tm, dk // 2), lambda i, j: (i, 0)),
                  pl.BlockSpec((tm, dk // 2), lambda i, j: (i, 0))],
        out_specs=pl.BlockSpec((tm, tn), lambda i, j: (i, j)),
        out_shape=jax.ShapeDtypeStruct((T, N), BF16),
        compiler_params=_params(("parallel", "parallel")),
        name="ret_projection",
    )(x, sc, sh, w_in, cos, sin)


def _retention_kernel(q_ref, k_ref, v_ref, sg_ref, din_ref, dq_ref, dkk_ref, dch_ref, gg_ref, gb_ref,
                      o_ref, state_ref, *, H, C, dk, dv):
    @pl.when(pl.program_id(1) == 0)
    def _():
        state_ref[...] = jnp.zeros_like(state_ref)

    n_chunks = q_ref.shape[0] // C
    for c in range(n_chunks):
        rows = slice(c * C, (c + 1) * C)
        for hh in range(H):
            qc = q_ref[rows, hh * dk:(hh + 1) * dk]
            kc = k_ref[rows, hh * dk:(hh + 1) * dk]
            vc = v_ref[rows, hh * dv:(hh + 1) * dv]
            st = state_ref[hh]
            a = _dot_nt(qc, kc) * din_ref[hh]
            o = _dot(a.astype(BF16), vc)
            qd = (qc.astype(F32) * dq_ref[hh]).astype(BF16)
            o = o + _dot(qd, st.astype(BF16))
            kd = (kc.astype(F32) * dkk_ref[hh]).astype(BF16)
            state_ref[hh] = st * dch_ref[hh, :, 0:1] + _dot_tn(kd, vc)
            mu = jnp.mean(o, axis=-1, keepdims=True)
            d = o - mu
            var = jnp.mean(d * d, axis=-1, keepdims=True)
            on = d * lax.rsqrt(var + EPS) * gg_ref[:, hh * dv:(hh + 1) * dv] + gb_ref[:, hh * dv:(hh + 1) * dv]
            sg = sg_ref[rows, hh * dv:(hh + 1) * dv].astype(F32)
            o_ref[rows, hh * dv:(hh + 1) * dv] = (sg * on).astype(o_ref.dtype)


def retention_core(p, gn_g, gn_b, *, batch, seq, tb=512):
    T, N = p.shape
    H, C = RET_H, RET_C
    hdk = N // 6
    dk, dv = hdk // H, 2 * hdk // H
    hdv = H * dv
    nb = seq // tb
    log_gamma = jnp.log1p(-(2.0 ** (-5.0 - jnp.arange(H, dtype=F32))))
    i = jnp.arange(C, dtype=F32)
    rel = i[:, None] - i[None, :]
    d_intra = jnp.where(rel >= 0, jnp.exp(log_gamma[:, None, None] * jnp.maximum(rel, 0.0)), 0.0).astype(F32)
    d_q = jnp.exp(log_gamma[:, None] * (i + 1.0)).astype(F32)[..., None]
    d_k = jnp.exp(log_gamma[:, None] * (C - 1.0 - i)).astype(F32)[..., None]
    d_ch = jnp.broadcast_to(jnp.exp(log_gamma * C).astype(F32)[:, None, None], (H, 1, LANES))
    kern = functools.partial(_retention_kernel, H=H, C=C, dk=dk, dv=dv)
    return pl.pallas_call(
        kern,
        grid=(batch, nb),
        in_specs=[pl.BlockSpec((tb, hdk), lambda b, s: (b * nb + s, 0)),
                  pl.BlockSpec((tb, hdk), lambda b, s: (b * nb + s, 1)),
                  pl.BlockSpec((tb, hdv), lambda b, s: (b * nb + s, 1)),
                  pl.BlockSpec((tb, hdv), lambda b, s: (b * nb + s, 2)),
                  pl.BlockSpec((H, C, C), lambda b, s: (0, 0, 0)),
                  pl.BlockSpec((H, C, 1), lambda b, s: (0, 0, 0)),
                  pl.BlockSpec((H, C, 1), lambda b, s: (0, 0, 0)),
                  pl.BlockSpec((H, 1, LANES), lambda b, s: (0, 0, 0)),
                  pl.BlockSpec((1, hdv), lambda b, s: (0, 0)),
                  pl.BlockSpec((1, hdv), lambda b, s: (0, 0))],
        out_specs=pl.BlockSpec((tb, hdv), lambda b, s: (b * nb + s, 0)),
        out_shape=jax.ShapeDtypeStruct((T, hdv), BF16),
        scratch_shapes=[pltpu.VMEM((H, dk, dv), F32)],
        compiler_params=_params(("parallel", "arbitrary")),
        name="retention_core",
    )(p, p, p, p, d_intra, d_q, d_k, d_ch, gn_g.reshape(1, hdv), gn_b.reshape(1, hdv))


def _glu_kernel(x_ref, sc_ref, sh_ref, wa_ref, wg_ref, ba_ref, bg_ref, o_ref):
    h = (x_ref[...] * (1.0 + sc_ref[0]) + sh_ref[0]).astype(BF16)
    a = _dot(h, wa_ref[...]) + ba_ref[...]
    g = _dot(h, wg_ref[...]) + bg_ref[...]
    o_ref[...] = a * _sigmoid(g)


def conv_glu(x, sc, sh, w_pw1, b_pw1, *, seq, tm=512, tn=512):
    T, K = x.shape
    N = w_pw1.shape[1] // 2
    nb = seq // tm
    ng = N // tn
    b2 = b_pw1.reshape(1, 2 * N)
    return pl.pallas_call(
        _glu_kernel,
        grid=(T // tm, ng),
        in_specs=[pl.BlockSpec((tm, K), lambda i, j: (i, 0)),
                  pl.BlockSpec((1, 1, K), lambda i, j: (i // nb, 0, 0)),
                  pl.BlockSpec((1, 1, K), lambda i, j: (i // nb, 0, 0)),
                  pl.BlockSpec((K, tn), lambda i, j: (0, j)),
                  pl.BlockSpec((K, tn), lambda i, j: (0, j + ng)),
                  pl.BlockSpec((1, tn), lambda i, j: (0, j)),
                  pl.BlockSpec((1, tn), lambda i, j: (0, j + ng))],
        out_specs=pl.BlockSpec((tm, tn), lambda i, j: (i, j)),
        out_shape=jax.ShapeDtypeStruct((T, N), F32),
        compiler_params=_params(("parallel", "parallel")),
        name="conv_glu",
    )(x, sc, sh, w_pw1, w_pw1, b2, b2)


def _conv_tail_kernel(u_ref, up_ref, wdw_ref, bdw_ref, lg_ref, lb_ref, w2_ref, b2_ref,
                      res_ref, gate_ref, g_ref, bb_ref, o_ref, win_ref, *, nb, halo, width, alpha):
    tm = u_ref.shape[0]
    first = (pl.program_id(0) % nb) == 0
    prev = up_ref[...]
    win_ref[0:halo, :] = jnp.where(first, jnp.zeros_like(prev), prev)
    win_ref[halo:halo + tm, :] = u_ref[...]
    acc = jnp.zeros(u_ref.shape, F32)
    off = halo - (width - 1)
    for k in range(width):
        acc = acc + wdw_ref[k:k + 1, :] * win_ref[off + k:off + k + tm, :]
    acc = acc + bdw_ref[...]
    v = _ln_rows(acc, lg_ref[...], lb_ref[...])
    v = v * _sigmoid(v)
    y = _dot(v.astype(BF16), w2_ref[...]) + b2_ref[...]
    z = alpha * res_ref[...] + (1.0 + gate_ref[0]) * y
    o_ref[...] = _ln_rows(z, g_ref[...], bb_ref[...])


def conv_tail(u, w_dw, b_dw, ln_g, ln_b, w_pw2, b_pw2, res, gate, g, b, *, seq, alpha, tm=512):
    T, D = u.shape
    nb = seq // tm
    halo = 32
    assert CONV_W - 1 <= halo and tm % halo == 0
    r = tm // halo
    wpad = jnp.zeros((halo, D), F32).at[:CONV_W].set(w_dw)
    kern = functools.partial(_conv_tail_kernel, nb=nb, halo=halo, width=CONV_W, alpha=alpha)
    vec = lambda a: a.reshape(1, D)
    one = pl.BlockSpec((1, D), lambda i: (0, 0))
    return pl.pallas_call(
        kern,
        grid=(T // tm,),
        in_specs=[pl.BlockSpec((tm, D), lambda i: (i, 0)),
                  pl.BlockSpec((halo, D), lambda i: (jnp.maximum(i * r - 1, 0), 0)),
                  pl.BlockSpec((halo, D), lambda i: (0, 0)),
                  one, one, one,
                  pl.BlockSpec((D, D), lambda i: (0, 0)),
                  one,
                  pl.BlockSpec((tm, D), lambda i: (i, 0)),
                  pl.BlockSpec((1, 1, D), lambda i: (i // nb, 0, 0)),
                  one, one],
        out_specs=pl.BlockSpec((tm, D), lambda i: (i, 0)),
        out_shape=jax.ShapeDtypeStruct((T, D), F32),
        scratch_shapes=[pltpu.VMEM((halo + tm, D), F32)],
        compiler_params=_params(("parallel",)),
        name="conv_tail",
    )(u, u, wpad, vec(b_dw), vec(ln_g), vec(ln_b), w_pw2, vec(b_pw2), res, gate, vec(g), vec(b))


def _pool_kernel(x_ref, xp_ref, sc_ref, sh_ref, w_ref, pb_ref, ps_ref, gate_ref, g_ref, bb_ref,
                 o_ref, win_ref, *, nb, halo, wins, alpha):
    tm, D = x_ref.shape
    gw = D // len(wins)
    i = pl.program_id(0)
    first = (i % nb) == 0
    x = x_ref[...]
    sc, sh = sc_ref[0], sh_ref[0]
    hp = xp_ref[...] * (1.0 + sc) + sh
    win_ref[0:halo, :] = jnp.where(first, jnp.zeros_like(hp), hp)
    win_ref[halo:halo + tm, :] = x * (1.0 + sc) + sh
    t = (i % nb) * tm + lax.broadcasted_iota(I32, (tm, gw), 0)
    ys = []
    for gi, wn in enumerate(wins):
        cols = slice(gi * gw, (gi + 1) * gw)
        s = win_ref[halo:halo + tm, cols]
        hcur = s
        for k in range(1, wn):
            s = s + win_ref[halo - k:halo - k + tm, cols]
        cnt = jnp.minimum(t + 1, wn).astype(F32)
        pooled = s / cnt - hcur
        ys.append(_dot(pooled.astype(BF16), w_ref[gi]))
    y = (jnp.concatenate(ys, axis=-1) + pb_ref[...]) * ps_ref[...]
    z = alpha * x + (1.0 + gate_ref[0]) * y
    o_ref[...] = _ln_rows(z, g_ref[...], bb_ref[...])


def pool_layer(x, sc, sh, pool_w, pool_b, pool_scale, gate, g, b, *, seq, alpha, tm=512):
    T, D = x.shape
    nb = seq // tm
    halo = 16
    assert max(POOL_WINS) <= halo and tm % halo == 0
    r = tm // halo
    G, gw, _ = pool_w.shape
    kern = functools.partial(_pool_kernel, nb=nb, halo=halo, wins=POOL_WINS, alpha=alpha)
    vec = lambda a: a.reshape(1, D)
    one = pl.BlockSpec((1, D), lambda i: (0, 0))
    per_b = pl.BlockSpec((1, 1, D), lambda i: (i // nb, 0, 0))
    return pl.pallas_call(
        kern,
        grid=(T // tm,),
        in_specs=[pl.BlockSpec((tm, D), lambda i: (i, 0)),
                  pl.BlockSpec((halo, D), lambda i: (jnp.maximum(i * r - 1, 0), 0)),
                  per_b, per_b,
                  pl.BlockSpec((G, gw, gw), lambda i: (0, 0, 0)),
                  one, one, per_b, one, one],
        out_specs=pl.BlockSpec((tm, D), lambda i: (i, 0)),
        out_shape=jax.ShapeDtypeStruct((T, D), F32),
        scratch_shapes=[pltpu.VMEM((halo + tm, D), F32)],
        compiler_params=_params(("parallel",)),
        name="pool_layer",
    )(x, x, sc, sh, pool_w.astype(BF16), vec(pool_b), vec(pool_scale), gate, vec(g), vec(b))


def _rope64(x, cos, sin_signed, lane):
    partner = jnp.where((lane & (NSA_DH - 1)) < NSA_DH // 2, pltpu.roll(x, LANES - NSA_DH // 2, 1), pltpu.roll(x, NSA_DH // 2, 1))
    return x * cos + partner * sin_signed


def _nsa_proj_kernel(x_ref, sc_ref, sh_ref, w_ref, cos_ref, sin_ref, gb_ref, q_ref, kv_ref, gt_ref, *, nq, qscale):
    h = x_ref[...] * (1.0 + sc_ref[0]) + sh_ref[0]
    acc = _dot(h.astype(BF16), w_ref[...])
    cos, sin = cos_ref[...], sin_ref[...]
    lane = lax.broadcasted_iota(I32, cos.shape, 1)
    for s in range(nq):
        blk = acc[:, s * LANES:(s + 1) * LANES]
        q_ref[:, s * LANES:(s + 1) * LANES] = (_rope64(blk, cos, sin, lane) * qscale).astype(q_ref.dtype)
    for s in range(6):
        blk = acc[:, (nq + s) * LANES:(nq + s + 1) * LANES]
        if s % 2 == 0:
            blk = _rope64(blk, cos, sin, lane)
        kv_ref[:, s * LANES:(s + 1) * LANES] = blk.astype(kv_ref.dtype)
    for gi in range(gt_ref.shape[1] // LANES):
        gl = acc[:, (nq + 6 + gi) * LANES:(nq + 7 + gi) * LANES]
        gt_ref[:, gi * LANES:(gi + 1) * LANES] = _sigmoid(gl + gb_ref[:, gi * LANES:(gi + 1) * LANES])


def nsa_projection(x, sc, sh, w_in_p, cos, sin_signed, gate_b_p, *, seq, tm=512):
    T, K = x.shape
    Np = w_in_p.shape[1]
    nq = NSA_H * NSA_DH // LANES
    ng = NSA_G * LANES
    nb = seq // tm
    kern = functools.partial(_nsa_proj_kernel, nq=nq, qscale=NSA_DH ** -0.5)
    per_b = pl.BlockSpec((1, 1, K), lambda i: (i // nb, 0, 0))
    return pl.pallas_call(
        kern,
        grid=(T // tm,),
        in_specs=[pl.BlockSpec((tm, K), lambda i: (i, 0)), per_b, per_b,
                  pl.BlockSpec((K, Np), lambda i: (0, 0)),
                  pl.BlockSpec((tm, LANES), lambda i: (i, 0)),
                  pl.BlockSpec((tm, LANES), lambda i: (i, 0)),
                  pl.BlockSpec((1, ng), lambda i: (0, 0))],
        out_specs=[pl.BlockSpec((tm, nq * LANES), lambda i: (i, 0)),
                   pl.BlockSpec((tm, 6 * LANES), lambda i: (i, 0)),
                   pl.BlockSpec((tm, ng), lambda i: (i, 0))],
        out_shape=[jax.ShapeDtypeStruct((T, nq * LANES), BF16),
                   jax.ShapeDtypeStruct((T, 6 * LANES), BF16),
                   jax.ShapeDtypeStruct((T, ng), F32)],
        compiler_params=_params(("parallel",)),
        name="nsa_projection",
    )(x, sc, sh, w_in_p, cos, sin_signed, gate_b_p)


def _gelu_tanh(v):
    return 0.5 * v * (1.0 + jnp.tanh(math.sqrt(2.0 / math.pi) * (v + 0.044715 * v * v * v)))


def _compress_kernel(xk_ref, xv_ref, pk_ref, pv_ref, kw1_ref, kw2_ref, vw1_ref, vw2_ref, ok_ref, ov_ref):
    def one(x_ref, pe_ref, w1_ref, w2_ref, o_ref):
        x = x_ref[0].astype(F32)
        n = x.shape[0]
        half = x.shape[1]
        a = _dot((x + pe_ref[0:1, :]).astype(BF16), w1_ref[0:half, :])
        b = _dot((x + pe_ref[1:2, :]).astype(BF16), w1_ref[half:2 * half, :])
        pre = a + pltpu.roll(b, n - 1, 0)
        o_ref[0] = _dot(_gelu_tanh(pre).astype(BF16), w2_ref[...]).astype(o_ref.dtype)

    one(xk_ref, pk_ref, kw1_ref, kw2_ref, ok_ref)
    one(xv_ref, pv_ref, vw1_ref, vw2_ref, ov_ref)


def nsa_compress(kc_chunks, vc_chunks, pos_k, pos_v, k_w1, k_w2, v_w1, v_w2):
    BG, n, width = kc_chunks.shape
    hid = k_w1.shape[1]
    dh = k_w2.shape[1]
    full = lambda a: pl.BlockSpec(a.shape, lambda i: (0,) * a.ndim)
    pk = pos_k.reshape(2, width)
    pv = pos_v.reshape(2, width)
    ws = [k_w1.astype(BF16), k_w2.astype(BF16), v_w1.astype(BF16), v_w2.astype(BF16)]
    del hid
    return pl.pallas_call(
        _compress_kernel,
        grid=(BG,),
        in_specs=[pl.BlockSpec((1, n, width), lambda i: (i, 0, 0)),
                  pl.BlockSpec((1, n, width), lambda i: (i, 0, 0)),
                  full(pk), full(pv)] + [full(w) for w in ws],
        out_specs=[pl.BlockSpec((1, n, dh), lambda i: (i, 0, 0)),
                   pl.BlockSpec((1, n, dh), lambda i: (i, 0, 0))],
        out_shape=[jax.ShapeDtypeStruct((BG, n, dh), BF16), jax.ShapeDtypeStruct((BG, n, dh), BF16)],
        compiler_params=_params(("parallel",)),
        name="nsa_compress",
    )(kc_chunks, vc_chunks, pk, pv, *ws)


def _nsa_attn_kernel(q_ref, gt_ref, kc_ref, vcl_ref, vch_ref, ks_ref, vsl_ref, vsh_ref, kw_ref, vwl_ref, vwh_ref,
                     ov_ref, o_ref, qs_ref, m_ref, l_ref, ae_ref, ao_ref, res_ref,
                     *, qt, ck, n_sel, top_n, sel_shift, cmp_len, cmp_step, window):
    t0 = pl.program_id(2) * qt
    R = q_ref.shape[2] // NSA_DH
    rows = R * qt
    half_rows = rows // 2
    lane = lax.broadcasted_iota(I32, (qt, LANES), 1)
    lane_f = lane.astype(F32)

    for p in range(R // 2):
        slab = q_ref[0, :, p * LANES:(p + 1) * LANES]
        qs_ref[p * qt:(p + 1) * qt, :] = jnp.where(lane < NSA_DH, slab, jnp.zeros_like(slab))
        qs_ref[half_rows + p * qt:half_rows + (p + 1) * qt, :] = jnp.where(lane >= NSA_DH, slab, jnp.zeros_like(slab))
    qs = qs_ref[...]
    tq = t0 + lax.broadcasted_iota(I32, (qt, 1), 0)

    ncmp = kc_ref.shape[1]
    cmp_end = lax.broadcasted_iota(I32, (qt, ncmp), 1) * cmp_step + (cmp_len - 1)
    s3 = _dot_nt(qs, kc_ref[0]).reshape(R, qt, ncmp) + jnp.where(cmp_end <= tq, 0.0, NEG_BIG)
    e3 = jnp.exp(s3 - jnp.max(s3, axis=-1, keepdims=True))
    row_ok = jnp.where(tq >= cmp_len - 1, 1.0, 0.0)
    pc3 = e3 * (row_ok / jnp.sum(e3, axis=-1, keepdims=True))
    pcb = pc3.reshape(rows, ncmp).astype(BF16)
    res_ref[0, 0:half_rows, :] = _dot(pcb[0:half_rows], vcl_ref[0])
    res_ref[0, half_rows:rows, :] = _dot(pcb[half_rows:], vch_ref[0])
    psum = jnp.sum(pc3, axis=0)
    imp = _dot(psum.astype(BF16), ov_ref[...])

    cur = lax.shift_right_logical(tq, sel_shift)
    dcur = cur - lane
    forced = (lane == 0) | (dcur == 0) | (dcur == 1)
    score = jnp.where(lax.shift_left(lane, sel_shift) <= tq, jnp.where(forced, NSA_FORCED, imp), NEG_BIG)
    score = jnp.where(lane < n_sel, score, -jnp.inf)
    sel = jnp.zeros((qt, LANES), F32)
    for _ in range(top_n):
        mx = jnp.max(score, axis=-1, keepdims=True)
        idx = jnp.min(jnp.where(score == mx, lane_f, float(LANES)), axis=-1, keepdims=True)
        hit = lane_f == idx
        sel = jnp.where(hit, jnp.where(mx > 0.5 * NEG_BIG, 1.0, sel), sel)
        score = jnp.where(hit, -jnp.inf, score)
    selb = sel.astype(BF16)

    def attend(k_ref, vl_ref, vh_ref, c_lo, c_hi, bias_fn, slot):
        m_ref[...] = jnp.full(m_ref.shape, NEG_BIG, F32)
        l_ref[...] = jnp.zeros(l_ref.shape, F32)
        ae_ref[...] = jnp.zeros(ae_ref.shape, F32)
        ao_ref[...] = jnp.zeros(ao_ref.shape, F32)

        def body(c, carry):
            start = pl.multiple_of(c * ck, ck)
            sc3 = _dot_nt(qs, k_ref[0, pl.ds(start, ck), :]).reshape(R, qt, ck) + bias_fn(start)
            m_old = m_ref[...]
            m_new = jnp.maximum(m_old, jnp.max(sc3, axis=-1, keepdims=True).reshape(rows, 1))
            alpha = jnp.exp(m_old - m_new)
            p3 = jnp.exp(sc3 - m_new.reshape(R, qt, 1))
            l_ref[...] = alpha * l_ref[...] + jnp.sum(p3, axis=-1, keepdims=True).reshape(rows, 1)
            m_ref[...] = m_new
            pb = p3.reshape(rows, ck).astype(BF16)
            ae_ref[...] = alpha[0:half_rows] * ae_ref[...] + _dot(pb[0:half_rows], vl_ref[0, pl.ds(start, ck), :])
            ao_ref[...] = alpha[half_rows:] * ao_ref[...] + _dot(pb[half_rows:], vh_ref[0, pl.ds(start, ck), :])
            return carry

        lax.fori_loop(c_lo, c_hi, body, 0)
        inv = 1.0 / l_ref[...]
        res_ref[slot, 0:half_rows, :] = ae_ref[...] * inv[0:half_rows]
        res_ref[slot, half_rows:rows, :] = ao_ref[...] * inv[half_rows:]

    def key_dist(start):
        return tq - (start + lax.broadcasted_iota(I32, (qt, ck), 1))

    def sel_bias(start):
        jrow = lax.broadcasted_iota(I32, (LANES, ck), 0)
        jcol = lax.shift_right_logical(start + lax.broadcasted_iota(I32, (LANES, ck), 1), sel_shift)
        expand = jnp.where(jrow == jcol, 1.0, 0.0).astype(BF16)
        picked = _dot(selb, expand)
        return jnp.where(key_dist(start) >= 0, jnp.where(picked > 0.5, 0.0, NEG_BIG), NEG_BIG)

    def win_bias(start):
        dist = key_dist(start)
        return jnp.where(jnp.where(dist >= 0, dist, window) < window, 0.0, NEG_BIG)

    c_hi = (t0 + qt + ck - 1) // ck
    attend(ks_ref, vsl_ref, vsh_ref, 0, c_hi, sel_bias, 1)
    attend(kw_ref, vwl_ref, vwh_ref, jnp.maximum(t0 - window + 1, 0) // ck, c_hi, win_bias, 2)

    gt = gt_ref[...]
    for p in range(R // 2):
        acc = jnp.zeros((qt, LANES), F32)
        for par in range(2):
            rsl = slice(par * half_rows + p * qt, par * half_rows + (p + 1) * qt)
            for br in range(3):
                col = (2 * p + par) * 3 + br
                acc = acc + gt[:, col:col + 1] * res_ref[br, rsl, :]
        o_ref[0, :, p * LANES:(p + 1) * LANES] = acc.astype(o_ref.dtype)


def nsa_attention(q, gates, kcd, vcl, vch, ksd, vsl, vsh, kwd, vwl, vwh, *, batch, seq, qt=256, ck=512):
    G = NSA_G
    R = NSA_H // G
    gq = R * NSA_DH
    n_sel = seq // NSA_SEL_LEN
    ncmp = kcd.shape[1]
    nq = seq // qt
    ov = np.zeros((ncmp, LANES), np.float32)
    for n in range(ncmp):
        for j in range(n_sel):
            if n * NSA_CMP_STEP < (j + 1) * NSA_SEL_LEN and n * NSA_CMP_STEP + NSA_CMP_LEN > j * NSA_SEL_LEN:
                ov[n, j] = 1.0
    n_valid = (seq - NSA_CMP_LEN) // NSA_CMP_STEP + 1
    ov[n_valid:] = 0.0
    kern = functools.partial(_nsa_attn_kernel, qt=qt, ck=ck, n_sel=n_sel, top_n=min(NSA_TOPN, n_sel),
                             sel_shift=int(math.log2(NSA_SEL_LEN)), cmp_len=NSA_CMP_LEN, cmp_step=NSA_CMP_STEP, window=NSA_WIN)
    cmp_spec = pl.BlockSpec((1, ncmp, LANES), lambda b, g, i: (b * G + g, 0, 0))
    seq_spec = pl.BlockSpec((1, seq, LANES), lambda b, g, i: (b * G + g, 0, 0))
    rows = R * qt
    return pl.pallas_call(
        kern,
        grid=(batch, G, nq),
        in_specs=[pl.BlockSpec((1, qt, gq), lambda b, g, i: (b, i, g)),
                  pl.BlockSpec((qt, LANES), lambda b, g, i: (b * nq + i, g)),
                  cmp_spec, cmp_spec, cmp_spec,
                  seq_spec, seq_spec, seq_spec, seq_spec, seq_spec, seq_spec,
                  pl.BlockSpec((ncmp, LANES), lambda b, g, i: (0, 0))],
        out_specs=pl.BlockSpec((1, qt, gq), lambda b, g, i: (b, i, g)),
        out_shape=jax.ShapeDtypeStruct((batch, seq, NSA_H * NSA_DH), BF16),
        scratch_shapes=[pltpu.VMEM((rows, LANES), BF16),
                        pltpu.VMEM((rows, 1), F32), pltpu.VMEM((rows, 1), F32),
                        pltpu.VMEM((rows // 2, LANES), F32), pltpu.VMEM((rows // 2, LANES), F32),
                        pltpu.VMEM((3, rows, LANES), F32)],
        compiler_params=_params(("parallel", "parallel", "arbitrary")),
        name="nsa_attention",
    )(q, gates, kcd, vcl, vch, ksd, vsl, vsh, kwd, vwl, vwh, jnp.asarray(ov, BF16))


def _router_kernel(x_ref, sc_ref, sh_ref, w_ref, b_ref, h3_ref, meta_ref, gate_ref, cnt_ref, seen_ref, *, top_k):
    @pl.when(pl.program_id(0) == 0)
    def _():
        seen_ref[...] = jnp.zeros_like(seen_ref)

    h = x_ref[...] * (1.0 + sc_ref[0]) + sh_ref[0]
    for s in range(h3_ref.shape[1]):
        h3_ref[:, s, :] = h[:, s * LANES:(s + 1) * LANES]
    logits = _dot(h.astype(BF16), w_ref[...]) + b_ref[...]
    tm = logits.shape[0]
    lane = lax.broadcasted_iota(I32, logits.shape, 1)
    lane_f = lane.astype(F32)
    val_out = jnp.full(logits.shape, NEG_BIG, F32)
    chosen = jnp.zeros(logits.shape, F32)
    picks = []
    for k in range(top_k):
        mx = jnp.max(logits, axis=-1, keepdims=True)
        idx = jnp.min(jnp.where(logits == mx, lane_f, float(LANES)), axis=-1, keepdims=True)
        hit = lane_f == idx
        picks.append((idx, hit))
        chosen = jnp.where(hit, 1.0, chosen)
        val_out = jnp.where(lane == k, mx, val_out)
        logits = jnp.where(hit, -jnp.inf, logits)
    e = jnp.exp(val_out - jnp.max(val_out, axis=-1, keepdims=True))
    e = jnp.where(lane < top_k, e, 0.0)
    gate_ref[...] = e / jnp.sum(e, axis=-1, keepdims=True)
    lower = jnp.where(lax.broadcasted_iota(I32, (tm, tm), 0) > lax.broadcasted_iota(I32, (tm, tm), 1), 1.0, 0.0)
    before = _dot(lower.astype(BF16), chosen.astype(BF16)) + seen_ref[...]
    meta = jnp.zeros(logits.shape, I32)
    for k, (idx, hit) in enumerate(picks):
        rank = jnp.sum(jnp.where(hit, before, 0.0), axis=-1, keepdims=True)
        meta = jnp.where(lane == k, idx.astype(I32), meta)
        meta = jnp.where(lane == top_k + k, rank.astype(I32), meta)
    meta_ref[...] = meta
    seen_ref[...] = seen_ref[...] + jnp.sum(chosen, axis=0, keepdims=True)
    cnt_ref[...] = seen_ref[...].astype(I32)


def moe_router(x, sc, sh, router_w, router_b, *, seq, tm=512):
    T, D = x.shape
    E = router_w.shape[1]
    nb = seq // tm
    ns = D // LANES
    wp = jnp.zeros((D, LANES), BF16).at[:, :E].set(router_w.astype(BF16))
    bp = jnp.full((1, LANES), NEG_BIG, F32).at[0, :E].set(router_b)
    per_b = pl.BlockSpec((1, 1, D), lambda i: (i // nb, 0, 0))
    return pl.pallas_call(
        functools.partial(_router_kernel, top_k=MOE_K),
        grid=(T // tm,),
        in_specs=[pl.BlockSpec((tm, D), lambda i: (i, 0)), per_b, per_b,
                  pl.BlockSpec((D, LANES), lambda i: (0, 0)),
                  pl.BlockSpec((1, LANES), lambda i: (0, 0))],
        out_specs=[pl.BlockSpec((tm, ns, LANES), lambda i: (i, 0, 0)),
                   pl.BlockSpec((tm, LANES), lambda i: (i, 0)),
                   pl.BlockSpec((tm, LANES), lambda i: (i, 0)),
                   pl.BlockSpec((1, LANES), lambda i: (0, 0))],
        out_shape=[jax.ShapeDtypeStruct((T, ns, LANES), F32),
                   jax.ShapeDtypeStruct((T, LANES), I32),
                   jax.ShapeDtypeStruct((T, LANES), F32),
                   jax.ShapeDtypeStruct((1, LANES), I32)],
        scratch_shapes=[pltpu.VMEM((1, LANES), F32)],
        compiler_params=_params(("arbitrary",)),
        name="moe_router",
    )(x, sc, sh, wp, bp)


def _dispatch_kernel(fill_ref, slots_ref, h3_ref, xs3_ref, zero_ref, sem, *, tm, top_k, n_exp):
    i = pl.program_id(0)

    @pl.when(i == 0)
    def _():
        zero_ref[...] = jnp.zeros_like(zero_ref)
        for e in range(n_exp):
            fill = pltpu.make_async_copy(zero_ref, xs3_ref.at[pl.ds(fill_ref[e], zero_ref.shape[0])], sem)
            fill.start()
            fill.wait()

    def row_copy(r, k):
        return pltpu.make_async_copy(h3_ref.at[i * tm + r], xs3_ref.at[slots_ref[r * top_k + k]], sem)

    def issue(r, carry):
        for k in range(top_k):
            row_copy(r, k).start()
        return carry

    def drain(r, carry):
        for k in range(top_k):
            row_copy(r, k).wait()
        return carry

    lax.fori_loop(0, tm, issue, 0, unroll=8)
    lax.fori_loop(0, tm, drain, 0, unroll=8)


def moe_dispatch(h3, slots, fill_start, n_rows, *, tm=256):
    T, ns, _ = h3.shape
    K = MOE_K
    grid_spec = pltpu.PrefetchScalarGridSpec(
        num_scalar_prefetch=1,
        grid=(T // tm,),
        in_specs=[pl.BlockSpec((tm * K,), lambda i, fill: (i,), memory_space=pltpu.SMEM),
                  pl.BlockSpec(memory_space=pl.ANY)],
        out_specs=pl.BlockSpec(memory_space=pl.ANY),
        scratch_shapes=[pltpu.VMEM((MOE_BLK, ns, LANES), F32), pltpu.SemaphoreType.DMA(())],
    )
    return pl.pallas_call(
        functools.partial(_dispatch_kernel, tm=tm, top_k=K, n_exp=MOE_E),
        grid_spec=grid_spec,
        out_shape=jax.ShapeDtypeStruct((n_rows, ns, LANES), F32),
        compiler_params=_params(("arbitrary",)),
        name="moe_dispatch",
    )(fill_start, slots, h3)


def _expert_kernel(be_ref, nu_ref, x3_ref, w1_ref, b1_ref, w2_ref, b2_ref, o3_ref, xb_ref, w1b_ref, w2b_ref, *, dff):
    i = pl.program_id(0)
    ns = x3_ref.shape[1]

    @pl.when(i < nu_ref[0])
    def _():
        @pl.when(jnp.logical_or(i == 0, be_ref[i] != be_ref[jnp.maximum(i - 1, 0)]))
        def _():
            w1b_ref[...] = w1_ref[0].astype(BF16)
            w2b_ref[...] = w2_ref[0].astype(BF16)

        for s in range(ns):
            xb_ref[:, s * LANES:(s + 1) * LANES] = x3_ref[:, s, :].astype(BF16)
        gu = _dot(xb_ref[...], w1b_ref[...]) + b1_ref[0]
        gg = jnp.minimum(gu[:, :dff], GLU_LIMIT)
        up = jnp.clip(gu[:, dff:], -GLU_LIMIT, GLU_LIMIT)
        act = (up + 1.0) * (gg * _sigmoid(GLU_ALPHA * gg))
        y = _dot(act.astype(BF16), w2b_ref[...]) + b2_ref[0]
        for s in range(ns):
            o3_ref[:, s, :] = y[:, s * LANES:(s + 1) * LANES]


def moe_experts(xs3, block_expert, n_used, w1, b1, w2, b2):
    E, D, two_f = w1.shape
    ns = D // LANES
    nblk = block_expert.shape[0]
    row_map = lambda i, be, nu: (jnp.minimum(i, nu[0] - 1), 0, 0)
    grid_spec = pltpu.PrefetchScalarGridSpec(
        num_scalar_prefetch=2,
        grid=(nblk,),
        in_specs=[pl.BlockSpec((MOE_BLK, ns, LANES), row_map),
                  pl.BlockSpec((1, D, two_f), lambda i, be, nu: (be[i], 0, 0)),
                  pl.BlockSpec((1, 1, two_f), lambda i, be, nu: (be[i], 0, 0)),
                  pl.BlockSpec((1, two_f // 2, D), lambda i, be, nu: (be[i], 0, 0)),
                  pl.BlockSpec((1, 1, D), lambda i, be, nu: (be[i], 0, 0))],
        out_specs=pl.BlockSpec((MOE_BLK, ns, LANES), row_map),
        scratch_shapes=[pltpu.VMEM((MOE_BLK, D), BF16), pltpu.VMEM((D, two_f), BF16), pltpu.VMEM((two_f // 2, D), BF16)],
    )
    return pl.pallas_call(
        functools.partial(_expert_kernel, dff=two_f // 2),
        grid_spec=grid_spec,
        out_shape=jax.ShapeDtypeStruct((nblk * MOE_BLK, ns, LANES), F32),
        compiler_params=_params(("arbitrary",)),
        name="moe_experts",
    )(block_expert, n_used, xs3, w1, b1.reshape(E, 1, two_f), w2, b2.reshape(E, 1, D))


def _combine_kernel(slots_ref, out3_ref, gt_ref, res_ref, gate_ref, g_ref, bb_ref, o_ref, y4_ref, sem, *, tm, top_k, alpha):
    def row_copy(r, k):
        return pltpu.make_async_copy(out3_ref.at[slots_ref[r * top_k + k]], y4_ref.at[k, r], sem)

    def issue(r, carry):
        for k in range(top_k):
            row_copy(r, k).start()
        return carry

    def drain(r, carry):
        for k in range(top_k):
            row_copy(r, k).wait()
        return carry

    lax.fori_loop(0, tm, issue, 0, unroll=8)
    lax.fori_loop(0, tm, drain, 0, unroll=8)
    gt = gt_ref[...]
    cols = []
    for s in range(y4_ref.shape[2]):
        acc = gt[:, 0:1] * y4_ref[0, :, s, :]
        for k in range(1, top_k):
            acc = acc + gt[:, k:k + 1] * y4_ref[k, :, s, :]
        cols.append(acc)
    y = jnp.concatenate(cols, axis=-1)
    z = alpha * res_ref[...] + (1.0 + gate_ref[0]) * y
    o_ref[...] = _ln_rows(z, g_ref[...], bb_ref[...])


def moe_combine(out3, slots, gates, res, gate, g, b, *, seq, alpha, tm=256):
    T, D = res.shape
    K = MOE_K
    ns = D // LANES
    nb = seq // tm
    one = pl.BlockSpec((1, D), lambda i: (0, 0))
    return pl.pallas_call(
        functools.partial(_combine_kernel, tm=tm, top_k=K, alpha=alpha),
        grid=(T // tm,),
        in_specs=[pl.BlockSpec((tm * K,), lambda i: (i,), memory_space=pltpu.SMEM),
                  pl.BlockSpec(memory_space=pl.ANY),
                  pl.BlockSpec((tm, LANES), lambda i: (i, 0)),
                  pl.BlockSpec((tm, D), lambda i: (i, 0)),
                  pl.BlockSpec((1, 1, D), lambda i: (i // nb, 0, 0)), one, one],
        out_specs=pl.BlockSpec((tm, D), lambda i: (i, 0)),
        out_shape=jax.ShapeDtypeStruct((T, D), F32),
        scratch_shapes=[pltpu.VMEM((K, tm, ns, LANES), F32), pltpu.SemaphoreType.DMA(())],
        compiler_params=_params(("arbitrary",)),
        name="moe_combine",
    )(slots, out3, gates, res, gate, g.reshape(1, D), b.reshape(1, D))


def moe_layer(x, sc, sh, gate, ln_g, ln_b, router_w, router_b, w1, b1, w2, b2, *, seq, alpha):
    T, D = x.shape
    E, K = MOE_E, MOE_K
    h3, meta, gates, cnt = moe_router(x, sc, sh, router_w, router_b, seq=seq)
    counts = cnt[0, :E]
    blocks_per = (counts + MOE_BLK - 1) // MOE_BLK
    block_end = jnp.cumsum(blocks_per)
    pad_start = (block_end - blocks_per) * MOE_BLK
    n_blocks = -(-(T * K) // MOE_BLK) + E
    blk = jnp.arange(n_blocks, dtype=I32)
    block_expert = jnp.minimum(jnp.sum(block_end[None, :] <= blk[:, None], axis=1), E - 1).astype(I32)
    onehot = meta[:, :K, None] == jnp.arange(E, dtype=I32)[None, None, :]
    slots = (meta[:, K:2 * K] + jnp.sum(jnp.where(onehot, pad_start[None, None, :], 0), axis=-1)).reshape(-1).astype(I32)
    fill_start = (pad_start + counts).astype(I32)
    xs3 = moe_dispatch(h3, slots, fill_start, (n_blocks + 1) * MOE_BLK)
    out3 = moe_experts(xs3, block_expert, block_end[-1:].astype(I32), w1, b1, w2, b2)
    return moe_combine(out3, slots, gates, x, gate, ln_g, ln_b, seq=seq, alpha=alpha)


def _rope_tables(positions, d):
    inv = ROPE_BASE ** (-jnp.arange(0, d, 2, dtype=F32) / d)
    ang = positions.astype(F32).reshape(-1)[:, None] * inv
    return jnp.cos(ang), jnp.sin(ang)


def kernel(x, c, positions, ada_w, ada_b, ln1_g, ln1_b, ln2_g, ln2_b, router_w, router_b, moe_w1, moe_b1, moe_w2, moe_b2, ret_w_in, ret_gn_g, ret_gn_b, ret_w_out, conv_w_pw1, conv_b_pw1, conv_w_dw, conv_b_dw, conv_ln_g, conv_ln_b, conv_w_pw2, conv_b_pw2, nsa_w_in, nsa_gate_b, nsa_cmp_pos_k, nsa_cmp_pos_v, nsa_cmp_k_w1, nsa_cmp_k_w2, nsa_cmp_v_w1, nsa_cmp_v_w2, nsa_w_out, pool_w, pool_b, pool_scale):
    B, S, D = x.shape
    T = B * S
    depth = ada_w.shape[0]
    alpha = (2.0 * depth) ** 0.25
    mod = ada_modulation(c, ada_w, ada_b)
    xf = x.reshape(T, D)
    for i in range(depth):
        kind, j = i % 4, i // 4
        sh1, sc1, g1, sh2, sc2, g2 = (mod[i, :, k * D:(k + 1) * D].reshape(B, 1, D) for k in range(6))
        res1 = (g1, ln1_g[i], ln1_b[i])
        if kind == 0:
            dk = ret_w_in.shape[2] // 6 // RET_H
            cos, sin = _rope_tables(positions, dk)
            p = ret_projection(xf, sc1, sh1, ret_w_in[j].astype(BF16), cos, sin, seq=S)
            o = retention_core(p, ret_gn_g[j], ret_gn_b[j], batch=B, seq=S)
            xf = fused_matmul(o, ret_w_out[j].astype(BF16), seq=S, resln=(xf,) + res1, alpha=alpha, name="ret_out")
        elif kind == 1:
            u = conv_glu(xf, sc1, sh1, conv_w_pw1[j].astype(BF16), conv_b_pw1[j], seq=S)
            xf = conv_tail(u, conv_w_dw[j], conv_b_dw[j], conv_ln_g[j], conv_ln_b[j], conv_w_pw2[j].astype(BF16),
                           conv_b_pw2[j], xf, *res1, seq=S, alpha=alpha)
        elif kind == 2:
            G, dh, H = NSA_G, NSA_DH, NSA_H
            cos, sin = _rope_tables(positions, dh)
            cos_p = jnp.tile(cos, (1, LANES // (dh // 2)))
            sin_p = jnp.tile(jnp.concatenate([-sin, sin], axis=1), (1, LANES // dh))
            n_main = H * dh + 6 * G * dh
            ng = 3 * H // G
            w_g = jnp.zeros((D, G, LANES), F32).at[:, :, :ng].set(nsa_w_in[j][:, n_main:].reshape(D, G, ng))
            w_p = jnp.concatenate([nsa_w_in[j][:, :n_main], w_g.reshape(D, G * LANES)], axis=1).astype(BF16)
            gb_p = jnp.zeros((G, LANES), F32).at[:, :ng].set(nsa_gate_b[j].reshape(G, ng)).reshape(1, G * LANES)
            q, kv, gates = nsa_projection(xf, sc1, sh1, w_p, cos_p, sin_p, gb_p, seq=S)
            kv = kv.reshape(B, S, 6, G, dh).transpose(2, 0, 3, 1, 4).reshape(6, B * G, S, dh)
            kc, vc, ks, vs, kw, vw = (kv[k] for k in range(6))
            chunk = lambda a: a.reshape(B * G, S // NSA_CMP_STEP, NSA_CMP_STEP * dh)
            k_cmp, v_cmp = nsa_compress(chunk(kc), chunk(vc), nsa_cmp_pos_k[j], nsa_cmp_pos_v[j], nsa_cmp_k_w1[j],
                                        nsa_cmp_k_w2[j], nsa_cmp_v_w1[j], nsa_cmp_v_w2[j])
            dup = lambda a: jnp.concatenate([a, a], axis=-1)
            lo = lambda a: jnp.concatenate([a, jnp.zeros_like(a)], axis=-1)
            hi = lambda a: jnp.concatenate([jnp.zeros_like(a), a], axis=-1)
            o = nsa_attention(q.reshape(B, S, H * dh), gates, dup(k_cmp), lo(v_cmp), hi(v_cmp), dup(ks), lo(vs), hi(vs),
                              dup(kw), lo(vw), hi(vw), batch=B, seq=S)
            xf = fused_matmul(o.reshape(T, H * dh), nsa_w_out[j].astype(BF16), seq=S, resln=(xf,) + res1, alpha=alpha,
                              name="nsa_out")
        else:
            xf = pool_layer(xf, sc1, sh1, pool_w[j], pool_b[j], pool_scale[j], *res1, seq=S, alpha=alpha)
        xf = moe_layer(xf, sc2, sh2, g2, ln2_g[i], ln2_b[i], router_w[i], router_b[i], moe_w1[i], moe_b1[i],
                       moe_w2[i], moe_b2[i], seq=S, alpha=alpha)
    return xf.reshape(B, S, D)
```

```python
import functools
import math

import jax
import jax.numpy as jnp
import numpy as np
from jax import lax
from jax.experimental import pallas as pl
from jax.experimental.pallas import tpu as pltpu

F32 = jnp.float32
BF16 = jnp.bfloat16
I32 = jnp.int32

ROPE_BASE = 10000.0
EPS = 1e-5
NEG_BIG = -1e30

RET_H = 4
RET_C = 128
CONV_W = 31
NSA_H = 16
NSA_G = 2
NSA_DH = 64
NSA_CMP_LEN = 32
NSA_CMP_STEP = 16
NSA_SEL_LEN = 64
NSA_TOPN = 16
NSA_WIN = 512
NSA_FORCED = 1e6
POOL_WINS = (2, 4, 8, 16)
MOE_E = 32
MOE_K = 4
MOE_BLK = 256
GLU_LIMIT = 7.0
GLU_ALPHA = 1.702

V7X_VMEM_LIMIT = 56 * 1024 * 1024
LANES = 128


def _params(sem):
    return pltpu.CompilerParams(dimension_semantics=sem, vmem_limit_bytes=V7X_VMEM_LIMIT)


def _ln_rows(v, g, b):
    mu = jnp.mean(v, axis=-1, keepdims=True)
    d = v - mu
    var = jnp.mean(d * d, axis=-1, keepdims=True)
    return d * lax.rsqrt(var + EPS) * g + b


def _sigmoid(v):
    return 1.0 / (1.0 + jnp.exp(-v))


def _dot(a, b):
    return jnp.dot(a, b, preferred_element_type=F32)


def _dot_nt(a, b):
    return lax.dot_general(a, b, (((1,), (1,)), ((), ())), preferred_element_type=F32)


def _dot_tn(a, b):
    return lax.dot_general(a, b, (((0,), (0,)), ((), ())), preferred_element_type=F32)


def _ada_kernel(c_ref, w_ref, b_ref, o_ref):
    c = c_ref[...]
    ca = c * _sigmoid(c)
    o_ref[0] = _dot(ca.astype(BF16), w_ref[0].astype(BF16)) + b_ref[0]


def ada_modulation(c, ada_w, ada_b):
    L, D, N = ada_w.shape
    B = c.shape[0]
    tn = 2048 if N % 2048 == 0 else N
    return pl.pallas_call(
        _ada_kernel,
        grid=(L, N // tn),
        in_specs=[pl.BlockSpec((B, D), lambda l, j: (0, 0)),
                  pl.BlockSpec((1, D, tn), lambda l, j: (l, 0, j)),
                  pl.BlockSpec((1, 1, tn), lambda l, j: (l, 0, j))],
        out_specs=pl.BlockSpec((1, B, tn), lambda l, j: (l, 0, j)),
        out_shape=jax.ShapeDtypeStruct((L, B, N), F32),
        compiler_params=_params(("parallel", "parallel")),
        name="ada_modulation",
    )(c, ada_w, ada_b.reshape(L, 1, N))


def _mm_kernel(*refs, has_mod, has_bias, epi, alpha):
    it = iter(refs)
    x_ref = next(it)
    if has_mod:
        sc_ref, sh_ref = next(it), next(it)
    w_ref = next(it)
    if has_bias:
        b_ref = next(it)
    if epi == "resln":
        res_ref, gate_ref, g_ref, bb_ref = next(it), next(it), next(it), next(it)
    o_ref = next(it)
    x = x_ref[...]
    if has_mod:
        x = x.astype(F32) * (1.0 + sc_ref[0]) + sh_ref[0]
    acc = _dot(x.astype(BF16), w_ref[...])
    if has_bias:
        acc = acc + b_ref[...]
    if epi == "resln":
        v = alpha * res_ref[...] + (1.0 + gate_ref[0]) * acc
        acc = _ln_rows(v, g_ref[...], bb_ref[...])
    o_ref[...] = acc.astype(o_ref.dtype)


def fused_matmul(x, w, *, seq, mod=None, bias=None, resln=None, alpha=1.0, out_dtype=F32, tm=512, tn=None, name="mm"):
    T, K = x.shape
    N = w.shape[1]
    tn = N if tn is None else tn
    assert T % tm == 0 and seq % tm == 0 and N % tn == 0
    nb = seq // tm
    args, specs = [x], [pl.BlockSpec((tm, K), lambda i, j: (i, 0))]
    if mod is not None:
        for m in mod:
            args.append(m)
            specs.append(pl.BlockSpec((1, 1, K), lambda i, j: (i // nb, 0, 0)))
    args.append(w)
    specs.append(pl.BlockSpec((K, tn), lambda i, j: (0, j)))
    if bias is not None:
        args.append(bias.reshape(1, N))
        specs.append(pl.BlockSpec((1, tn), lambda i, j: (0, j)))
    epi = "none"
    if resln is not None:
        assert tn == N
        res, gate, g, b = resln
        epi = "resln"
        args += [res, gate, g.reshape(1, N), b.reshape(1, N)]
        specs += [pl.BlockSpec((tm, N), lambda i, j: (i, 0)),
                  pl.BlockSpec((1, 1, N), lambda i, j: (i // nb, 0, 0)),
                  pl.BlockSpec((1, N), lambda i, j: (0, 0)),
                  pl.BlockSpec((1, N), lambda i, j: (0, 0))]
    kern = functools.partial(_mm_kernel, has_mod=mod is not None, has_bias=bias is not None, epi=epi, alpha=alpha)
    return pl.pallas_call(
        kern,
        grid=(T // tm, N // tn),
        in_specs=specs,
        out_specs=pl.BlockSpec((tm, tn), lambda i, j: (i, j)),
        out_shape=jax.ShapeDtypeStruct((T, N), out_dtype),
        compiler_params=_params(("parallel", "parallel")),
        name=name,
    )(*args)


def _ret_proj_kernel(x_ref, sc_ref, sh_ref, w_ref, cos_ref, sin_ref, o_ref, *, dk, n_qk, n_v, kscale):
    j = pl.program_id(1)
    h = x_ref[...] * (1.0 + sc_ref[0]) + sh_ref[0]
    acc = _dot(h.astype(BF16), w_ref[...])
    half = dk // 2

    def rope(scale):
        cos, sin = cos_ref[...], sin_ref[...]
        for hh in range(acc.shape[1] // dk):
            x1 = acc[:, hh * dk:hh * dk + half]
            x2 = acc[:, hh * dk + half:(hh + 1) * dk]
            o_ref[:, hh * dk:hh * dk + half] = ((x1 * cos - x2 * sin) * scale).astype(o_ref.dtype)
            o_ref[:, hh * dk + half:(hh + 1) * dk] = ((x1 * sin + x2 * cos) * scale).astype(o_ref.dtype)

    @pl.when(j < n_qk)
    def _():
        rope(1.0)

    @pl.when(jnp.logical_and(j >= n_qk, j < 2 * n_qk))
    def _():
        rope(kscale)

    @pl.when(jnp.logical_and(j >= 2 * n_qk, j < 2 * n_qk + n_v))
    def _():
        o_ref[...] = acc.astype(o_ref.dtype)

    @pl.when(j >= 2 * n_qk + n_v)
    def _():
        o_ref[...] = (acc * _sigmoid(acc)).astype(o_ref.dtype)


def ret_projection(x, sc, sh, w_in, cos, sin, *, seq, tm=512, tn=1024):
    T, K = x.shape
    N = w_in.shape[1]
    hdk = N // 6
    dk = hdk // RET_H
    assert hdk % tn == 0 and tn % dk == 0
    nb = seq // tm
    kern = functools.partial(_ret_proj_kernel, dk=dk, n_qk=hdk // tn, n_v=2 * hdk // tn, kscale=dk ** -0.5)
    return pl.pallas_call(
        kern,
        grid=(T // tm, N // tn),
        in_specs=[pl.BlockSpec((tm, K), lambda i, j: (i, 0)),
                  pl.BlockSpec((1, 1, K), lambda i, j: (i // nb, 0, 0)),
                  pl.BlockSpec((1, 1, K), lambda i, j: (i // nb, 0, 0)),
                  pl.BlockSpec((K, tn), lambda i, j: (0, j)),
                  pl.BlockSpec((tm, dk // 2), lambda i, j: (i, 0)),
                  pl.BlockSpec((tm, dk // 2), lambda i, j: (i, 0))],
        out_specs=pl.BlockSpec((tm, tn), lambda i, j: (i, j)),
        out_shape=jax.ShapeDtypeStruct((T, N), BF16),
        compiler_params=_params(("parallel", "parallel")),
        name="ret_projection",
    )(x, sc, sh, w_in, cos, sin)


def _retention_kernel(q_ref, k_ref, v_ref, sg_ref, din_ref, dq_ref, dkk_ref, dch_ref, gg_ref, gb_ref,
                      o_ref, state_ref, *, H, C, dk, dv):
    @pl.when(pl.program_id(1) == 0)
    def _():
        state_ref[...] = jnp.zeros_like(state_ref)

    n_chunks = q_ref.shape[0] // C
    for c in range(n_chunks):
        rows = slice(c * C, (c + 1) * C)
        for hh in range(H):
            qc = q_ref[rows, hh * dk:(hh + 1) * dk]
            kc = k_ref[rows, hh * dk:(hh + 1) * dk]
            vc = v_ref[rows, hh * dv:(hh + 1) * dv]
            st = state_ref[hh]
            a = _dot_nt(qc, kc) * din_ref[hh]
            o = _dot(a.astype(BF16), vc)
            qd = (qc.astype(F32) * dq_ref[hh]).astype(BF16)
            o = o + _dot(qd, st.astype(BF16))
            kd = (kc.astype(F32) * dkk_ref[hh]).astype(BF16)
            state_ref[hh] = st * dch_ref[hh, :, 0:1] + _dot_tn(kd, vc)
            mu = jnp.mean(o, axis=-1, keepdims=True)
            d = o - mu
            var = jnp.mean(d * d, axis=-1, keepdims=True)
            on = d * lax.rsqrt(var + EPS) * gg_ref[:, hh * dv:(hh + 1) * dv] + gb_ref[:, hh * dv:(hh + 1) * dv]
            sg = sg_ref[rows, hh * dv:(hh + 1) * dv].astype(F32)
            o_ref[rows, hh * dv:(hh + 1) * dv] = (sg * on).astype(o_ref.dtype)


def retention_core(p, gn_g, gn_b, *, batch, seq, tb=512):
    T, N = p.shape
    H, C = RET_H, RET_C
    hdk = N // 6
    dk, dv = hdk // H, 2 * hdk // H
    hdv = H * dv
    nb = seq // tb
    log_gamma = jnp.log1p(-(2.0 ** (-5.0 - jnp.arange(H, dtype=F32))))
    i = jnp.arange(C, dtype=F32)
    rel = i[:, None] - i[None, :]
    d_intra = jnp.where(rel >= 0, jnp.exp(log_gamma[:, None, None] * jnp.maximum(rel, 0.0)), 0.0).astype(F32)
    d_q = jnp.exp(log_gamma[:, None] * (i + 1.0)).astype(F32)[..., None]
    d_k = jnp.exp(log_gamma[:, None] * (C - 1.0 - i)).astype(F32)[..., None]
    d_ch = jnp.broadcast_to(jnp.exp(log_gamma * C).astype(F32)[:, None, None], (H, 1, LANES))
    kern = functools.partial(_retention_kernel, H=H, C=C, dk=dk, dv=dv)
    return pl.pallas_call(
        kern,
        grid=(batch, nb),
        in_specs=[pl.BlockSpec((tb, hdk), lambda b, s: (b * nb + s, 0)),
                  pl.BlockSpec((tb, hdk), lambda b, s: (b * nb + s, 1)),
                  pl.BlockSpec((tb, hdv), lambda b, s: (b * nb + s, 1)),
                  pl.BlockSpec((tb, hdv), lambda b, s: (b * nb + s, 2)),
                  pl.BlockSpec((H, C, C), lambda b, s: (0, 0, 0)),
                  pl.BlockSpec((H, C, 1), lambda b, s: (0, 0, 0)),
                  pl.BlockSpec((H, C, 1), lambda b, s: (0, 0, 0)),
                  pl.BlockSpec((H, 1, LANES), lambda b, s: (0, 0, 0)),
                  pl.BlockSpec((1, hdv), lambda b, s: (0, 0)),
                  pl.BlockSpec((1, hdv), lambda b, s: (0, 0))],
        out_specs=pl.BlockSpec((tb, hdv), lambda b, s: (b * nb + s, 0)),
        out_shape=jax.ShapeDtypeStruct((T, hdv), BF16),
        scratch_shapes=[pltpu.VMEM((H, dk, dv), F32)],
        compiler_params=_params(("parallel", "arbitrary")),
        name="retention_core",
    )(p, p, p, p, d_intra, d_q, d_k, d_ch, gn_g.reshape(1, hdv), gn_b.reshape(1, hdv))


def _glu_kernel(x_ref, sc_ref, sh_ref, wa_ref, wg_ref, ba_ref, bg_ref, o_ref):
    h = (x_ref[...] * (1.0 + sc_ref[0]) + sh_ref[0]).astype(BF16)
    a = _dot(h, wa_ref[...]) + ba_ref[...]
    g = _dot(h, wg_ref[...]) + bg_ref[...]
    o_ref[...] = a * _sigmoid(g)


def conv_glu(x, sc, sh, w_pw1, b_pw1, *, seq, tm=512, tn=512):
    T, K = x.shape
    N = w_pw1.shape[1] // 2
    nb = seq // tm
    ng = N // tn
    b2 = b_pw1.reshape(1, 2 * N)
    return pl.pallas_call(
        _glu_kernel,
        grid=(T // tm, ng),
        in_specs=[pl.BlockSpec((tm, K), lambda i, j: (i, 0)),
                  pl.BlockSpec((1, 1, K), lambda i, j: (i // nb, 0, 0)),
                  pl.BlockSpec((1, 1, K), lambda i, j: (i // nb, 0, 0)),
                  pl.BlockSpec((K, tn), lambda i, j: (0, j)),
                  pl.BlockSpec((K, tn), lambda i, j: (0, j + ng)),
                  pl.BlockSpec((1, tn), lambda i, j: (0, j)),
                  pl.BlockSpec((1, tn), lambda i, j: (0, j + ng))],
        out_specs=pl.BlockSpec((tm, tn), lambda i, j: (i, j)),
        out_shape=jax.ShapeDtypeStruct((T, N), F32),
        compiler_params=_params(("parallel", "parallel")),
        name="conv_glu",
    )(x, sc, sh, w_pw1, w_pw1, b2, b2)


def _conv_tail_kernel(u_ref, up_ref, wdw_ref, bdw_ref, lg_ref, lb_ref, w2_ref, b2_ref,
                      res_ref, gate_ref, g_ref, bb_ref, o_ref, win_ref, shift_ref, *, nb, halo, width, alpha):
    tm = u_ref.shape[0]
    first = (pl.program_id(0) % nb) == 0
    prev = up_ref[...]
    win_ref[0:halo, :] = jnp.where(first, jnp.zeros_like(prev), prev)
    win_ref[halo:halo + tm, :] = u_ref[...]
    for b in range(1, 8):
        n = (tm + halo - b) // 8 * 8
        shift_ref[b - 1, 0:n, :] = win_ref[b:b + n, :]
    acc = jnp.zeros(u_ref.shape, F32)
    off = halo - (width - 1)
    for k in range(width):
        b, a = (off + k) % 8, (off + k) // 8 * 8
        rows = win_ref[a:a + tm, :] if b == 0 else shift_ref[b - 1, a:a + tm, :]
        acc = acc + wdw_ref[k:k + 1, :] * rows
    acc = acc + bdw_ref[...]
    v = _ln_rows(acc, lg_ref[...], lb_ref[...])
    v = v * _sigmoid(v)
    y = _dot(v.astype(BF16), w2_ref[...]) + b2_ref[...]
    z = alpha * res_ref[...] + (1.0 + gate_ref[0]) * y
    o_ref[...] = _ln_rows(z, g_ref[...], bb_ref[...])


def conv_tail(u, w_dw, b_dw, ln_g, ln_b, w_pw2, b_pw2, res, gate, g, b, *, seq, alpha, tm=512):
    T, D = u.shape
    nb = seq // tm
    halo = 32
    assert CONV_W - 1 <= halo and tm % halo == 0
    r = tm // halo
    wpad = jnp.zeros((halo, D), F32).at[:CONV_W].set(w_dw)
    kern = functools.partial(_conv_tail_kernel, nb=nb, halo=halo, width=CONV_W, alpha=alpha)
    vec = lambda a: a.reshape(1, D)
    one = pl.BlockSpec((1, D), lambda i: (0, 0))
    return pl.pallas_call(
        kern,
        grid=(T // tm,),
        in_specs=[pl.BlockSpec((tm, D), lambda i: (i, 0)),
                  pl.BlockSpec((halo, D), lambda i: (jnp.maximum(i * r - 1, 0), 0)),
                  pl.BlockSpec((halo, D), lambda i: (0, 0)),
                  one, one, one,
                  pl.BlockSpec((D, D), lambda i: (0, 0)),
                  one,
                  pl.BlockSpec((tm, D), lambda i: (i, 0)),
                  pl.BlockSpec((1, 1, D), lambda i: (i // nb, 0, 0)),
                  one, one],
        out_specs=pl.BlockSpec((tm, D), lambda i: (i, 0)),
        out_shape=jax.ShapeDtypeStruct((T, D), F32),
        scratch_shapes=[pltpu.VMEM((halo + tm, D), F32), pltpu.VMEM((7, halo + tm, D), F32)],
        compiler_params=_params(("parallel",)),
        name="conv_tail",
    )(u, u, wpad, vec(b_dw), vec(ln_g), vec(ln_b), w_pw2, vec(b_pw2), res, gate, vec(g), vec(b))


def _pool_kernel(x_ref, xp_ref, sc_ref, sh_ref, w_ref, pb_ref, ps_ref, gate_ref, g_ref, bb_ref,
                 o_ref, win_ref, *, nb, halo, wins, alpha):
    tm, D = x_ref.shape
    gw = D // len(wins)
    i = pl.program_id(0)
    first = (i % nb) == 0
    x = x_ref[...]
    sc, sh = sc_ref[0], sh_ref[0]
    hp = xp_ref[...] * (1.0 + sc) + sh
    win_ref[0:halo, :] = jnp.where(first, jnp.zeros_like(hp), hp)
    win_ref[halo:halo + tm, :] = x * (1.0 + sc) + sh
    t = (i % nb) * tm + lax.broadcasted_iota(I32, (tm, gw), 0)
    ys = []
    for gi, wn in enumerate(wins):
        cols = slice(gi * gw, (gi + 1) * gw)
        s = win_ref[halo:halo + tm, cols]
        hcur = s
        for k in range(1, wn):
            s = s + win_ref[halo - k:halo - k + tm, cols]
        cnt = jnp.minimum(t + 1, wn).astype(F32)
        pooled = s / cnt - hcur
        ys.append(_dot(pooled.astype(BF16), w_ref[gi]))
    y = (jnp.concatenate(ys, axis=-1) + pb_ref[...]) * ps_ref[...]
    z = alpha * x + (1.0 + gate_ref[0]) * y
    o_ref[...] = _ln_rows(z, g_ref[...], bb_ref[...])


def pool_layer(x, sc, sh, pool_w, pool_b, pool_scale, gate, g, b, *, seq, alpha, tm=512):
    T, D = x.shape
    nb = seq // tm
    halo = 16
    assert max(POOL_WINS) <= halo and tm % halo == 0
    r = tm // halo
    G, gw, _ = pool_w.shape
    kern = functools.partial(_pool_kernel, nb=nb, halo=halo, wins=POOL_WINS, alpha=alpha)
    vec = lambda a: a.reshape(1, D)
    one = pl.BlockSpec((1, D), lambda i: (0, 0))
    per_b = pl.BlockSpec((1, 1, D), lambda i: (i // nb, 0, 0))
    return pl.pallas_call(
        kern,
        grid=(T // tm,),
        in_specs=[pl.BlockSpec((tm, D), lambda i: (i, 0)),
                  pl.BlockSpec((halo, D), lambda i: (jnp.maximum(i * r - 1, 0), 0)),
                  per_b, per_b,
                  pl.BlockSpec((G, gw, gw), lambda i: (0, 0, 0)),
                  one, one, per_b, one, one],
        out_specs=pl.BlockSpec((tm, D), lambda i: (i, 0)),
        out_shape=jax.ShapeDtypeStruct((T, D), F32),
        scratch_shapes=[pltpu.VMEM((halo + tm, D), F32)],
        compiler_params=_params(("parallel",)),
        name="pool_layer",
    )(x, x, sc, sh, pool_w.astype(BF16), vec(pool_b), vec(pool_scale), gate, vec(g), vec(b))


def _rope64(x, cos, sin_signed, lane):
    partner = jnp.where((lane & (NSA_DH - 1)) < NSA_DH // 2, pltpu.roll(x, LANES - NSA_DH // 2, 1), pltpu.roll(x, NSA_DH // 2, 1))
    return x * cos + partner * sin_signed


def _nsa_proj_kernel(x_ref, sc_ref, sh_ref, w_ref, cos_ref, sin_ref, gb_ref, q_ref, cv_ref, kvx_ref, gt_ref, *, nq, qscale):
    h = x_ref[...] * (1.0 + sc_ref[0]) + sh_ref[0]
    acc = _dot(h.astype(BF16), w_ref[...])
    cos, sin = cos_ref[...], sin_ref[...]
    lane = lax.broadcasted_iota(I32, cos.shape, 1)
    for s in range(nq):
        blk = acc[:, s * LANES:(s + 1) * LANES]
        q_ref[:, s * LANES:(s + 1) * LANES] = (_rope64(blk, cos, sin, lane) * qscale).astype(q_ref.dtype)
    low = lane < NSA_DH
    for s in range(6):
        blk = acc[:, (nq + s) * LANES:(nq + s + 1) * LANES]
        if s % 2 == 0:
            blk = _rope64(blk, cos, sin, lane)
        if s < 2:
            cv_ref[:, s * LANES:(s + 1) * LANES] = blk.astype(cv_ref.dtype)
            continue
        swapped = pltpu.roll(blk, NSA_DH, 1)
        for gi in range(2):
            own, other = (blk, swapped) if gi == 0 else (swapped, blk)
            if s % 2 == 0:
                kvx_ref[3 * (s // 2 - 1), 0, gi] = jnp.where(low, own, other).astype(kvx_ref.dtype)
            else:
                kvx_ref[3 * (s // 2 - 1) + 1, 0, gi] = jnp.where(low, own, 0.0).astype(kvx_ref.dtype)
                kvx_ref[3 * (s // 2 - 1) + 2, 0, gi] = jnp.where(low, 0.0, other).astype(kvx_ref.dtype)
    for gi in range(gt_ref.shape[1] // LANES):
        gl = acc[:, (nq + 6 + gi) * LANES:(nq + 7 + gi) * LANES]
        gt_ref[:, gi * LANES:(gi + 1) * LANES] = _sigmoid(gl + gb_ref[:, gi * LANES:(gi + 1) * LANES])


def nsa_projection(x, sc, sh, w_in_p, cos, sin_signed, gate_b_p, *, seq, tm=512):
    T, K = x.shape
    Np = w_in_p.shape[1]
    nq = NSA_H * NSA_DH // LANES
    assert NSA_G * NSA_DH == LANES
    ng = NSA_G * LANES
    nb = seq // tm
    kern = functools.partial(_nsa_proj_kernel, nq=nq, qscale=NSA_DH ** -0.5)
    per_b = pl.BlockSpec((1, 1, K), lambda i: (i // nb, 0, 0))
    return pl.pallas_call(
        kern,
        grid=(T // tm,),
        in_specs=[pl.BlockSpec((tm, K), lambda i: (i, 0)), per_b, per_b,
                  pl.BlockSpec((K, Np), lambda i: (0, 0)),
                  pl.BlockSpec((tm, LANES), lambda i: (i, 0)),
                  pl.BlockSpec((tm, LANES), lambda i: (i, 0)),
                  pl.BlockSpec((1, ng), lambda i: (0, 0))],
        out_specs=[pl.BlockSpec((tm, nq * LANES), lambda i: (i, 0)),
                   pl.BlockSpec((tm, 2 * LANES), lambda i: (i, 0)),
                   pl.BlockSpec((6, 1, NSA_G, tm, LANES), lambda i: (0, i // nb, 0, i % nb, 0)),
                   pl.BlockSpec((tm, ng), lambda i: (i, 0))],
        out_shape=[jax.ShapeDtypeStruct((T, nq * LANES), BF16),
                   jax.ShapeDtypeStruct((T, 2 * LANES), BF16),
                   jax.ShapeDtypeStruct((6, T // seq, NSA_G, seq, LANES), BF16),
                   jax.ShapeDtypeStruct((T, ng), F32)],
        compiler_params=_params(("parallel",)),
        name="nsa_projection",
    )(x, sc, sh, w_in_p, cos, sin_signed, gate_b_p)


def _gelu_tanh(v):
    return 0.5 * v * (1.0 + jnp.tanh(math.sqrt(2.0 / math.pi) * (v + 0.044715 * v * v * v)))


def _compress_kernel(xk_ref, xv_ref, pk_ref, pv_ref, kw1_ref, kw2_ref, vw1_ref, vw2_ref, ok_ref, ov_ref):
    def one(x_ref, pe_ref, w1_ref, w2_ref, o_ref):
        x = x_ref[0].astype(F32)
        n = x.shape[0]
        half = x.shape[1]
        a = _dot((x + pe_ref[0:1, :]).astype(BF16), w1_ref[0:half, :])
        b = _dot((x + pe_ref[1:2, :]).astype(BF16), w1_ref[half:2 * half, :])
        pre = a + pltpu.roll(b, n - 1, 0)
        o_ref[0] = _dot(_gelu_tanh(pre).astype(BF16), w2_ref[...]).astype(o_ref.dtype)

    one(xk_ref, pk_ref, kw1_ref, kw2_ref, ok_ref)
    one(xv_ref, pv_ref, vw1_ref, vw2_ref, ov_ref)


def nsa_compress(kc_chunks, vc_chunks, pos_k, pos_v, k_w1, k_w2, v_w1, v_w2):
    BG, n, width = kc_chunks.shape
    hid = k_w1.shape[1]
    dh = k_w2.shape[1]
    full = lambda a: pl.BlockSpec(a.shape, lambda i: (0,) * a.ndim)
    pk = pos_k.reshape(2, width)
    pv = pos_v.reshape(2, width)
    ws = [k_w1.astype(BF16), k_w2.astype(BF16), v_w1.astype(BF16), v_w2.astype(BF16)]
    del hid
    return pl.pallas_call(
        _compress_kernel,
        grid=(BG,),
        in_specs=[pl.BlockSpec((1, n, width), lambda i: (i, 0, 0)),
                  pl.BlockSpec((1, n, width), lambda i: (i, 0, 0)),
                  full(pk), full(pv)] + [full(w) for w in ws],
        out_specs=[pl.BlockSpec((1, n, dh), lambda i: (i, 0, 0)),
                   pl.BlockSpec((1, n, dh), lambda i: (i, 0, 0))],
        out_shape=[jax.ShapeDtypeStruct((BG, n, dh), BF16), jax.ShapeDtypeStruct((BG, n, dh), BF16)],
        compiler_params=_params(("parallel",)),
        name="nsa_compress",
    )(kc_chunks, vc_chunks, pk, pv, *ws)


def _nsa_attn_kernel(q_ref, gt_ref, kc_ref, vcl_ref, vch_ref, ks_ref, vsl_ref, vsh_ref, kw_ref, vwl_ref, vwh_ref,
                     ov_ref, o_ref, qs_ref, m_ref, l_ref, ae_ref, ao_ref, res_ref,
                     *, qt, ck, n_sel, top_n, sel_shift, cmp_len, cmp_step, window):
    t0 = pl.program_id(2) * qt
    R = q_ref.shape[2] // NSA_DH
    rows = R * qt
    half_rows = rows // 2
    lane = lax.broadcasted_iota(I32, (qt, LANES), 1)

    for p in range(R // 2):
        slab = q_ref[0, :, p * LANES:(p + 1) * LANES]
        qs_ref[p * qt:(p + 1) * qt, :] = jnp.where(lane < NSA_DH, slab, jnp.zeros_like(slab))
        qs_ref[half_rows + p * qt:half_rows + (p + 1) * qt, :] = jnp.where(lane >= NSA_DH, slab, jnp.zeros_like(slab))
    qs = qs_ref[...]
    tq = t0 + lax.broadcasted_iota(I32, (qt, 1), 0)

    ncmp = kc_ref.shape[0]
    cmp_end = lax.broadcasted_iota(I32, (qt, ncmp), 1) * cmp_step + (cmp_len - 1)
    s3 = _dot_nt(qs, kc_ref[...]).reshape(R, qt, ncmp) + jnp.where(cmp_end <= tq, 0.0, NEG_BIG)
    e3 = jnp.exp(s3 - jnp.max(s3, axis=-1, keepdims=True))
    row_ok = jnp.where(tq >= cmp_len - 1, 1.0, 0.0)
    pc3 = e3 * (row_ok / jnp.sum(e3, axis=-1, keepdims=True))
    pcb = pc3.reshape(rows, ncmp).astype(BF16)
    res_ref[0, 0:half_rows, :] = _dot(pcb[0:half_rows], vcl_ref[...])
    res_ref[0, half_rows:rows, :] = _dot(pcb[half_rows:], vch_ref[...])
    psum = jnp.sum(pc3, axis=0)
    imp = _dot(psum.astype(BF16), ov_ref[...])

    cur = lax.shift_right_logical(tq, sel_shift)
    dcur = cur - lane
    forced = (lane == 0) | (dcur == 0) | (dcur == 1)
    score = jnp.where(lax.shift_left(lane, sel_shift) <= tq, jnp.where(forced, NSA_FORCED, imp), NEG_BIG)
    score = jnp.where(lane < n_sel, score, -jnp.inf)
    st = score.T
    blk_f = lax.broadcasted_iota(I32, st.shape, 0).astype(F32)
    sel_t = jnp.zeros(st.shape, F32)
    for _ in range(top_n):
        mx = jnp.max(st, axis=0, keepdims=True)
        idx = jnp.min(jnp.where(st == mx, blk_f, float(LANES)), axis=0, keepdims=True)
        hit = blk_f == idx
        sel_t = jnp.where(hit, jnp.where(mx > 0.5 * NEG_BIG, 1.0, sel_t), sel_t)
        st = jnp.where(hit, -jnp.inf, st)
    selb = sel_t.T.astype(BF16)

    def attend(k_ref, vl_ref, vh_ref, c_lo, c_hi, bias_fn, slot):
        m_ref[...] = jnp.full(m_ref.shape, NEG_BIG, F32)
        l_ref[...] = jnp.zeros(l_ref.shape, F32)
        ae_ref[...] = jnp.zeros(ae_ref.shape, F32)
        ao_ref[...] = jnp.zeros(ao_ref.shape, F32)

        def body(c, carry):
            start = pl.multiple_of(c * ck, ck)
            sc3 = _dot_nt(qs, k_ref[pl.ds(start, ck), :]).reshape(R, qt, ck) + bias_fn(start)
            m_old = m_ref[...]
            m_new = jnp.maximum(m_old, jnp.max(sc3, axis=-1, keepdims=True).reshape(rows, 1))
            alpha = jnp.exp(m_old - m_new)
            p3 = jnp.exp(sc3 - m_new.reshape(R, qt, 1))
            l_ref[...] = alpha * l_ref[...] + jnp.sum(p3, axis=-1, keepdims=True).reshape(rows, 1)
            m_ref[...] = m_new
            pb = p3.reshape(rows, ck).astype(BF16)
            ae_ref[...] = alpha[0:half_rows] * ae_ref[...] + _dot(pb[0:half_rows], vl_ref[pl.ds(start, ck), :])
            ao_ref[...] = alpha[half_rows:] * ao_ref[...] + _dot(pb[half_rows:], vh_ref[pl.ds(start, ck), :])
            return carry

        lax.fori_loop(c_lo, c_hi, body, 0)
        inv = 1.0 / l_ref[...]
        res_ref[slot, 0:half_rows, :] = ae_ref[...] * inv[0:half_rows]
        res_ref[slot, half_rows:rows, :] = ao_ref[...] * inv[half_rows:]

    def key_dist(start):
        return tq - (start + lax.broadcasted_iota(I32, (qt, ck), 1))

    def sel_bias(start):
        jrow = lax.broadcasted_iota(I32, (LANES, ck), 0)
        jcol = lax.shift_right_logical(start + lax.broadcasted_iota(I32, (LANES, ck), 1), sel_shift)
        expand = jnp.where(jrow == jcol, 1.0, 0.0).astype(BF16)
        picked = _dot(selb, expand)
        return jnp.where(key_dist(start) >= 0, jnp.where(picked > 0.5, 0.0, NEG_BIG), NEG_BIG)

    def win_bias(start):
        dist = key_dist(start)
        return jnp.where(jnp.where(dist >= 0, dist, window) < window, 0.0, NEG_BIG)

    c_hi = (t0 + qt + ck - 1) // ck
    attend(ks_ref, vsl_ref, vsh_ref, 0, c_hi, sel_bias, 1)
    attend(kw_ref, vwl_ref, vwh_ref, jnp.maximum(t0 - window + 1, 0) // ck, c_hi, win_bias, 2)

    gt = gt_ref[...]
    for p in range(R // 2):
        acc = jnp.zeros((qt, LANES), F32)
        for par in range(2):
            rsl = slice(par * half_rows + p * qt, par * half_rows + (p + 1) * qt)
            for br in range(3):
                col = (2 * p + par) * 3 + br
                acc = acc + gt[:, col:col + 1] * res_ref[br, rsl, :]
        o_ref[0, :, p * LANES:(p + 1) * LANES] = acc.astype(o_ref.dtype)


def nsa_attention(q, gates, kcd, vcl, vch, kvx, *, batch, seq, qt=256, ck=512):
    G = NSA_G
    R = NSA_H // G
    gq = R * NSA_DH
    n_sel = seq // NSA_SEL_LEN
    ncmp = kcd.shape[1]
    nq = seq // qt
    ov = np.zeros((ncmp, LANES), np.float32)
    for n in range(ncmp):
        for j in range(n_sel):
            if n * NSA_CMP_STEP < (j + 1) * NSA_SEL_LEN and n * NSA_CMP_STEP + NSA_CMP_LEN > j * NSA_SEL_LEN:
                ov[n, j] = 1.0
    n_valid = (seq - NSA_CMP_LEN) // NSA_CMP_STEP + 1
    ov[n_valid:] = 0.0
    kern = functools.partial(_nsa_attn_kernel, qt=qt, ck=ck, n_sel=n_sel, top_n=min(NSA_TOPN, n_sel),
                             sel_shift=int(math.log2(NSA_SEL_LEN)), cmp_len=NSA_CMP_LEN, cmp_step=NSA_CMP_STEP, window=NSA_WIN)
    cmp_spec = pl.BlockSpec((None, ncmp, LANES), lambda b, g, i: (b * G + g, 0, 0))
    seq_specs = [pl.BlockSpec((None, None, None, seq, LANES), functools.partial(lambda b, g, i, k: (k, b, g, 0, 0), k=k))
                 for k in range(6)]
    rows = R * qt
    return pl.pallas_call(
        kern,
        grid=(batch, G, nq),
        in_specs=[pl.BlockSpec((1, qt, gq), lambda b, g, i: (b, i, g)),
                  pl.BlockSpec((qt, LANES), lambda b, g, i: (b * nq + i, g)),
                  cmp_spec, cmp_spec, cmp_spec] + seq_specs + [
                  pl.BlockSpec((ncmp, LANES), lambda b, g, i: (0, 0))],
        out_specs=pl.BlockSpec((1, qt, gq), lambda b, g, i: (b, i, g)),
        out_shape=jax.ShapeDtypeStruct((batch, seq, NSA_H * NSA_DH), BF16),
        scratch_shapes=[pltpu.VMEM((rows, LANES), BF16),
                        pltpu.VMEM((rows, 1), F32), pltpu.VMEM((rows, 1), F32),
                        pltpu.VMEM((rows // 2, LANES), F32), pltpu.VMEM((rows // 2, LANES), F32),
                        pltpu.VMEM((3, rows, LANES), F32)],
        compiler_params=_params(("parallel", "parallel", "arbitrary")),
        name="nsa_attention",
    )(q, gates, kcd, vcl, vch, kvx, kvx, kvx, kvx, kvx, kvx, jnp.asarray(ov, BF16))


def _tile_rows(ref, idx, ns):
    return ref.at[pl.ds(pl.multiple_of(idx * ns, ns), ns)]


def _router_kernel(x_ref, sc_ref, sh_ref, w_ref, b_ref, h3_ref, meta_ref, gate_ref, cnt_ref, seen_ref, *, top_k):
    @pl.when(pl.program_id(0) == 0)
    def _():
        seen_ref[...] = jnp.zeros_like(seen_ref)

    h = x_ref[...] * (1.0 + sc_ref[0]) + sh_ref[0]
    tm = h.shape[0]
    ns = h.shape[1] // LANES
    for s in range(ns):
        h3_ref[pl.ds(s, tm, stride=ns), :] = h[:, s * LANES:(s + 1) * LANES]
    logits = _dot(h.astype(BF16), w_ref[...]) + b_ref[...]
    lane = lax.broadcasted_iota(I32, logits.shape, 1)
    lane_f = lane.astype(F32)
    val_out = jnp.full(logits.shape, NEG_BIG, F32)
    chosen = jnp.zeros(logits.shape, F32)
    picks = []
    for k in range(top_k):
        mx = jnp.max(logits, axis=-1, keepdims=True)
        idx = jnp.min(jnp.where(logits == mx, lane_f, float(LANES)), axis=-1, keepdims=True)
        hit = lane_f == idx
        picks.append((idx, hit))
        chosen = jnp.where(hit, 1.0, chosen)
        val_out = jnp.where(lane == k, mx, val_out)
        logits = jnp.where(hit, -jnp.inf, logits)
    e = jnp.exp(val_out - jnp.max(val_out, axis=-1, keepdims=True))
    e = jnp.where(lane < top_k, e, 0.0)
    gate_ref[...] = e / jnp.sum(e, axis=-1, keepdims=True)
    lower = jnp.where(lax.broadcasted_iota(I32, (tm, tm), 0) > lax.broadcasted_iota(I32, (tm, tm), 1), 1.0, 0.0)
    before = _dot(lower.astype(BF16), chosen.astype(BF16)) + seen_ref[...]
    meta = jnp.zeros(logits.shape, I32)
    for k, (idx, hit) in enumerate(picks):
        rank = jnp.sum(jnp.where(hit, before, 0.0), axis=-1, keepdims=True)
        meta = jnp.where(lane == k, idx.astype(I32), meta)
        meta = jnp.where(lane == top_k + k, rank.astype(I32), meta)
    meta_ref[...] = meta
    seen_ref[...] = seen_ref[...] + jnp.sum(chosen, axis=0, keepdims=True)
    cnt_ref[...] = seen_ref[...].astype(I32)


def moe_router(x, sc, sh, router_w, router_b, *, seq, tm=512):
    T, D = x.shape
    E = router_w.shape[1]
    nb = seq // tm
    ns = D // LANES
    assert ns == 8
    wp = jnp.zeros((D, LANES), BF16).at[:, :E].set(router_w.astype(BF16))
    bp = jnp.full((1, LANES), NEG_BIG, F32).at[0, :E].set(router_b)
    per_b = pl.BlockSpec((1, 1, D), lambda i: (i // nb, 0, 0))
    return pl.pallas_call(
        functools.partial(_router_kernel, top_k=MOE_K),
        grid=(T // tm,),
        in_specs=[pl.BlockSpec((tm, D), lambda i: (i, 0)), per_b, per_b,
                  pl.BlockSpec((D, LANES), lambda i: (0, 0)),
                  pl.BlockSpec((1, LANES), lambda i: (0, 0))],
        out_specs=[pl.BlockSpec((tm * ns, LANES), lambda i: (i, 0)),
                   pl.BlockSpec((tm, LANES), lambda i: (i, 0)),
                   pl.BlockSpec((tm, LANES), lambda i: (i, 0)),
                   pl.BlockSpec((1, LANES), lambda i: (0, 0))],
        out_shape=[jax.ShapeDtypeStruct((T * ns, LANES), F32),
                   jax.ShapeDtypeStruct((T, LANES), I32),
                   jax.ShapeDtypeStruct((T, LANES), F32),
                   jax.ShapeDtypeStruct((1, LANES), I32)],
        scratch_shapes=[pltpu.VMEM((1, LANES), F32)],
        compiler_params=_params(("arbitrary",)),
        name="moe_router",
    )(x, sc, sh, wp, bp)


def _dispatch_kernel(fill_ref, slots_ref, h3_ref, xs3_ref, zero_ref, sem, *, tm, top_k, n_exp, ns):
    i = pl.program_id(0)

    @pl.when(i == 0)
    def _():
        zero_ref[...] = jnp.zeros_like(zero_ref)
        for e in range(n_exp):
            dst = xs3_ref.at[pl.ds(pl.multiple_of(fill_ref[e] * ns, ns), zero_ref.shape[0])]
            fill = pltpu.make_async_copy(zero_ref, dst, sem)
            fill.start()
            fill.wait()

    def row_copy(r, k):
        return pltpu.make_async_copy(_tile_rows(h3_ref, r, ns), _tile_rows(xs3_ref, slots_ref[r * top_k + k], ns), sem)

    def issue(r, carry):
        for k in range(top_k):
            row_copy(r, k).start()
        return carry

    def drain(r, carry):
        for k in range(top_k):
            row_copy(r, k).wait()
        return carry

    lax.fori_loop(0, tm, issue, 0, unroll=8)
    lax.fori_loop(0, tm, drain, 0, unroll=8)


def moe_dispatch(h3, slots, fill_start, n_rows, *, tm=256):
    ns = 8
    T = h3.shape[0] // ns
    K = MOE_K
    grid_spec = pltpu.PrefetchScalarGridSpec(
        num_scalar_prefetch=1,
        grid=(T // tm,),
        in_specs=[pl.BlockSpec((tm * K,), lambda i, fill: (i,), memory_space=pltpu.SMEM),
                  pl.BlockSpec((tm * ns, LANES), lambda i, fill: (i, 0))],
        out_specs=pl.BlockSpec(memory_space=pl.ANY),
        scratch_shapes=[pltpu.VMEM((MOE_BLK * ns, LANES), F32), pltpu.SemaphoreType.DMA(())],
    )
    return pl.pallas_call(
        functools.partial(_dispatch_kernel, tm=tm, top_k=K, n_exp=MOE_E, ns=ns),
        grid_spec=grid_spec,
        out_shape=jax.ShapeDtypeStruct((n_rows * ns, LANES), F32),
        compiler_params=_params(("arbitrary",)),
        name="moe_dispatch",
    )(fill_start, slots, h3)


def _expert_kernel(be_ref, nu_ref, x3_ref, w1_ref, b1_ref, w2_ref, b2_ref, o3_ref, xb_ref, w1b_ref, w2b_ref, *, dff):
    i = pl.program_id(0)
    blk = xb_ref.shape[0]
    ns = x3_ref.shape[0] // blk

    @pl.when(i < nu_ref[0])
    def _():
        @pl.when(jnp.logical_or(i == 0, be_ref[i] != be_ref[jnp.maximum(i - 1, 0)]))
        def _():
            w1b_ref[...] = w1_ref[0].astype(BF16)
            w2b_ref[...] = w2_ref[0].astype(BF16)

        for s in range(ns):
            xb_ref[:, s * LANES:(s + 1) * LANES] = x3_ref[pl.ds(s, blk, stride=ns), :].astype(BF16)
        gu = _dot(xb_ref[...], w1b_ref[...]) + b1_ref[0]
        gg = jnp.minimum(gu[:, :dff], GLU_LIMIT)
        up = jnp.clip(gu[:, dff:], -GLU_LIMIT, GLU_LIMIT)
        act = (up + 1.0) * (gg * _sigmoid(GLU_ALPHA * gg))
        y = _dot(act.astype(BF16), w2b_ref[...]) + b2_ref[0]
        for s in range(ns):
            o3_ref[pl.ds(s, blk, stride=ns), :] = y[:, s * LANES:(s + 1) * LANES]


def moe_experts(xs3, block_expert, n_used, w1, b1, w2, b2):
    E, D, two_f = w1.shape
    ns = D // LANES
    nblk = block_expert.shape[0]
    row_map = lambda i, be, nu: (jnp.minimum(i, nu[0] - 1), 0)
    grid_spec = pltpu.PrefetchScalarGridSpec(
        num_scalar_prefetch=2,
        grid=(nblk,),
        in_specs=[pl.BlockSpec((MOE_BLK * ns, LANES), row_map),
                  pl.BlockSpec((1, D, two_f), lambda i, be, nu: (be[i], 0, 0)),
                  pl.BlockSpec((1, 1, two_f), lambda i, be, nu: (be[i], 0, 0)),
                  pl.BlockSpec((1, two_f // 2, D), lambda i, be, nu: (be[i], 0, 0)),
                  pl.BlockSpec((1, 1, D), lambda i, be, nu: (be[i], 0, 0))],
        out_specs=pl.BlockSpec((MOE_BLK * ns, LANES), row_map),
        scratch_shapes=[pltpu.VMEM((MOE_BLK, D), BF16), pltpu.VMEM((D, two_f), BF16), pltpu.VMEM((two_f // 2, D), BF16)],
    )
    return pl.pallas_call(
        functools.partial(_expert_kernel, dff=two_f // 2),
        grid_spec=grid_spec,
        out_shape=jax.ShapeDtypeStruct((nblk * MOE_BLK * ns, LANES), F32),
        compiler_params=_params(("arbitrary",)),
        name="moe_experts",
    )(block_expert, n_used, xs3, w1, b1.reshape(E, 1, two_f), w2, b2.reshape(E, 1, D))


def _combine_kernel(slots_ref, nslots_ref, out3_ref, gt_ref, res_ref, gate_ref, g_ref, bb_ref, o_ref, y4_ref, sems,
                    *, tm, top_k, alpha, ns):
    i = pl.program_id(0)
    buf = i % 2

    def row_copy(slots, b, r, k):
        return pltpu.make_async_copy(_tile_rows(out3_ref, slots[r * top_k + k], ns), _tile_rows(y4_ref.at[b, k], r, ns),
                                     sems.at[b])

    def issue(slots, b):
        def body(r, carry):
            for k in range(top_k):
                row_copy(slots, b, r, k).start()
            return carry
        lax.fori_loop(0, tm, body, 0, unroll=8)

    def drain(slots, b):
        def body(r, carry):
            for k in range(top_k):
                row_copy(slots, b, r, k).wait()
            return carry
        lax.fori_loop(0, tm, body, 0, unroll=8)

    @pl.when(i == 0)
    def _():
        issue(slots_ref, 0)

    @pl.when(i + 1 < pl.num_programs(0))
    def _():
        issue(nslots_ref, 1 - buf)

    drain(slots_ref, buf)
    gt = gt_ref[...]
    cols = []
    for s in range(ns):
        acc = gt[:, 0:1] * y4_ref[buf, 0, pl.ds(s, tm, stride=ns), :]
        for k in range(1, top_k):
            acc = acc + gt[:, k:k + 1] * y4_ref[buf, k, pl.ds(s, tm, stride=ns), :]
        cols.append(acc)
    y = jnp.concatenate(cols, axis=-1)
    z = alpha * res_ref[...] + (1.0 + gate_ref[0]) * y
    o_ref[...] = _ln_rows(z, g_ref[...], bb_ref[...])


def moe_combine(out3, slots, gates, res, gate, g, b, *, seq, alpha, tm=256):
    T, D = res.shape
    K = MOE_K
    ns = D // LANES
    nb = seq // tm
    nsteps = T // tm
    one = pl.BlockSpec((1, D), lambda i: (0, 0))
    return pl.pallas_call(
        functools.partial(_combine_kernel, tm=tm, top_k=K, alpha=alpha, ns=ns),
        grid=(nsteps,),
        in_specs=[pl.BlockSpec((tm * K,), lambda i: (i,), memory_space=pltpu.SMEM),
                  pl.BlockSpec((tm * K,), lambda i: (jnp.minimum(i + 1, nsteps - 1),), memory_space=pltpu.SMEM),
                  pl.BlockSpec(memory_space=pl.ANY),
                  pl.BlockSpec((tm, LANES), lambda i: (i, 0)),
                  pl.BlockSpec((tm, D), lambda i: (i, 0)),
                  pl.BlockSpec((1, 1, D), lambda i: (i // nb, 0, 0)), one, one],
        out_specs=pl.BlockSpec((tm, D), lambda i: (i, 0)),
        out_shape=jax.ShapeDtypeStruct((T, D), F32),
        scratch_shapes=[pltpu.VMEM((2, K, tm * ns, LANES), F32), pltpu.SemaphoreType.DMA((2,))],
        compiler_params=_params(("arbitrary",)),
        name="moe_combine",
    )(slots, slots, out3, gates, res, gate, g.reshape(1, D), b.reshape(1, D))


def moe_layer(x, sc, sh, gate, ln_g, ln_b, router_w, router_b, w1, b1, w2, b2, *, seq, alpha):
    T, D = x.shape
    E, K = MOE_E, MOE_K
    h3, meta, gates, cnt = moe_router(x, sc, sh, router_w, router_b, seq=seq)
    counts = cnt[0, :E]
    blocks_per = (counts + MOE_BLK - 1) // MOE_BLK
    block_end = jnp.cumsum(blocks_per)
    pad_start = (block_end - blocks_per) * MOE_BLK
    n_blocks = -(-(T * K) // MOE_BLK) + E
    blk = jnp.arange(n_blocks, dtype=I32)
    block_expert = jnp.minimum(jnp.sum(block_end[None, :] <= blk[:, None], axis=1), E - 1).astype(I32)
    onehot = meta[:, :K, None] == jnp.arange(E, dtype=I32)[None, None, :]
    slots = (meta[:, K:2 * K] + jnp.sum(jnp.where(onehot, pad_start[None, None, :], 0), axis=-1)).reshape(-1).astype(I32)
    fill_start = (pad_start + counts).astype(I32)
    xs3 = moe_dispatch(h3, slots, fill_start, (n_blocks + 1) * MOE_BLK)
    out3 = moe_experts(xs3, block_expert, block_end[-1:].astype(I32), w1, b1, w2, b2)
    return moe_combine(out3, slots, gates, x, gate, ln_g, ln_b, seq=seq, alpha=alpha)


def _rope_tables(positions, d):
    inv = ROPE_BASE ** (-jnp.arange(0, d, 2, dtype=F32) / d)
    ang = positions.astype(F32).reshape(-1)[:, None] * inv
    return jnp.cos(ang), jnp.sin(ang)


def kernel(x, c, positions, ada_w, ada_b, ln1_g, ln1_b, ln2_g, ln2_b, router_w, router_b, moe_w1, moe_b1, moe_w2, moe_b2, ret_w_in, ret_gn_g, ret_gn_b, ret_w_out, conv_w_pw1, conv_b_pw1, conv_w_dw, conv_b_dw, conv_ln_g, conv_ln_b, conv_w_pw2, conv_b_pw2, nsa_w_in, nsa_gate_b, nsa_cmp_pos_k, nsa_cmp_pos_v, nsa_cmp_k_w1, nsa_cmp_k_w2, nsa_cmp_v_w1, nsa_cmp_v_w2, nsa_w_out, pool_w, pool_b, pool_scale):
    B, S, D = x.shape
    T = B * S
    depth = ada_w.shape[0]
    alpha = (2.0 * depth) ** 0.25
    mod = ada_modulation(c, ada_w, ada_b)
    xf = x.reshape(T, D)
    for i in range(depth):
        kind, j = i % 4, i // 4
        sh1, sc1, g1, sh2, sc2, g2 = (mod[i, :, k * D:(k + 1) * D].reshape(B, 1, D) for k in range(6))
        res1 = (g1, ln1_g[i], ln1_b[i])
        if kind == 0:
            dk = ret_w_in.shape[2] // 6 // RET_H
            cos, sin = _rope_tables(positions, dk)
            p = ret_projection(xf, sc1, sh1, ret_w_in[j].astype(BF16), cos, sin, seq=S)
            o = retention_core(p, ret_gn_g[j], ret_gn_b[j], batch=B, seq=S)
            xf = fused_matmul(o, ret_w_out[j].astype(BF16), seq=S, resln=(xf,) + res1, alpha=alpha, name="ret_out")
        elif kind == 1:
            u = conv_glu(xf, sc1, sh1, conv_w_pw1[j].astype(BF16), conv_b_pw1[j], seq=S)
            xf = conv_tail(u, conv_w_dw[j], conv_b_dw[j], conv_ln_g[j], conv_ln_b[j], conv_w_pw2[j].astype(BF16),
                           conv_b_pw2[j], xf, *res1, seq=S, alpha=alpha)
        elif kind == 2:
            G, dh, H = NSA_G, NSA_DH, NSA_H
            cos, sin = _rope_tables(positions, dh)
            cos_p = jnp.tile(cos, (1, LANES // (dh // 2)))
            sin_p = jnp.tile(jnp.concatenate([-sin, sin], axis=1), (1, LANES // dh))
            n_main = H * dh + 6 * G * dh
            ng = 3 * H // G
            w_g = jnp.zeros((D, G, LANES), F32).at[:, :, :ng].set(nsa_w_in[j][:, n_main:].reshape(D, G, ng))
            w_p = jnp.concatenate([nsa_w_in[j][:, :n_main], w_g.reshape(D, G * LANES)], axis=1).astype(BF16)
            gb_p = jnp.zeros((G, LANES), F32).at[:, :ng].set(nsa_gate_b[j].reshape(G, ng)).reshape(1, G * LANES)
            q, cv, kvx, gates = nsa_projection(xf, sc1, sh1, w_p, cos_p, sin_p, gb_p, seq=S)
            cv = cv.reshape(B, S, 2, G, dh).transpose(2, 0, 3, 1, 4).reshape(2, B * G, S // NSA_CMP_STEP, NSA_CMP_STEP * dh)
            k_cmp, v_cmp = nsa_compress(cv[0], cv[1], nsa_cmp_pos_k[j], nsa_cmp_pos_v[j], nsa_cmp_k_w1[j],
                                        nsa_cmp_k_w2[j], nsa_cmp_v_w1[j], nsa_cmp_v_w2[j])
            dup = lambda a: jnp.concatenate([a, a], axis=-1)
            lo = lambda a: jnp.concatenate([a, jnp.zeros_like(a)], axis=-1)
            hi = lambda a: jnp.concatenate([jnp.zeros_like(a), a], axis=-1)
            o = nsa_attention(q.reshape(B, S, H * dh), gates, dup(k_cmp), lo(v_cmp), hi(v_cmp), kvx, batch=B, seq=S)
            xf = fused_matmul(o.reshape(T, H * dh), nsa_w_out[j].astype(BF16), seq=S, resln=(xf,) + res1, alpha=alpha,
                              name="nsa_out")
        else:
            xf = pool_layer(xf, sc1, sh1, pool_w[j], pool_b[j], pool_scale[j], *res1, seq=S, alpha=alpha)
        xf = moe_layer(xf, sc2, sh2, g2, ln2_g[i], ln2_b[i], router_w[i], router_b[i], moe_w1[i], moe_b1[i],
                       moe_w2[i], moe_b2[i], seq=S, alpha=alpha)
    return xf.reshape(B, S, D)
```

```python
import functools
import math

import jax
import jax.numpy as jnp
import numpy as np
from jax import lax
from jax.experimental import pallas as pl
from jax.experimental.pallas import tpu as pltpu

F32 = jnp.float32
BF16 = jnp.bfloat16
I32 = jnp.int32

ROPE_BASE = 10000.0
EPS = 1e-5
NEG_BIG = -1e30

RET_H = 4
RET_C = 128
CONV_W = 31
NSA_H = 16
NSA_G = 2
NSA_DH = 64
NSA_CMP_LEN = 32
NSA_CMP_STEP = 16
NSA_SEL_LEN = 64
NSA_TOPN = 16
NSA_WIN = 512
NSA_FORCED = 1e6
POOL_WINS = (2, 4, 8, 16)
MOE_E = 32
MOE_K = 4
MOE_BLK = 256
GLU_LIMIT = 7.0
GLU_ALPHA = 1.702

V7X_VMEM_LIMIT = 56 * 1024 * 1024
LANES = 128


def _params(sem):
    return pltpu.CompilerParams(dimension_semantics=sem, vmem_limit_bytes=V7X_VMEM_LIMIT)


def _ln_rows(v, g, b):
    mu = jnp.mean(v, axis=-1, keepdims=True)
    d = v - mu
    var = jnp.mean(d * d, axis=-1, keepdims=True)
    return d * lax.rsqrt(var + EPS) * g + b


def _sigmoid(v):
    return 1.0 / (1.0 + jnp.exp(-v))


def _dot(a, b):
    return jnp.dot(a, b, preferred_element_type=F32)


def _dot_nt(a, b):
    return lax.dot_general(a, b, (((1,), (1,)), ((), ())), preferred_element_type=F32)


def _dot_tn(a, b):
    return lax.dot_general(a, b, (((0,), (0,)), ((), ())), preferred_element_type=F32)


def _ada_kernel(c_ref, w_ref, b_ref, o_ref):
    c = c_ref[...]
    ca = c * _sigmoid(c)
    o_ref[0] = _dot(ca.astype(BF16), w_ref[0].astype(BF16)) + b_ref[0]


def ada_modulation(c, ada_w, ada_b):
    L, D, N = ada_w.shape
    B = c.shape[0]
    tn = 2048 if N % 2048 == 0 else N
    return pl.pallas_call(
        _ada_kernel,
        grid=(L, N // tn),
        in_specs=[pl.BlockSpec((B, D), lambda l, j: (0, 0)),
                  pl.BlockSpec((1, D, tn), lambda l, j: (l, 0, j)),
                  pl.BlockSpec((1, 1, tn), lambda l, j: (l, 0, j))],
        out_specs=pl.BlockSpec((1, B, tn), lambda l, j: (l, 0, j)),
        out_shape=jax.ShapeDtypeStruct((L, B, N), F32),
        compiler_params=_params(("parallel", "parallel")),
        name="ada_modulation",
    )(c, ada_w, ada_b.reshape(L, 1, N))


def _mm_kernel(*refs, has_mod, has_bias, epi, alpha):
    it = iter(refs)
    x_ref = next(it)
    if has_mod:
        sc_ref, sh_ref = next(it), next(it)
    w_ref = next(it)
    if has_bias:
        b_ref = next(it)
    if epi == "resln":
        res_ref, gate_ref, g_ref, bb_ref = next(it), next(it), next(it), next(it)
    o_ref = next(it)
    x = x_ref[...]
    if has_mod:
        x = x.astype(F32) * (1.0 + sc_ref[0]) + sh_ref[0]
    acc = _dot(x.astype(BF16), w_ref[...])
    if has_bias:
        acc = acc + b_ref[...]
    if epi == "resln":
        v = alpha * res_ref[...] + (1.0 + gate_ref[0]) * acc
        acc = _ln_rows(v, g_ref[...], bb_ref[...])
    o_ref[...] = acc.astype(o_ref.dtype)


def fused_matmul(x, w, *, seq, mod=None, bias=None, resln=None, alpha=1.0, out_dtype=F32, tm=512, tn=None, name="mm"):
    T, K = x.shape
    N = w.shape[1]
    tn = N if tn is None else tn
    assert T % tm == 0 and seq % tm == 0 and N % tn == 0
    nb = seq // tm
    args, specs = [x], [pl.BlockSpec((tm, K), lambda i, j: (i, 0))]
    if mod is not None:
        for m in mod:
            args.append(m)
            specs.append(pl.BlockSpec((1, 1, K), lambda i, j: (i // nb, 0, 0)))
    args.append(w)
    specs.append(pl.BlockSpec((K, tn), lambda i, j: (0, j)))
    if bias is not None:
        args.append(bias.reshape(1, N))
        specs.append(pl.BlockSpec((1, tn), lambda i, j: (0, j)))
    epi = "none"
    if resln is not None:
        assert tn == N
        res, gate, g, b = resln
        epi = "resln"
        args += [res, gate, g.reshape(1, N), b.reshape(1, N)]
        specs += [pl.BlockSpec((tm, N), lambda i, j: (i, 0)),
                  pl.BlockSpec((1, 1, N), lambda i, j: (i // nb, 0, 0)),
                  pl.BlockSpec((1, N), lambda i, j: (0, 0)),
                  pl.BlockSpec((1, N), lambda i, j: (0, 0))]
    kern = functools.partial(_mm_kernel, has_mod=mod is not None, has_bias=bias is not None, epi=epi, alpha=alpha)
    return pl.pallas_call(
        kern,
        grid=(T // tm, N // tn),
        in_specs=specs,
        out_specs=pl.BlockSpec((tm, tn), lambda i, j: (i, j)),
        out_shape=jax.ShapeDtypeStruct((T, N), out_dtype),
        compiler_params=_params(("parallel", "parallel")),
        name=name,
    )(*args)


def _ret_proj_kernel(x_ref, sc_ref, sh_ref, w_ref, cos_ref, sin_ref, o_ref, *, dk, hdk, tn, kscale):
    h = (x_ref[...] * (1.0 + sc_ref[0]) + sh_ref[0]).astype(BF16)
    cos, sin = cos_ref[...], sin_ref[...]
    half = dk // 2
    for j in range(o_ref.shape[1] // tn):
        c0 = j * tn
        acc = _dot(h, w_ref[:, c0:c0 + tn])
        if c0 < 2 * hdk:
            scale = 1.0 if c0 < hdk else kscale
            for hh in range(tn // dk):
                x1 = acc[:, hh * dk:hh * dk + half]
                x2 = acc[:, hh * dk + half:(hh + 1) * dk]
                o_ref[:, c0 + hh * dk:c0 + hh * dk + half] = ((x1 * cos - x2 * sin) * scale).astype(o_ref.dtype)
                o_ref[:, c0 + hh * dk + half:c0 + (hh + 1) * dk] = ((x1 * sin + x2 * cos) * scale).astype(o_ref.dtype)
        elif c0 < 4 * hdk:
            o_ref[:, c0:c0 + tn] = acc.astype(o_ref.dtype)
        else:
            o_ref[:, c0:c0 + tn] = (acc * _sigmoid(acc)).astype(o_ref.dtype)


def ret_projection(x, sc, sh, w_in, cos, sin, *, seq, tm=512, tn=1024):
    T, K = x.shape
    N = w_in.shape[1]
    hdk = N // 6
    dk = hdk // RET_H
    assert hdk % tn == 0 and tn % dk == 0
    nb = seq // tm
    kern = functools.partial(_ret_proj_kernel, dk=dk, hdk=hdk, tn=tn, kscale=dk ** -0.5)
    return pl.pallas_call(
        kern,
        grid=(T // tm,),
        in_specs=[pl.BlockSpec((tm, K), lambda i: (i, 0)),
                  pl.BlockSpec((1, 1, K), lambda i: (i // nb, 0, 0)),
                  pl.BlockSpec((1, 1, K), lambda i: (i // nb, 0, 0)),
                  pl.BlockSpec((K, N), lambda i: (0, 0), pipeline_mode=pl.Buffered(1)),
                  pl.BlockSpec((tm, dk // 2), lambda i: (i, 0)),
                  pl.BlockSpec((tm, dk // 2), lambda i: (i, 0))],
        out_specs=pl.BlockSpec((tm, N), lambda i: (i, 0)),
        out_shape=jax.ShapeDtypeStruct((T, N), BF16),
        compiler_params=_params(("parallel",)),
        name="ret_projection",
    )(x, sc, sh, w_in, cos, sin)


def _retention_kernel(q_ref, k_ref, v_ref, sg_ref, din_ref, dq_ref, dkk_ref, dch_ref, gg_ref, gb_ref,
                      o_ref, state_ref, *, H, C, dk, dv):
    @pl.when(pl.program_id(1) == 0)
    def _():
        state_ref[...] = jnp.zeros_like(state_ref)

    n_chunks = q_ref.shape[0] // C
    for c in range(n_chunks):
        rows = slice(c * C, (c + 1) * C)
        for hh in range(H):
            qc = q_ref[rows, hh * dk:(hh + 1) * dk]
            kc = k_ref[rows, hh * dk:(hh + 1) * dk]
            vc = v_ref[rows, hh * dv:(hh + 1) * dv]
            st = state_ref[hh]
            a = _dot_nt(qc, kc) * din_ref[hh]
            o = _dot(a.astype(BF16), vc)
            qd = (qc.astype(F32) * dq_ref[hh]).astype(BF16)
            o = o + _dot(qd, st.astype(BF16))
            kd = (kc.astype(F32) * dkk_ref[hh]).astype(BF16)
            state_ref[hh] = st * dch_ref[hh, :, 0:1] + _dot_tn(kd, vc)
            mu = jnp.mean(o, axis=-1, keepdims=True)
            d = o - mu
            var = jnp.mean(d * d, axis=-1, keepdims=True)
            on = d * lax.rsqrt(var + EPS) * gg_ref[:, hh * dv:(hh + 1) * dv] + gb_ref[:, hh * dv:(hh + 1) * dv]
            sg = sg_ref[rows, hh * dv:(hh + 1) * dv].astype(F32)
            o_ref[rows, hh * dv:(hh + 1) * dv] = (sg * on).astype(o_ref.dtype)


def retention_core(p, gn_g, gn_b, *, batch, seq, tb=512):
    T, N = p.shape
    H, C = RET_H, RET_C
    hdk = N // 6
    dk, dv = hdk // H, 2 * hdk // H
    hdv = H * dv
    nb = seq // tb
    log_gamma = jnp.log1p(-(2.0 ** (-5.0 - jnp.arange(H, dtype=F32))))
    i = jnp.arange(C, dtype=F32)
    rel = i[:, None] - i[None, :]
    d_intra = jnp.where(rel >= 0, jnp.exp(log_gamma[:, None, None] * jnp.maximum(rel, 0.0)), 0.0).astype(F32)
    d_q = jnp.exp(log_gamma[:, None] * (i + 1.0)).astype(F32)[..., None]
    d_k = jnp.exp(log_gamma[:, None] * (C - 1.0 - i)).astype(F32)[..., None]
    d_ch = jnp.broadcast_to(jnp.exp(log_gamma * C).astype(F32)[:, None, None], (H, 1, LANES))
    kern = functools.partial(_retention_kernel, H=H, C=C, dk=dk, dv=dv)
    return pl.pallas_call(
        kern,
        grid=(batch, nb),
        in_specs=[pl.BlockSpec((tb, hdk), lambda b, s: (b * nb + s, 0)),
                  pl.BlockSpec((tb, hdk), lambda b, s: (b * nb + s, 1)),
                  pl.BlockSpec((tb, hdv), lambda b, s: (b * nb + s, 1)),
                  pl.BlockSpec((tb, hdv), lambda b, s: (b * nb + s, 2)),
                  pl.BlockSpec((H, C, C), lambda b, s: (0, 0, 0)),
                  pl.BlockSpec((H, C, 1), lambda b, s: (0, 0, 0)),
                  pl.BlockSpec((H, C, 1), lambda b, s: (0, 0, 0)),
                  pl.BlockSpec((H, 1, LANES), lambda b, s: (0, 0, 0)),
                  pl.BlockSpec((1, hdv), lambda b, s: (0, 0)),
                  pl.BlockSpec((1, hdv), lambda b, s: (0, 0))],
        out_specs=pl.BlockSpec((tb, hdv), lambda b, s: (b * nb + s, 0)),
        out_shape=jax.ShapeDtypeStruct((T, hdv), BF16),
        scratch_shapes=[pltpu.VMEM((H, dk, dv), F32)],
        compiler_params=_params(("parallel", "arbitrary")),
        name="retention_core",
    )(p, p, p, p, d_intra, d_q, d_k, d_ch, gn_g.reshape(1, hdv), gn_b.reshape(1, hdv))


def _glu_kernel(x_ref, sc_ref, sh_ref, wa_ref, wg_ref, ba_ref, bg_ref, o_ref):
    h = (x_ref[...] * (1.0 + sc_ref[0]) + sh_ref[0]).astype(BF16)
    a = _dot(h, wa_ref[...]) + ba_ref[...]
    g = _dot(h, wg_ref[...]) + bg_ref[...]
    o_ref[...] = a * _sigmoid(g)


def conv_glu(x, sc, sh, w_pw1, b_pw1, *, seq, tm=512, tn=512):
    T, K = x.shape
    N = w_pw1.shape[1] // 2
    nb = seq // tm
    ng = N // tn
    b2 = b_pw1.reshape(1, 2 * N)
    return pl.pallas_call(
        _glu_kernel,
        grid=(T // tm, ng),
        in_specs=[pl.BlockSpec((tm, K), lambda i, j: (i, 0)),
                  pl.BlockSpec((1, 1, K), lambda i, j: (i // nb, 0, 0)),
                  pl.BlockSpec((1, 1, K), lambda i, j: (i // nb, 0, 0)),
                  pl.BlockSpec((K, tn), lambda i, j: (0, j)),
                  pl.BlockSpec((K, tn), lambda i, j: (0, j + ng)),
                  pl.BlockSpec((1, tn), lambda i, j: (0, j)),
                  pl.BlockSpec((1, tn), lambda i, j: (0, j + ng))],
        out_specs=pl.BlockSpec((tm, tn), lambda i, j: (i, j)),
        out_shape=jax.ShapeDtypeStruct((T, N), F32),
        compiler_params=_params(("parallel", "parallel")),
        name="conv_glu",
    )(x, sc, sh, w_pw1, w_pw1, b2, b2)


def _conv_tail_kernel(u_ref, up_ref, wdw_ref, bdw_ref, lg_ref, lb_ref, w2_ref, b2_ref,
                      res_ref, gate_ref, g_ref, bb_ref, o_ref, win_ref, shift_ref, *, nb, halo, width, alpha):
    tm = u_ref.shape[0]
    first = (pl.program_id(0) % nb) == 0
    prev = up_ref[...]
    win_ref[0:halo, :] = jnp.where(first, jnp.zeros_like(prev), prev)
    win_ref[halo:halo + tm, :] = u_ref[...]
    for b in range(1, 8):
        n = (tm + halo - b) // 8 * 8
        shift_ref[b - 1, 0:n, :] = win_ref[b:b + n, :]
    acc = jnp.zeros(u_ref.shape, F32)
    off = halo - (width - 1)
    for k in range(width):
        b, a = (off + k) % 8, (off + k) // 8 * 8
        rows = win_ref[a:a + tm, :] if b == 0 else shift_ref[b - 1, a:a + tm, :]
        acc = acc + wdw_ref[k:k + 1, :] * rows
    acc = acc + bdw_ref[...]
    v = _ln_rows(acc, lg_ref[...], lb_ref[...])
    v = v * _sigmoid(v)
    y = _dot(v.astype(BF16), w2_ref[...]) + b2_ref[...]
    z = alpha * res_ref[...] + (1.0 + gate_ref[0]) * y
    o_ref[...] = _ln_rows(z, g_ref[...], bb_ref[...])


def conv_tail(u, w_dw, b_dw, ln_g, ln_b, w_pw2, b_pw2, res, gate, g, b, *, seq, alpha, tm=512):
    T, D = u.shape
    nb = seq // tm
    halo = 32
    assert CONV_W - 1 <= halo and tm % halo == 0
    r = tm // halo
    wpad = jnp.zeros((halo, D), F32).at[:CONV_W].set(w_dw)
    kern = functools.partial(_conv_tail_kernel, nb=nb, halo=halo, width=CONV_W, alpha=alpha)
    vec = lambda a: a.reshape(1, D)
    one = pl.BlockSpec((1, D), lambda i: (0, 0))
    return pl.pallas_call(
        kern,
        grid=(T // tm,),
        in_specs=[pl.BlockSpec((tm, D), lambda i: (i, 0)),
                  pl.BlockSpec((halo, D), lambda i: (jnp.maximum(i * r - 1, 0), 0)),
                  pl.BlockSpec((halo, D), lambda i: (0, 0)),
                  one, one, one,
                  pl.BlockSpec((D, D), lambda i: (0, 0)),
                  one,
                  pl.BlockSpec((tm, D), lambda i: (i, 0)),
                  pl.BlockSpec((1, 1, D), lambda i: (i // nb, 0, 0)),
                  one, one],
        out_specs=pl.BlockSpec((tm, D), lambda i: (i, 0)),
        out_shape=jax.ShapeDtypeStruct((T, D), F32),
        scratch_shapes=[pltpu.VMEM((halo + tm, D), F32), pltpu.VMEM((7, halo + tm, D), F32)],
        compiler_params=_params(("parallel",)),
        name="conv_tail",
    )(u, u, wpad, vec(b_dw), vec(ln_g), vec(ln_b), w_pw2, vec(b_pw2), res, gate, vec(g), vec(b))


def _pool_kernel(x_ref, xp_ref, sc_ref, sh_ref, w_ref, pb_ref, ps_ref, gate_ref, g_ref, bb_ref,
                 o_ref, win_ref, *, nb, halo, wins, alpha):
    tm, D = x_ref.shape
    gw = D // len(wins)
    i = pl.program_id(0)
    first = (i % nb) == 0
    x = x_ref[...]
    sc, sh = sc_ref[0], sh_ref[0]
    hp = xp_ref[...] * (1.0 + sc) + sh
    win_ref[0:halo, :] = jnp.where(first, jnp.zeros_like(hp), hp)
    win_ref[halo:halo + tm, :] = x * (1.0 + sc) + sh
    t = (i % nb) * tm + lax.broadcasted_iota(I32, (tm, gw), 0)
    ys = []
    for gi, wn in enumerate(wins):
        cols = slice(gi * gw, (gi + 1) * gw)
        s = win_ref[halo:halo + tm, cols]
        hcur = s
        for k in range(1, wn):
            s = s + win_ref[halo - k:halo - k + tm, cols]
        cnt = jnp.minimum(t + 1, wn).astype(F32)
        pooled = s / cnt - hcur
        ys.append(_dot(pooled.astype(BF16), w_ref[gi]))
    y = (jnp.concatenate(ys, axis=-1) + pb_ref[...]) * ps_ref[...]
    z = alpha * x + (1.0 + gate_ref[0]) * y
    o_ref[...] = _ln_rows(z, g_ref[...], bb_ref[...])


def pool_layer(x, sc, sh, pool_w, pool_b, pool_scale, gate, g, b, *, seq, alpha, tm=512):
    T, D = x.shape
    nb = seq // tm
    halo = 16
    assert max(POOL_WINS) <= halo and tm % halo == 0
    r = tm // halo
    G, gw, _ = pool_w.shape
    kern = functools.partial(_pool_kernel, nb=nb, halo=halo, wins=POOL_WINS, alpha=alpha)
    vec = lambda a: a.reshape(1, D)
    one = pl.BlockSpec((1, D), lambda i: (0, 0))
    per_b = pl.BlockSpec((1, 1, D), lambda i: (i // nb, 0, 0))
    return pl.pallas_call(
        kern,
        grid=(T // tm,),
        in_specs=[pl.BlockSpec((tm, D), lambda i: (i, 0)),
                  pl.BlockSpec((halo, D), lambda i: (jnp.maximum(i * r - 1, 0), 0)),
                  per_b, per_b,
                  pl.BlockSpec((G, gw, gw), lambda i: (0, 0, 0)),
                  one, one, per_b, one, one],
        out_specs=pl.BlockSpec((tm, D), lambda i: (i, 0)),
        out_shape=jax.ShapeDtypeStruct((T, D), F32),
        scratch_shapes=[pltpu.VMEM((halo + tm, D), F32)],
        compiler_params=_params(("parallel",)),
        name="pool_layer",
    )(x, x, sc, sh, pool_w.astype(BF16), vec(pool_b), vec(pool_scale), gate, vec(g), vec(b))


def _rope64(x, cos, sin_signed, lane):
    partner = jnp.where((lane & (NSA_DH - 1)) < NSA_DH // 2, pltpu.roll(x, LANES - NSA_DH // 2, 1), pltpu.roll(x, NSA_DH // 2, 1))
    return x * cos + partner * sin_signed


def _nsa_proj_kernel(x_ref, sc_ref, sh_ref, w_ref, cos_ref, sin_ref, gb_ref, q_ref, cv_ref, kvx_ref, gt_ref, *, nq, qscale):
    h = x_ref[...] * (1.0 + sc_ref[0]) + sh_ref[0]
    acc = _dot(h.astype(BF16), w_ref[...])
    cos, sin = cos_ref[...], sin_ref[...]
    lane = lax.broadcasted_iota(I32, cos.shape, 1)
    for s in range(nq):
        blk = acc[:, s * LANES:(s + 1) * LANES]
        q_ref[:, s * LANES:(s + 1) * LANES] = (_rope64(blk, cos, sin, lane) * qscale).astype(q_ref.dtype)
    low = lane < NSA_DH
    for s in range(6):
        blk = acc[:, (nq + s) * LANES:(nq + s + 1) * LANES]
        if s % 2 == 0:
            blk = _rope64(blk, cos, sin, lane)
        if s < 2:
            cv_ref[:, s * LANES:(s + 1) * LANES] = blk.astype(cv_ref.dtype)
            continue
        swapped = pltpu.roll(blk, NSA_DH, 1)
        for gi in range(2):
            own, other = (blk, swapped) if gi == 0 else (swapped, blk)
            if s % 2 == 0:
                kvx_ref[3 * (s // 2 - 1), 0, gi] = jnp.where(low, own, other).astype(kvx_ref.dtype)
            else:
                kvx_ref[3 * (s // 2 - 1) + 1, 0, gi] = jnp.where(low, own, 0.0).astype(kvx_ref.dtype)
                kvx_ref[3 * (s // 2 - 1) + 2, 0, gi] = jnp.where(low, 0.0, other).astype(kvx_ref.dtype)
    for gi in range(gt_ref.shape[1] // LANES):
        gl = acc[:, (nq + 6 + gi) * LANES:(nq + 7 + gi) * LANES]
        gt_ref[:, gi * LANES:(gi + 1) * LANES] = _sigmoid(gl + gb_ref[:, gi * LANES:(gi + 1) * LANES])


def nsa_projection(x, sc, sh, w_in_p, cos, sin_signed, gate_b_p, *, seq, tm=512):
    T, K = x.shape
    Np = w_in_p.shape[1]
    nq = NSA_H * NSA_DH // LANES
    assert NSA_G * NSA_DH == LANES
    ng = NSA_G * LANES
    nb = seq // tm
    kern = functools.partial(_nsa_proj_kernel, nq=nq, qscale=NSA_DH ** -0.5)
    per_b = pl.BlockSpec((1, 1, K), lambda i: (i // nb, 0, 0))
    return pl.pallas_call(
        kern,
        grid=(T // tm,),
        in_specs=[pl.BlockSpec((tm, K), lambda i: (i, 0)), per_b, per_b,
                  pl.BlockSpec((K, Np), lambda i: (0, 0)),
                  pl.BlockSpec((tm, LANES), lambda i: (i, 0)),
                  pl.BlockSpec((tm, LANES), lambda i: (i, 0)),
                  pl.BlockSpec((1, ng), lambda i: (0, 0))],
        out_specs=[pl.BlockSpec((tm, nq * LANES), lambda i: (i, 0)),
                   pl.BlockSpec((tm, 2 * LANES), lambda i: (i, 0)),
                   pl.BlockSpec((6, 1, NSA_G, tm, LANES), lambda i: (0, i // nb, 0, i % nb, 0)),
                   pl.BlockSpec((tm, ng), lambda i: (i, 0))],
        out_shape=[jax.ShapeDtypeStruct((T, nq * LANES), BF16),
                   jax.ShapeDtypeStruct((T, 2 * LANES), BF16),
                   jax.ShapeDtypeStruct((6, T // seq, NSA_G, seq, LANES), BF16),
                   jax.ShapeDtypeStruct((T, ng), F32)],
        compiler_params=_params(("parallel",)),
        name="nsa_projection",
    )(x, sc, sh, w_in_p, cos, sin_signed, gate_b_p)


def _gelu_tanh(v):
    return 0.5 * v * (1.0 + jnp.tanh(math.sqrt(2.0 / math.pi) * (v + 0.044715 * v * v * v)))


def _compress_kernel(xk_ref, xv_ref, pk_ref, pv_ref, kw1_ref, kw2_ref, vw1_ref, vw2_ref, ok_ref, ov_ref):
    def one(x_ref, pe_ref, w1_ref, w2_ref, o_ref):
        x = x_ref[0].astype(F32)
        n = x.shape[0]
        half = x.shape[1]
        a = _dot((x + pe_ref[0:1, :]).astype(BF16), w1_ref[0:half, :])
        b = _dot((x + pe_ref[1:2, :]).astype(BF16), w1_ref[half:2 * half, :])
        pre = a + pltpu.roll(b, n - 1, 0)
        o_ref[0] = _dot(_gelu_tanh(pre).astype(BF16), w2_ref[...]).astype(o_ref.dtype)

    one(xk_ref, pk_ref, kw1_ref, kw2_ref, ok_ref)
    one(xv_ref, pv_ref, vw1_ref, vw2_ref, ov_ref)


def nsa_compress(kc_chunks, vc_chunks, pos_k, pos_v, k_w1, k_w2, v_w1, v_w2):
    BG, n, width = kc_chunks.shape
    hid = k_w1.shape[1]
    dh = k_w2.shape[1]
    full = lambda a: pl.BlockSpec(a.shape, lambda i: (0,) * a.ndim)
    pk = pos_k.reshape(2, width)
    pv = pos_v.reshape(2, width)
    ws = [k_w1.astype(BF16), k_w2.astype(BF16), v_w1.astype(BF16), v_w2.astype(BF16)]
    del hid
    return pl.pallas_call(
        _compress_kernel,
        grid=(BG,),
        in_specs=[pl.BlockSpec((1, n, width), lambda i: (i, 0, 0)),
                  pl.BlockSpec((1, n, width), lambda i: (i, 0, 0)),
                  full(pk), full(pv)] + [full(w) for w in ws],
        out_specs=[pl.BlockSpec((1, n, dh), lambda i: (i, 0, 0)),
                   pl.BlockSpec((1, n, dh), lambda i: (i, 0, 0))],
        out_shape=[jax.ShapeDtypeStruct((BG, n, dh), BF16), jax.ShapeDtypeStruct((BG, n, dh), BF16)],
        compiler_params=_params(("parallel",)),
        name="nsa_compress",
    )(kc_chunks, vc_chunks, pk, pv, *ws)


def _nsa_attn_kernel(q_ref, gt_ref, kc_ref, vcl_ref, vch_ref, ks_ref, vsl_ref, vsh_ref, kw_ref, vwl_ref, vwh_ref,
                     ov_ref, o_ref, qs_ref, m_ref, l_ref, ae_ref, ao_ref, res_ref, sc_ref,
                     *, qt, ck, n_sel, top_n, sel_shift, cmp_len, cmp_step, window):
    t0 = pl.program_id(2) * qt
    R = q_ref.shape[2] // NSA_DH
    rows = R * qt
    half_rows = rows // 2
    lane = lax.broadcasted_iota(I32, (qt, LANES), 1)

    for p in range(R // 2):
        slab = q_ref[0, :, p * LANES:(p + 1) * LANES]
        qs_ref[p * qt:(p + 1) * qt, :] = jnp.where(lane < NSA_DH, slab, jnp.zeros_like(slab))
        qs_ref[half_rows + p * qt:half_rows + (p + 1) * qt, :] = jnp.where(lane >= NSA_DH, slab, jnp.zeros_like(slab))
    qs = qs_ref[...]
    tq = t0 + lax.broadcasted_iota(I32, (qt, 1), 0)

    ncmp = kc_ref.shape[0]
    cmp_end = lax.broadcasted_iota(I32, (qt, ncmp), 1) * cmp_step + (cmp_len - 1)
    s3 = _dot_nt(qs, kc_ref[...]).reshape(R, qt, ncmp) + jnp.where(cmp_end <= tq, 0.0, NEG_BIG)
    e3 = jnp.exp(s3 - jnp.max(s3, axis=-1, keepdims=True))
    row_ok = jnp.where(tq >= cmp_len - 1, 1.0, 0.0)
    pc3 = e3 * (row_ok / jnp.sum(e3, axis=-1, keepdims=True))
    pcb = pc3.reshape(rows, ncmp).astype(BF16)
    res_ref[0, 0:half_rows, :] = _dot(pcb[0:half_rows], vcl_ref[...])
    res_ref[0, half_rows:rows, :] = _dot(pcb[half_rows:], vch_ref[...])
    psum = jnp.sum(pc3, axis=0)
    imp = _dot(psum.astype(BF16), ov_ref[...])

    cur = lax.shift_right_logical(tq, sel_shift)
    dcur = cur - lane
    forced = (lane == 0) | (dcur == 0) | (dcur == 1)
    score = jnp.where(lax.shift_left(lane, sel_shift) <= tq, jnp.where(forced, NSA_FORCED, imp), NEG_BIG)
    score = jnp.where(lane < n_sel, score, -jnp.inf)
    st = score.T
    blk_f = lax.broadcasted_iota(I32, st.shape, 0).astype(F32)
    sel_t = jnp.zeros(st.shape, F32)
    for _ in range(top_n):
        mx = jnp.max(st, axis=0, keepdims=True)
        idx = jnp.min(jnp.where(st == mx, blk_f, float(LANES)), axis=0, keepdims=True)
        hit = blk_f == idx
        sel_t = jnp.where(hit, jnp.where(mx > 0.5 * NEG_BIG, 1.0, sel_t), sel_t)
        st = jnp.where(hit, -jnp.inf, st)
    selb = sel_t.T.astype(BF16)

    def attend(k_ref, vl_ref, vh_ref, c_lo, c_hi, bias_fn, slot):
        m_ref[...] = jnp.full(m_ref.shape, NEG_BIG, F32)
        l_ref[...] = jnp.zeros(l_ref.shape, F32)
        ae_ref[...] = jnp.zeros(ae_ref.shape, F32)
        ao_ref[...] = jnp.zeros(ao_ref.shape, F32)

        def scores(c):
            start = pl.multiple_of(c * ck, ck)
            return _dot_nt(qs, k_ref[pl.ds(start, ck), :]).reshape(R, qt, ck) + bias_fn(start)

        sc_ref[...] = scores(c_lo)

        def body(c, carry):
            start = pl.multiple_of(c * ck, ck)
            sc_next = scores(jnp.minimum(c + 1, c_hi - 1))
            sc3 = sc_ref[...]
            m_old = m_ref[...]
            m_new = jnp.maximum(m_old, jnp.max(sc3, axis=-1, keepdims=True).reshape(rows, 1))
            alpha = jnp.exp(m_old - m_new)
            p3 = jnp.exp(sc3 - m_new.reshape(R, qt, 1))
            l_ref[...] = alpha * l_ref[...] + jnp.sum(p3, axis=-1, keepdims=True).reshape(rows, 1)
            m_ref[...] = m_new
            pb = p3.reshape(rows, ck).astype(BF16)
            ae_ref[...] = alpha[0:half_rows] * ae_ref[...] + _dot(pb[0:half_rows], vl_ref[pl.ds(start, ck), :])
            ao_ref[...] = alpha[half_rows:] * ao_ref[...] + _dot(pb[half_rows:], vh_ref[pl.ds(start, ck), :])
            sc_ref[...] = sc_next
            return carry

        lax.fori_loop(c_lo, c_hi, body, 0)
        inv = 1.0 / l_ref[...]
        res_ref[slot, 0:half_rows, :] = ae_ref[...] * inv[0:half_rows]
        res_ref[slot, half_rows:rows, :] = ao_ref[...] * inv[half_rows:]

    def key_dist(start):
        return tq - (start + lax.broadcasted_iota(I32, (qt, ck), 1))

    def sel_bias(start):
        jrow = lax.broadcasted_iota(I32, (LANES, ck), 0)
        jcol = lax.shift_right_logical(start + lax.broadcasted_iota(I32, (LANES, ck), 1), sel_shift)
        expand = jnp.where(jrow == jcol, 1.0, 0.0).astype(BF16)
        picked = _dot(selb, expand)
        return jnp.where(key_dist(start) >= 0, jnp.where(picked > 0.5, 0.0, NEG_BIG), NEG_BIG)

    def win_bias(start):
        dist = key_dist(start)
        return jnp.where(jnp.where(dist >= 0, dist, window) < window, 0.0, NEG_BIG)

    c_hi = (t0 + qt + ck - 1) // ck
    attend(ks_ref, vsl_ref, vsh_ref, 0, c_hi, sel_bias, 1)
    attend(kw_ref, vwl_ref, vwh_ref, jnp.maximum(t0 - window + 1, 0) // ck, c_hi, win_bias, 2)

    gt = gt_ref[...]
    for p in range(R // 2):
        acc = jnp.zeros((qt, LANES), F32)
        for par in range(2):
            rsl = slice(par * half_rows + p * qt, par * half_rows + (p + 1) * qt)
            for br in range(3):
                col = (2 * p + par) * 3 + br
                acc = acc + gt[:, col:col + 1] * res_ref[br, rsl, :]
        o_ref[0, :, p * LANES:(p + 1) * LANES] = acc.astype(o_ref.dtype)


def nsa_attention(q, gates, kcd, vcl, vch, kvx, *, batch, seq, qt=256, ck=512):
    G = NSA_G
    R = NSA_H // G
    gq = R * NSA_DH
    n_sel = seq // NSA_SEL_LEN
    ncmp = kcd.shape[1]
    nq = seq // qt
    ov = np.zeros((ncmp, LANES), np.float32)
    for n in range(ncmp):
        for j in range(n_sel):
            if n * NSA_CMP_STEP < (j + 1) * NSA_SEL_LEN and n * NSA_CMP_STEP + NSA_CMP_LEN > j * NSA_SEL_LEN:
                ov[n, j] = 1.0
    n_valid = (seq - NSA_CMP_LEN) // NSA_CMP_STEP + 1
    ov[n_valid:] = 0.0
    kern = functools.partial(_nsa_attn_kernel, qt=qt, ck=ck, n_sel=n_sel, top_n=min(NSA_TOPN, n_sel),
                             sel_shift=int(math.log2(NSA_SEL_LEN)), cmp_len=NSA_CMP_LEN, cmp_step=NSA_CMP_STEP, window=NSA_WIN)
    cmp_spec = pl.BlockSpec((None, ncmp, LANES), lambda b, g, i: (b * G + g, 0, 0))
    seq_specs = [pl.BlockSpec((None, None, None, seq, LANES), functools.partial(lambda b, g, i, k: (k, b, g, 0, 0), k=k))
                 for k in range(6)]
    rows = R * qt
    return pl.pallas_call(
        kern,
        grid=(batch, G, nq),
        in_specs=[pl.BlockSpec((1, qt, gq), lambda b, g, i: (b, i, g)),
                  pl.BlockSpec((qt, LANES), lambda b, g, i: (b * nq + i, g)),
                  cmp_spec, cmp_spec, cmp_spec] + seq_specs + [
                  pl.BlockSpec((ncmp, LANES), lambda b, g, i: (0, 0))],
        out_specs=pl.BlockSpec((1, qt, gq), lambda b, g, i: (b, i, g)),
        out_shape=jax.ShapeDtypeStruct((batch, seq, NSA_H * NSA_DH), BF16),
        scratch_shapes=[pltpu.VMEM((rows, LANES), BF16),
                        pltpu.VMEM((rows, 1), F32), pltpu.VMEM((rows, 1), F32),
                        pltpu.VMEM((rows // 2, LANES), F32), pltpu.VMEM((rows // 2, LANES), F32),
                        pltpu.VMEM((3, rows, LANES), F32),
                        pltpu.VMEM((R, qt, ck), F32)],
        compiler_params=_params(("parallel", "parallel", "arbitrary")),
        name="nsa_attention",
    )(q, gates, kcd, vcl, vch, kvx, kvx, kvx, kvx, kvx, kvx, jnp.asarray(ov, BF16))


def _tile_rows(ref, idx, ns):
    return ref.at[pl.ds(pl.multiple_of(idx * ns, ns), ns)]


def _router_kernel(x_ref, sc_ref, sh_ref, w_ref, b_ref, h3_ref, meta_ref, gate_ref, cnt_ref, seen_ref, *, top_k):
    @pl.when(pl.program_id(0) == 0)
    def _():
        seen_ref[...] = jnp.zeros_like(seen_ref)

    h = x_ref[...] * (1.0 + sc_ref[0]) + sh_ref[0]
    tm = h.shape[0]
    ns = h.shape[1] // LANES
    for s in range(ns):
        h3_ref[pl.ds(s, tm, stride=ns), :] = h[:, s * LANES:(s + 1) * LANES]
    logits = _dot(h.astype(BF16), w_ref[...]) + b_ref[...]
    lane = lax.broadcasted_iota(I32, logits.shape, 1)
    lane_f = lane.astype(F32)
    val_out = jnp.full(logits.shape, NEG_BIG, F32)
    chosen = jnp.zeros(logits.shape, F32)
    picks = []
    for k in range(top_k):
        mx = jnp.max(logits, axis=-1, keepdims=True)
        idx = jnp.min(jnp.where(logits == mx, lane_f, float(LANES)), axis=-1, keepdims=True)
        hit = lane_f == idx
        picks.append((idx, hit))
        chosen = jnp.where(hit, 1.0, chosen)
        val_out = jnp.where(lane == k, mx, val_out)
        logits = jnp.where(hit, -jnp.inf, logits)
    e = jnp.exp(val_out - jnp.max(val_out, axis=-1, keepdims=True))
    e = jnp.where(lane < top_k, e, 0.0)
    gate_ref[...] = e / jnp.sum(e, axis=-1, keepdims=True)
    lower = jnp.where(lax.broadcasted_iota(I32, (tm, tm), 0) > lax.broadcasted_iota(I32, (tm, tm), 1), 1.0, 0.0)
    before = _dot(lower.astype(BF16), chosen.astype(BF16)) + seen_ref[...]
    meta = jnp.zeros(logits.shape, I32)
    for k, (idx, hit) in enumerate(picks):
        rank = jnp.sum(jnp.where(hit, before, 0.0), axis=-1, keepdims=True)
        meta = jnp.where(lane == k, idx.astype(I32), meta)
        meta = jnp.where(lane == top_k + k, rank.astype(I32), meta)
    meta_ref[...] = meta
    seen_ref[...] = seen_ref[...] + jnp.sum(chosen, axis=0, keepdims=True)
    cnt_ref[...] = seen_ref[...].astype(I32)


def moe_router(x, sc, sh, router_w, router_b, *, seq, tm=512):
    T, D = x.shape
    E = router_w.shape[1]
    nb = seq // tm
    ns = D // LANES
    assert ns == 8
    wp = jnp.zeros((D, LANES), BF16).at[:, :E].set(router_w.astype(BF16))
    bp = jnp.full((1, LANES), NEG_BIG, F32).at[0, :E].set(router_b)
    per_b = pl.BlockSpec((1, 1, D), lambda i: (i // nb, 0, 0))
    return pl.pallas_call(
        functools.partial(_router_kernel, top_k=MOE_K),
        grid=(T // tm,),
        in_specs=[pl.BlockSpec((tm, D), lambda i: (i, 0)), per_b, per_b,
                  pl.BlockSpec((D, LANES), lambda i: (0, 0)),
                  pl.BlockSpec((1, LANES), lambda i: (0, 0))],
        out_specs=[pl.BlockSpec((tm * ns, LANES), lambda i: (i, 0)),
                   pl.BlockSpec((tm, LANES), lambda i: (i, 0)),
                   pl.BlockSpec((tm, LANES), lambda i: (i, 0)),
                   pl.BlockSpec((1, LANES), lambda i: (0, 0))],
        out_shape=[jax.ShapeDtypeStruct((T * ns, LANES), F32),
                   jax.ShapeDtypeStruct((T, LANES), I32),
                   jax.ShapeDtypeStruct((T, LANES), F32),
                   jax.ShapeDtypeStruct((1, LANES), I32)],
        scratch_shapes=[pltpu.VMEM((1, LANES), F32)],
        compiler_params=_params(("arbitrary",)),
        name="moe_router",
    )(x, sc, sh, wp, bp)


def _dispatch_kernel(fill_ref, slots_ref, slots1_ref, slots2_ref, h3_ref, xs3_ref, zero_ref, hbuf_ref, fill_sem,
                     load_sems, row_sems, *, tm, top_k, n_exp, ns):
    i = pl.program_id(0)
    n = pl.num_programs(0)
    nbuf = hbuf_ref.shape[0]

    def load(step):
        src = h3_ref.at[pl.ds(pl.multiple_of(step * (tm * ns), tm * ns), tm * ns)]
        return pltpu.make_async_copy(src, hbuf_ref.at[step % nbuf], load_sems.at[step % nbuf])

    def row_copy(step, slots, r, k):
        return pltpu.make_async_copy(_tile_rows(hbuf_ref.at[step % nbuf], r, ns), _tile_rows(xs3_ref, slots[r * top_k + k], ns),
                                     row_sems.at[step % nbuf])

    def issue_rows(step, slots):
        def body(r, carry):
            for k in range(top_k):
                row_copy(step, slots, r, k).start()
            return carry
        lax.fori_loop(0, tm, body, 0, unroll=8)

    def drain_rows(step, slots):
        def body(r, carry):
            for k in range(top_k):
                row_copy(step, slots, r, k).wait()
            return carry
        lax.fori_loop(0, tm, body, 0, unroll=8)

    @pl.when(i == 0)
    def _():
        load(0).start()
        zero_ref[...] = jnp.zeros_like(zero_ref)
        for e in range(n_exp):
            dst = xs3_ref.at[pl.ds(pl.multiple_of(fill_ref[e] * ns, ns), zero_ref.shape[0])]
            fill = pltpu.make_async_copy(zero_ref, dst, fill_sem)
            fill.start()
            fill.wait()

    @pl.when(i >= 2)
    def _():
        drain_rows(i - 2, slots2_ref)

    @pl.when(i + 1 < n)
    def _():
        load(i + 1).start()

    load(i).wait()
    issue_rows(i, slots_ref)

    @pl.when(i == n - 1)
    def _():
        @pl.when(i >= 1)
        def _():
            drain_rows(i - 1, slots1_ref)
        drain_rows(i, slots_ref)


def moe_dispatch(h3, slots, fill_start, n_rows, *, tm=256):
    ns = 8
    T = h3.shape[0] // ns
    K = MOE_K
    slot_spec = lambda back: pl.BlockSpec((tm * K,), lambda i, fill: (jnp.maximum(i - back, 0),), memory_space=pltpu.SMEM)
    grid_spec = pltpu.PrefetchScalarGridSpec(
        num_scalar_prefetch=1,
        grid=(T // tm,),
        in_specs=[slot_spec(0), slot_spec(1), slot_spec(2), pl.BlockSpec(memory_space=pl.ANY)],
        out_specs=pl.BlockSpec(memory_space=pl.ANY),
        scratch_shapes=[pltpu.VMEM((MOE_BLK * ns, LANES), F32), pltpu.VMEM((3, tm * ns, LANES), F32),
                        pltpu.SemaphoreType.DMA(()), pltpu.SemaphoreType.DMA((3,)), pltpu.SemaphoreType.DMA((3,))],
    )
    return pl.pallas_call(
        functools.partial(_dispatch_kernel, tm=tm, top_k=K, n_exp=MOE_E, ns=ns),
        grid_spec=grid_spec,
        out_shape=jax.ShapeDtypeStruct((n_rows * ns, LANES), F32),
        compiler_params=_params(("arbitrary",)),
        name="moe_dispatch",
    )(fill_start, slots, slots, slots, h3)


def _expert_kernel(be_ref, nu_ref, x3_ref, w1_ref, b1_ref, w2_ref, b2_ref, o3_ref, w1b_ref, w2b_ref, *, dff):
    i = pl.program_id(0)
    ns = w1b_ref.shape[0] // LANES
    blk = x3_ref.shape[0] // ns

    @pl.when(i < nu_ref[0])
    def _():
        @pl.when(jnp.logical_or(i == 0, be_ref[i] != be_ref[jnp.maximum(i - 1, 0)]))
        def _():
            w1b_ref[...] = w1_ref[0].astype(BF16)
            w2b_ref[...] = w2_ref[0].astype(BF16)

        xb = jnp.concatenate([x3_ref[pl.ds(s, blk, stride=ns), :].astype(BF16) for s in range(ns)], axis=-1)
        gu = _dot(xb, w1b_ref[...]) + b1_ref[0]
        gg = jnp.minimum(gu[:, :dff], GLU_LIMIT)
        up = jnp.clip(gu[:, dff:], -GLU_LIMIT, GLU_LIMIT)
        act = (up + 1.0) * (gg * _sigmoid(GLU_ALPHA * gg))
        y = _dot(act.astype(BF16), w2b_ref[...]) + b2_ref[0]
        for s in range(ns):
            o3_ref[pl.ds(s, blk, stride=ns), :] = y[:, s * LANES:(s + 1) * LANES]


def moe_experts(xs3, block_expert, n_used, w1, b1, w2, b2, layer):
    L, E, D, two_f = w1.shape
    ns = D // LANES
    nblk = block_expert.shape[0]
    row_map = lambda i, be, nu: (jnp.minimum(i, nu[0] - 1), 0)
    w_map = lambda i, be, nu: (layer, be[i], 0, 0)
    grid_spec = pltpu.PrefetchScalarGridSpec(
        num_scalar_prefetch=2,
        grid=(nblk,),
        in_specs=[pl.BlockSpec((MOE_BLK * ns, LANES), row_map),
                  pl.BlockSpec((None, 1, D, two_f), w_map),
                  pl.BlockSpec((None, 1, 1, two_f), w_map),
                  pl.BlockSpec((None, 1, two_f // 2, D), w_map),
                  pl.BlockSpec((None, 1, 1, D), w_map)],
        out_specs=pl.BlockSpec((MOE_BLK * ns, LANES), row_map),
        scratch_shapes=[pltpu.VMEM((D, two_f), BF16), pltpu.VMEM((two_f // 2, D), BF16)],
    )
    return pl.pallas_call(
        functools.partial(_expert_kernel, dff=two_f // 2),
        grid_spec=grid_spec,
        out_shape=jax.ShapeDtypeStruct((nblk * MOE_BLK * ns, LANES), F32),
        compiler_params=_params(("arbitrary",)),
        name="moe_experts",
    )(block_expert, n_used, xs3, w1, b1.reshape(L, E, 1, two_f), w2, b2.reshape(L, E, 1, D))


def _combine_kernel(slots_ref, nslots_ref, out3_ref, gt_ref, res_ref, gate_ref, g_ref, bb_ref, o_ref, y4_ref, sems,
                    *, tm, top_k, alpha, ns):
    i = pl.program_id(0)
    buf = i % 2

    def row_copy(slots, b, r, k):
        return pltpu.make_async_copy(_tile_rows(out3_ref, slots[r * top_k + k], ns), _tile_rows(y4_ref.at[b, k], r, ns),
                                     sems.at[b])

    def issue(slots, b):
        def body(r, carry):
            for k in range(top_k):
                row_copy(slots, b, r, k).start()
            return carry
        lax.fori_loop(0, tm, body, 0, unroll=8)

    def drain(slots, b):
        def body(r, carry):
            for k in range(top_k):
                row_copy(slots, b, r, k).wait()
            return carry
        lax.fori_loop(0, tm, body, 0, unroll=8)

    @pl.when(i == 0)
    def _():
        issue(slots_ref, 0)

    @pl.when(i + 1 < pl.num_programs(0))
    def _():
        issue(nslots_ref, 1 - buf)

    drain(slots_ref, buf)
    gt = gt_ref[...]
    cols = []
    for s in range(ns):
        acc = gt[:, 0:1] * y4_ref[buf, 0, pl.ds(s, tm, stride=ns), :]
        for k in range(1, top_k):
            acc = acc + gt[:, k:k + 1] * y4_ref[buf, k, pl.ds(s, tm, stride=ns), :]
        cols.append(acc)
    y = jnp.concatenate(cols, axis=-1)
    z = alpha * res_ref[...] + (1.0 + gate_ref[0]) * y
    o_ref[...] = _ln_rows(z, g_ref[...], bb_ref[...])


def moe_combine(out3, slots, gates, res, gate, g, b, *, seq, alpha, tm=256):
    T, D = res.shape
    K = MOE_K
    ns = D // LANES
    nb = seq // tm
    nsteps = T // tm
    one = pl.BlockSpec((1, D), lambda i: (0, 0))
    return pl.pallas_call(
        functools.partial(_combine_kernel, tm=tm, top_k=K, alpha=alpha, ns=ns),
        grid=(nsteps,),
        in_specs=[pl.BlockSpec((tm * K,), lambda i: (i,), memory_space=pltpu.SMEM),
                  pl.BlockSpec((tm * K,), lambda i: (jnp.minimum(i + 1, nsteps - 1),), memory_space=pltpu.SMEM),
                  pl.BlockSpec(memory_space=pl.ANY),
                  pl.BlockSpec((tm, LANES), lambda i: (i, 0)),
                  pl.BlockSpec((tm, D), lambda i: (i, 0)),
                  pl.BlockSpec((1, 1, D), lambda i: (i // nb, 0, 0)), one, one],
        out_specs=pl.BlockSpec((tm, D), lambda i: (i, 0)),
        out_shape=jax.ShapeDtypeStruct((T, D), F32),
        scratch_shapes=[pltpu.VMEM((2, K, tm * ns, LANES), F32), pltpu.SemaphoreType.DMA((2,))],
        compiler_params=_params(("arbitrary",)),
        name="moe_combine",
    )(slots, slots, out3, gates, res, gate, g.reshape(1, D), b.reshape(1, D))


def moe_layer(x, sc, sh, gate, ln_g, ln_b, router_w, router_b, w1, b1, w2, b2, layer, *, seq, alpha):
    T, D = x.shape
    E, K = MOE_E, MOE_K
    h3, meta, gates, cnt = moe_router(x, sc, sh, router_w, router_b, seq=seq)
    counts = cnt[0, :E]
    blocks_per = (counts + MOE_BLK - 1) // MOE_BLK
    block_end = jnp.cumsum(blocks_per)
    pad_start = (block_end - blocks_per) * MOE_BLK
    n_blocks = -(-(T * K) // MOE_BLK) + E
    blk = jnp.arange(n_blocks, dtype=I32)
    block_expert = jnp.minimum(jnp.sum(block_end[None, :] <= blk[:, None], axis=1), E - 1).astype(I32)
    onehot = meta[:, :K, None] == jnp.arange(E, dtype=I32)[None, None, :]
    slots = (meta[:, K:2 * K] + jnp.sum(jnp.where(onehot, pad_start[None, None, :], 0), axis=-1)).reshape(-1).astype(I32)
    fill_start = (pad_start + counts).astype(I32)
    xs3 = moe_dispatch(h3, slots, fill_start, (n_blocks + 1) * MOE_BLK)
    out3 = moe_experts(xs3, block_expert, block_end[-1:].astype(I32), w1, b1, w2, b2, layer)
    return moe_combine(out3, slots, gates, x, gate, ln_g, ln_b, seq=seq, alpha=alpha)


def _rope_tables(positions, d):
    inv = ROPE_BASE ** (-jnp.arange(0, d, 2, dtype=F32) / d)
    ang = positions.astype(F32).reshape(-1)[:, None] * inv
    return jnp.cos(ang), jnp.sin(ang)


def kernel(x, c, positions, ada_w, ada_b, ln1_g, ln1_b, ln2_g, ln2_b, router_w, router_b, moe_w1, moe_b1, moe_w2, moe_b2, ret_w_in, ret_gn_g, ret_gn_b, ret_w_out, conv_w_pw1, conv_b_pw1, conv_w_dw, conv_b_dw, conv_ln_g, conv_ln_b, conv_w_pw2, conv_b_pw2, nsa_w_in, nsa_gate_b, nsa_cmp_pos_k, nsa_cmp_pos_v, nsa_cmp_k_w1, nsa_cmp_k_w2, nsa_cmp_v_w1, nsa_cmp_v_w2, nsa_w_out, pool_w, pool_b, pool_scale):
    B, S, D = x.shape
    T = B * S
    depth = ada_w.shape[0]
    alpha = (2.0 * depth) ** 0.25
    mod = ada_modulation(c, ada_w, ada_b)
    xf = x.reshape(T, D)
    for i in range(depth):
        kind, j = i % 4, i // 4
        sh1, sc1, g1, sh2, sc2, g2 = (mod[i, :, k * D:(k + 1) * D].reshape(B, 1, D) for k in range(6))
        res1 = (g1, ln1_g[i], ln1_b[i])
        if kind == 0:
            dk = ret_w_in.shape[2] // 6 // RET_H
            cos, sin = _rope_tables(positions, dk)
            p = ret_projection(xf, sc1, sh1, ret_w_in[j].astype(BF16), cos, sin, seq=S)
            o = retention_core(p, ret_gn_g[j], ret_gn_b[j], batch=B, seq=S)
            xf = fused_matmul(o, ret_w_out[j].astype(BF16), seq=S, resln=(xf,) + res1, alpha=alpha, name="ret_out")
        elif kind == 1:
            u = conv_glu(xf, sc1, sh1, conv_w_pw1[j].astype(BF16), conv_b_pw1[j], seq=S)
            xf = conv_tail(u, conv_w_dw[j], conv_b_dw[j], conv_ln_g[j], conv_ln_b[j], conv_w_pw2[j].astype(BF16),
                           conv_b_pw2[j], xf, *res1, seq=S, alpha=alpha)
        elif kind == 2:
            G, dh, H = NSA_G, NSA_DH, NSA_H
            cos, sin = _rope_tables(positions, dh)
            cos_p = jnp.tile(cos, (1, LANES // (dh // 2)))
            sin_p = jnp.tile(jnp.concatenate([-sin, sin], axis=1), (1, LANES // dh))
            n_main = H * dh + 6 * G * dh
            ng = 3 * H // G
            w_g = jnp.zeros((D, G, LANES), F32).at[:, :, :ng].set(nsa_w_in[j][:, n_main:].reshape(D, G, ng))
            w_p = jnp.concatenate([nsa_w_in[j][:, :n_main], w_g.reshape(D, G * LANES)], axis=1).astype(BF16)
            gb_p = jnp.zeros((G, LANES), F32).at[:, :ng].set(nsa_gate_b[j].reshape(G, ng)).reshape(1, G * LANES)
            q, cv, kvx, gates = nsa_projection(xf, sc1, sh1, w_p, cos_p, sin_p, gb_p, seq=S)
            cv = cv.reshape(B, S, 2, G, dh).transpose(2, 0, 3, 1, 4).reshape(2, B * G, S // NSA_CMP_STEP, NSA_CMP_STEP * dh)
            k_cmp, v_cmp = nsa_compress(cv[0], cv[1], nsa_cmp_pos_k[j], nsa_cmp_pos_v[j], nsa_cmp_k_w1[j],
                                        nsa_cmp_k_w2[j], nsa_cmp_v_w1[j], nsa_cmp_v_w2[j])
            dup = lambda a: jnp.concatenate([a, a], axis=-1)
            lo = lambda a: jnp.concatenate([a, jnp.zeros_like(a)], axis=-1)
            hi = lambda a: jnp.concatenate([jnp.zeros_like(a), a], axis=-1)
            o = nsa_attention(q.reshape(B, S, H * dh), gates, dup(k_cmp), lo(v_cmp), hi(v_cmp), kvx, batch=B, seq=S)
            xf = fused_matmul(o.reshape(T, H * dh), nsa_w_out[j].astype(BF16), seq=S, resln=(xf,) + res1, alpha=alpha,
                              name="nsa_out")
        else:
            xf = pool_layer(xf, sc1, sh1, pool_w[j], pool_b[j], pool_scale[j], *res1, seq=S, alpha=alpha)
        xf = moe_layer(xf, sc2, sh2, g2, ln2_g[i], ln2_b[i], router_w[i], router_b[i], moe_w1, moe_b1, moe_w2, moe_b2, i,
                       seq=S, alpha=alpha)
    return xf.reshape(B, S, D)
```

```python
import functools
import math

import jax
import jax.numpy as jnp
import numpy as np
from jax import lax
from jax.experimental import pallas as pl
from jax.experimental.pallas import tpu as pltpu

F32 = jnp.float32
BF16 = jnp.bfloat16
I32 = jnp.int32

ROPE_BASE = 10000.0
EPS = 1e-5
NEG_BIG = -1e30

RET_H = 4
RET_C = 128
CONV_W = 31
NSA_H = 16
NSA_G = 2
NSA_DH = 64
NSA_CMP_LEN = 32
NSA_CMP_STEP = 16
NSA_SEL_LEN = 64
NSA_TOPN = 16
NSA_WIN = 512
NSA_FORCED = 1e6
POOL_WINS = (2, 4, 8, 16)
MOE_E = 32
MOE_K = 4
MOE_BLK = 512
MOE_SUB = 256
GLU_LIMIT = 7.0
GLU_ALPHA = 1.702

V7X_VMEM_LIMIT = 56 * 1024 * 1024
LANES = 128


def _params(sem):
    return pltpu.CompilerParams(dimension_semantics=sem, vmem_limit_bytes=V7X_VMEM_LIMIT)


def _ln_rows(v, g, b):
    mu = jnp.mean(v, axis=-1, keepdims=True)
    d = v - mu
    var = jnp.mean(d * d, axis=-1, keepdims=True)
    return d * lax.rsqrt(var + EPS) * g + b


def _sigmoid(v):
    return 1.0 / (1.0 + jnp.exp(-v))


def _dot(a, b):
    return jnp.dot(a, b, preferred_element_type=F32)


def _dot_nt(a, b):
    return lax.dot_general(a, b, (((1,), (1,)), ((), ())), preferred_element_type=F32)


def _dot_tn(a, b):
    return lax.dot_general(a, b, (((0,), (0,)), ((), ())), preferred_element_type=F32)


def _ada_kernel(c_ref, w_ref, b_ref, o_ref):
    c = c_ref[...]
    ca = c * _sigmoid(c)
    o_ref[0] = _dot(ca.astype(BF16), w_ref[0].astype(BF16)) + b_ref[0]


def ada_modulation(c, ada_w, ada_b):
    L, D, N = ada_w.shape
    B = c.shape[0]
    tn = 2048 if N % 2048 == 0 else N
    return pl.pallas_call(
        _ada_kernel,
        grid=(L, N // tn),
        in_specs=[pl.BlockSpec((B, D), lambda l, j: (0, 0)),
                  pl.BlockSpec((1, D, tn), lambda l, j: (l, 0, j)),
                  pl.BlockSpec((1, 1, tn), lambda l, j: (l, 0, j))],
        out_specs=pl.BlockSpec((1, B, tn), lambda l, j: (l, 0, j)),
        out_shape=jax.ShapeDtypeStruct((L, B, N), F32),
        compiler_params=_params(("parallel", "parallel")),
        name="ada_modulation",
    )(c, ada_w, ada_b.reshape(L, 1, N))


def _mm_kernel(*refs, has_mod, has_bias, epi, alpha):
    it = iter(refs)
    x_ref = next(it)
    if has_mod:
        sc_ref, sh_ref = next(it), next(it)
    w_ref = next(it)
    if has_bias:
        b_ref = next(it)
    if epi == "resln":
        res_ref, gate_ref, g_ref, bb_ref = next(it), next(it), next(it), next(it)
    o_ref = next(it)
    x = x_ref[...]
    if has_mod:
        x = x.astype(F32) * (1.0 + sc_ref[0]) + sh_ref[0]
    acc = _dot(x.astype(BF16), w_ref[...])
    if has_bias:
        acc = acc + b_ref[...]
    if epi == "resln":
        v = alpha * res_ref[...] + (1.0 + gate_ref[0]) * acc
        acc = _ln_rows(v, g_ref[...], bb_ref[...])
    o_ref[...] = acc.astype(o_ref.dtype)


def fused_matmul(x, w, *, seq, mod=None, bias=None, resln=None, alpha=1.0, out_dtype=F32, tm=512, tn=None, name="mm"):
    T, K = x.shape
    N = w.shape[1]
    tn = N if tn is None else tn
    assert T % tm == 0 and seq % tm == 0 and N % tn == 0
    nb = seq // tm
    args, specs = [x], [pl.BlockSpec((tm, K), lambda i, j: (i, 0))]
    if mod is not None:
        for m in mod:
            args.append(m)
            specs.append(pl.BlockSpec((1, 1, K), lambda i, j: (i // nb, 0, 0)))
    args.append(w)
    specs.append(pl.BlockSpec((K, tn), lambda i, j: (0, j)))
    if bias is not None:
        args.append(bias.reshape(1, N))
        specs.append(pl.BlockSpec((1, tn), lambda i, j: (0, j)))
    epi = "none"
    if resln is not None:
        assert tn == N
        res, gate, g, b = resln
        epi = "resln"
        args += [res, gate, g.reshape(1, N), b.reshape(1, N)]
        specs += [pl.BlockSpec((tm, N), lambda i, j: (i, 0)),
                  pl.BlockSpec((1, 1, N), lambda i, j: (i // nb, 0, 0)),
                  pl.BlockSpec((1, N), lambda i, j: (0, 0)),
                  pl.BlockSpec((1, N), lambda i, j: (0, 0))]
    kern = functools.partial(_mm_kernel, has_mod=mod is not None, has_bias=bias is not None, epi=epi, alpha=alpha)
    return pl.pallas_call(
        kern,
        grid=(T // tm, N // tn),
        in_specs=specs,
        out_specs=pl.BlockSpec((tm, tn), lambda i, j: (i, j)),
        out_shape=jax.ShapeDtypeStruct((T, N), out_dtype),
        compiler_params=_params(("parallel", "parallel")),
        name=name,
    )(*args)


def _ret_proj_kernel(x_ref, sc_ref, sh_ref, w_ref, cos_ref, sin_ref, o_ref, *, dk, hdk, tn, kscale):
    h = (x_ref[...] * (1.0 + sc_ref[0]) + sh_ref[0]).astype(BF16)
    cos, sin = cos_ref[...], sin_ref[...]
    half = dk // 2
    for j in range(o_ref.shape[1] // tn):
        c0 = j * tn
        acc = _dot(h, w_ref[:, c0:c0 + tn])
        if c0 < 2 * hdk:
            scale = 1.0 if c0 < hdk else kscale
            for hh in range(tn // dk):
                x1 = acc[:, hh * dk:hh * dk + half]
                x2 = acc[:, hh * dk + half:(hh + 1) * dk]
                o_ref[:, c0 + hh * dk:c0 + hh * dk + half] = ((x1 * cos - x2 * sin) * scale).astype(o_ref.dtype)
                o_ref[:, c0 + hh * dk + half:c0 + (hh + 1) * dk] = ((x1 * sin + x2 * cos) * scale).astype(o_ref.dtype)
        elif c0 < 4 * hdk:
            o_ref[:, c0:c0 + tn] = acc.astype(o_ref.dtype)
        else:
            o_ref[:, c0:c0 + tn] = (acc * _sigmoid(acc)).astype(o_ref.dtype)


def ret_projection(x, sc, sh, w_in, cos, sin, *, seq, tm=512, tn=1024):
    T, K = x.shape
    N = w_in.shape[1]
    hdk = N // 6
    dk = hdk // RET_H
    assert hdk % tn == 0 and tn % dk == 0
    nb = seq // tm
    kern = functools.partial(_ret_proj_kernel, dk=dk, hdk=hdk, tn=tn, kscale=dk ** -0.5)
    return pl.pallas_call(
        kern,
        grid=(T // tm,),
        in_specs=[pl.BlockSpec((tm, K), lambda i: (i, 0)),
                  pl.BlockSpec((1, 1, K), lambda i: (i // nb, 0, 0)),
                  pl.BlockSpec((1, 1, K), lambda i: (i // nb, 0, 0)),
                  pl.BlockSpec((K, N), lambda i: (0, 0), pipeline_mode=pl.Buffered(1)),
                  pl.BlockSpec((tm, dk // 2), lambda i: (i, 0)),
                  pl.BlockSpec((tm, dk // 2), lambda i: (i, 0))],
        out_specs=pl.BlockSpec((tm, N), lambda i: (i, 0)),
        out_shape=jax.ShapeDtypeStruct((T, N), BF16),
        compiler_params=_params(("parallel",)),
        name="ret_projection",
    )(x, sc, sh, w_in, cos, sin)


def _retention_kernel(q_ref, k_ref, v_ref, sg_ref, din_ref, dq_ref, dkk_ref, dch_ref, gg_ref, gb_ref,
                      o_ref, state_ref, *, H, C, dk, dv):
    @pl.when(pl.program_id(1) == 0)
    def _():
        state_ref[...] = jnp.zeros_like(state_ref)

    n_chunks = q_ref.shape[0] // C
    for c in range(n_chunks):
        rows = slice(c * C, (c + 1) * C)
        for hh in range(H):
            qc = q_ref[rows, hh * dk:(hh + 1) * dk]
            kc = k_ref[rows, hh * dk:(hh + 1) * dk]
            vc = v_ref[rows, hh * dv:(hh + 1) * dv]
            st = state_ref[hh]
            a = _dot_nt(qc, kc) * din_ref[hh]
            o = _dot(a.astype(BF16), vc)
            qd = (qc.astype(F32) * dq_ref[hh]).astype(BF16)
            o = o + _dot(qd, st.astype(BF16))
            kd = (kc.astype(F32) * dkk_ref[hh]).astype(BF16)
            state_ref[hh] = st * dch_ref[hh, :, 0:1] + _dot_tn(kd, vc)
            mu = jnp.mean(o, axis=-1, keepdims=True)
            d = o - mu
            var = jnp.mean(d * d, axis=-1, keepdims=True)
            on = d * lax.rsqrt(var + EPS) * gg_ref[:, hh * dv:(hh + 1) * dv] + gb_ref[:, hh * dv:(hh + 1) * dv]
            sg = sg_ref[rows, hh * dv:(hh + 1) * dv].astype(F32)
            o_ref[rows, hh * dv:(hh + 1) * dv] = (sg * on).astype(o_ref.dtype)


def retention_core(p, gn_g, gn_b, *, batch, seq, tb=512):
    T, N = p.shape
    H, C = RET_H, RET_C
    hdk = N // 6
    dk, dv = hdk // H, 2 * hdk // H
    hdv = H * dv
    nb = seq // tb
    log_gamma = jnp.log1p(-(2.0 ** (-5.0 - jnp.arange(H, dtype=F32))))
    i = jnp.arange(C, dtype=F32)
    rel = i[:, None] - i[None, :]
    d_intra = jnp.where(rel >= 0, jnp.exp(log_gamma[:, None, None] * jnp.maximum(rel, 0.0)), 0.0).astype(F32)
    d_q = jnp.exp(log_gamma[:, None] * (i + 1.0)).astype(F32)[..., None]
    d_k = jnp.exp(log_gamma[:, None] * (C - 1.0 - i)).astype(F32)[..., None]
    d_ch = jnp.broadcast_to(jnp.exp(log_gamma * C).astype(F32)[:, None, None], (H, 1, LANES))
    kern = functools.partial(_retention_kernel, H=H, C=C, dk=dk, dv=dv)
    return pl.pallas_call(
        kern,
        grid=(batch, nb),
        in_specs=[pl.BlockSpec((tb, hdk), lambda b, s: (b * nb + s, 0)),
                  pl.BlockSpec((tb, hdk), lambda b, s: (b * nb + s, 1)),
                  pl.BlockSpec((tb, hdv), lambda b, s: (b * nb + s, 1)),
                  pl.BlockSpec((tb, hdv), lambda b, s: (b * nb + s, 2)),
                  pl.BlockSpec((H, C, C), lambda b, s: (0, 0, 0)),
                  pl.BlockSpec((H, C, 1), lambda b, s: (0, 0, 0)),
                  pl.BlockSpec((H, C, 1), lambda b, s: (0, 0, 0)),
                  pl.BlockSpec((H, 1, LANES), lambda b, s: (0, 0, 0)),
                  pl.BlockSpec((1, hdv), lambda b, s: (0, 0)),
                  pl.BlockSpec((1, hdv), lambda b, s: (0, 0))],
        out_specs=pl.BlockSpec((tb, hdv), lambda b, s: (b * nb + s, 0)),
        out_shape=jax.ShapeDtypeStruct((T, hdv), BF16),
        scratch_shapes=[pltpu.VMEM((H, dk, dv), F32)],
        compiler_params=_params(("parallel", "arbitrary")),
        name="retention_core",
    )(p, p, p, p, d_intra, d_q, d_k, d_ch, gn_g.reshape(1, hdv), gn_b.reshape(1, hdv))


def _glu_kernel(x_ref, sc_ref, sh_ref, wa_ref, wg_ref, ba_ref, bg_ref, o_ref):
    h = (x_ref[...] * (1.0 + sc_ref[0]) + sh_ref[0]).astype(BF16)
    a = _dot(h, wa_ref[...]) + ba_ref[...]
    g = _dot(h, wg_ref[...]) + bg_ref[...]
    o_ref[...] = a * _sigmoid(g)


def conv_glu(x, sc, sh, w_pw1, b_pw1, *, seq, tm=512, tn=512):
    T, K = x.shape
    N = w_pw1.shape[1] // 2
    nb = seq // tm
    ng = N // tn
    b2 = b_pw1.reshape(1, 2 * N)
    return pl.pallas_call(
        _glu_kernel,
        grid=(T // tm, ng),
        in_specs=[pl.BlockSpec((tm, K), lambda i, j: (i, 0)),
                  pl.BlockSpec((1, 1, K), lambda i, j: (i // nb, 0, 0)),
                  pl.BlockSpec((1, 1, K), lambda i, j: (i // nb, 0, 0)),
                  pl.BlockSpec((K, tn), lambda i, j: (0, j)),
                  pl.BlockSpec((K, tn), lambda i, j: (0, j + ng)),
                  pl.BlockSpec((1, tn), lambda i, j: (0, j)),
                  pl.BlockSpec((1, tn), lambda i, j: (0, j + ng))],
        out_specs=pl.BlockSpec((tm, tn), lambda i, j: (i, j)),
        out_shape=jax.ShapeDtypeStruct((T, N), F32),
        compiler_params=_params(("parallel", "parallel")),
        name="conv_glu",
    )(x, sc, sh, w_pw1, w_pw1, b2, b2)


def _conv_tail_kernel(u_ref, up_ref, wdw_ref, bdw_ref, lg_ref, lb_ref, w2_ref, b2_ref,
                      res_ref, gate_ref, g_ref, bb_ref, o_ref, win_ref, shift_ref, *, nb, halo, width, alpha):
    tm = u_ref.shape[0]
    first = (pl.program_id(0) % nb) == 0
    prev = up_ref[...]
    win_ref[0:halo, :] = jnp.where(first, jnp.zeros_like(prev), prev)
    win_ref[halo:halo + tm, :] = u_ref[...]
    for b in range(1, 8):
        n = (tm + halo - b) // 8 * 8
        shift_ref[b - 1, 0:n, :] = win_ref[b:b + n, :]
    acc = jnp.zeros(u_ref.shape, F32)
    off = halo - (width - 1)
    for k in range(width):
        b, a = (off + k) % 8, (off + k) // 8 * 8
        rows = win_ref[a:a + tm, :] if b == 0 else shift_ref[b - 1, a:a + tm, :]
        acc = acc + wdw_ref[k:k + 1, :] * rows
    acc = acc + bdw_ref[...]
    v = _ln_rows(acc, lg_ref[...], lb_ref[...])
    v = v * _sigmoid(v)
    y = _dot(v.astype(BF16), w2_ref[...]) + b2_ref[...]
    z = alpha * res_ref[...] + (1.0 + gate_ref[0]) * y
    o_ref[...] = _ln_rows(z, g_ref[...], bb_ref[...])


def conv_tail(u, w_dw, b_dw, ln_g, ln_b, w_pw2, b_pw2, res, gate, g, b, *, seq, alpha, tm=512):
    T, D = u.shape
    nb = seq // tm
    halo = 32
    assert CONV_W - 1 <= halo and tm % halo == 0
    r = tm // halo
    wpad = jnp.zeros((halo, D), F32).at[:CONV_W].set(w_dw)
    kern = functools.partial(_conv_tail_kernel, nb=nb, halo=halo, width=CONV_W, alpha=alpha)
    vec = lambda a: a.reshape(1, D)
    one = pl.BlockSpec((1, D), lambda i: (0, 0))
    return pl.pallas_call(
        kern,
        grid=(T // tm,),
        in_specs=[pl.BlockSpec((tm, D), lambda i: (i, 0)),
                  pl.BlockSpec((halo, D), lambda i: (jnp.maximum(i * r - 1, 0), 0)),
                  pl.BlockSpec((halo, D), lambda i: (0, 0)),
                  one, one, one,
                  pl.BlockSpec((D, D), lambda i: (0, 0)),
                  one,
                  pl.BlockSpec((tm, D), lambda i: (i, 0)),
                  pl.BlockSpec((1, 1, D), lambda i: (i // nb, 0, 0)),
                  one, one],
        out_specs=pl.BlockSpec((tm, D), lambda i: (i, 0)),
        out_shape=jax.ShapeDtypeStruct((T, D), F32),
        scratch_shapes=[pltpu.VMEM((halo + tm, D), F32), pltpu.VMEM((7, halo + tm, D), F32)],
        compiler_params=_params(("parallel",)),
        name="conv_tail",
    )(u, u, wpad, vec(b_dw), vec(ln_g), vec(ln_b), w_pw2, vec(b_pw2), res, gate, vec(g), vec(b))


def _pool_kernel(x_ref, xp_ref, sc_ref, sh_ref, w_ref, pb_ref, ps_ref, gate_ref, g_ref, bb_ref,
                 o_ref, win_ref, *, nb, halo, wins, alpha):
    tm, D = x_ref.shape
    gw = D // len(wins)
    i = pl.program_id(0)
    first = (i % nb) == 0
    x = x_ref[...]
    sc, sh = sc_ref[0], sh_ref[0]
    hp = xp_ref[...] * (1.0 + sc) + sh
    win_ref[0:halo, :] = jnp.where(first, jnp.zeros_like(hp), hp)
    win_ref[halo:halo + tm, :] = x * (1.0 + sc) + sh
    t = (i % nb) * tm + lax.broadcasted_iota(I32, (tm, gw), 0)
    ys = []
    for gi, wn in enumerate(wins):
        cols = slice(gi * gw, (gi + 1) * gw)
        s = win_ref[halo:halo + tm, cols]
        hcur = s
        for k in range(1, wn):
            s = s + win_ref[halo - k:halo - k + tm, cols]
        cnt = jnp.minimum(t + 1, wn).astype(F32)
        pooled = s / cnt - hcur
        ys.append(_dot(pooled.astype(BF16), w_ref[gi]))
    y = (jnp.concatenate(ys, axis=-1) + pb_ref[...]) * ps_ref[...]
    z = alpha * x + (1.0 + gate_ref[0]) * y
    o_ref[...] = _ln_rows(z, g_ref[...], bb_ref[...])


def pool_layer(x, sc, sh, pool_w, pool_b, pool_scale, gate, g, b, *, seq, alpha, tm=512):
    T, D = x.shape
    nb = seq // tm
    halo = 16
    assert max(POOL_WINS) <= halo and tm % halo == 0
    r = tm // halo
    G, gw, _ = pool_w.shape
    kern = functools.partial(_pool_kernel, nb=nb, halo=halo, wins=POOL_WINS, alpha=alpha)
    vec = lambda a: a.reshape(1, D)
    one = pl.BlockSpec((1, D), lambda i: (0, 0))
    per_b = pl.BlockSpec((1, 1, D), lambda i: (i // nb, 0, 0))
    return pl.pallas_call(
        kern,
        grid=(T // tm,),
        in_specs=[pl.BlockSpec((tm, D), lambda i: (i, 0)),
                  pl.BlockSpec((halo, D), lambda i: (jnp.maximum(i * r - 1, 0), 0)),
                  per_b, per_b,
                  pl.BlockSpec((G, gw, gw), lambda i: (0, 0, 0)),
                  one, one, per_b, one, one],
        out_specs=pl.BlockSpec((tm, D), lambda i: (i, 0)),
        out_shape=jax.ShapeDtypeStruct((T, D), F32),
        scratch_shapes=[pltpu.VMEM((halo + tm, D), F32)],
        compiler_params=_params(("parallel",)),
        name="pool_layer",
    )(x, x, sc, sh, pool_w.astype(BF16), vec(pool_b), vec(pool_scale), gate, vec(g), vec(b))


def _rope64(x, cos, sin_signed, lane):
    partner = jnp.where((lane & (NSA_DH - 1)) < NSA_DH // 2, pltpu.roll(x, LANES - NSA_DH // 2, 1), pltpu.roll(x, NSA_DH // 2, 1))
    return x * cos + partner * sin_signed


def _nsa_proj_kernel(x_ref, sc_ref, sh_ref, w_ref, cos_ref, sin_ref, gb_ref, q_ref, cv_ref, kvx_ref, gt_ref, *, nq, qscale):
    h = x_ref[...] * (1.0 + sc_ref[0]) + sh_ref[0]
    acc = _dot(h.astype(BF16), w_ref[...])
    cos, sin = cos_ref[...], sin_ref[...]
    lane = lax.broadcasted_iota(I32, cos.shape, 1)
    for s in range(nq):
        blk = acc[:, s * LANES:(s + 1) * LANES]
        q_ref[:, s * LANES:(s + 1) * LANES] = (_rope64(blk, cos, sin, lane) * qscale).astype(q_ref.dtype)
    low = lane < NSA_DH
    for s in range(6):
        blk = acc[:, (nq + s) * LANES:(nq + s + 1) * LANES]
        if s % 2 == 0:
            blk = _rope64(blk, cos, sin, lane)
        if s < 2:
            cv_ref[:, s * LANES:(s + 1) * LANES] = blk.astype(cv_ref.dtype)
            continue
        swapped = pltpu.roll(blk, NSA_DH, 1)
        for gi in range(2):
            own, other = (blk, swapped) if gi == 0 else (swapped, blk)
            if s % 2 == 0:
                kvx_ref[3 * (s // 2 - 1), 0, gi] = jnp.where(low, own, other).astype(kvx_ref.dtype)
            else:
                kvx_ref[3 * (s // 2 - 1) + 1, 0, gi] = jnp.where(low, own, 0.0).astype(kvx_ref.dtype)
                kvx_ref[3 * (s // 2 - 1) + 2, 0, gi] = jnp.where(low, 0.0, other).astype(kvx_ref.dtype)
    for gi in range(gt_ref.shape[1] // LANES):
        gl = acc[:, (nq + 6 + gi) * LANES:(nq + 7 + gi) * LANES]
        gt_ref[:, gi * LANES:(gi + 1) * LANES] = _sigmoid(gl + gb_ref[:, gi * LANES:(gi + 1) * LANES])


def nsa_projection(x, sc, sh, w_in_p, cos, sin_signed, gate_b_p, *, seq, tm=512):
    T, K = x.shape
    Np = w_in_p.shape[1]
    nq = NSA_H * NSA_DH // LANES
    assert NSA_G * NSA_DH == LANES
    ng = NSA_G * LANES
    nb = seq // tm
    kern = functools.partial(_nsa_proj_kernel, nq=nq, qscale=NSA_DH ** -0.5)
    per_b = pl.BlockSpec((1, 1, K), lambda i: (i // nb, 0, 0))
    return pl.pallas_call(
        kern,
        grid=(T // tm,),
        in_specs=[pl.BlockSpec((tm, K), lambda i: (i, 0)), per_b, per_b,
                  pl.BlockSpec((K, Np), lambda i: (0, 0)),
                  pl.BlockSpec((tm, LANES), lambda i: (i, 0)),
                  pl.BlockSpec((tm, LANES), lambda i: (i, 0)),
                  pl.BlockSpec((1, ng), lambda i: (0, 0))],
        out_specs=[pl.BlockSpec((tm, nq * LANES), lambda i: (i, 0)),
                   pl.BlockSpec((tm, 2 * LANES), lambda i: (i, 0)),
                   pl.BlockSpec((6, 1, NSA_G, tm, LANES), lambda i: (0, i // nb, 0, i % nb, 0)),
                   pl.BlockSpec((tm, ng), lambda i: (i, 0))],
        out_shape=[jax.ShapeDtypeStruct((T, nq * LANES), BF16),
                   jax.ShapeDtypeStruct((T, 2 * LANES), BF16),
                   jax.ShapeDtypeStruct((6, T // seq, NSA_G, seq, LANES), BF16),
                   jax.ShapeDtypeStruct((T, ng), F32)],
        compiler_params=_params(("parallel",)),
        name="nsa_projection",
    )(x, sc, sh, w_in_p, cos, sin_signed, gate_b_p)


def _gelu_tanh(v):
    return 0.5 * v * (1.0 + jnp.tanh(math.sqrt(2.0 / math.pi) * (v + 0.044715 * v * v * v)))


def _compress_kernel(xk_ref, xv_ref, pk_ref, pv_ref, kw1_ref, kw2_ref, vw1_ref, vw2_ref, ok_ref, ov_ref):
    def one(x_ref, pe_ref, w1_ref, w2_ref, o_ref):
        x = x_ref[0].astype(F32)
        n = x.shape[0]
        half = x.shape[1]
        a = _dot((x + pe_ref[0:1, :]).astype(BF16), w1_ref[0:half, :])
        b = _dot((x + pe_ref[1:2, :]).astype(BF16), w1_ref[half:2 * half, :])
        pre = a + pltpu.roll(b, n - 1, 0)
        o_ref[0] = _dot(_gelu_tanh(pre).astype(BF16), w2_ref[...]).astype(o_ref.dtype)

    one(xk_ref, pk_ref, kw1_ref, kw2_ref, ok_ref)
    one(xv_ref, pv_ref, vw1_ref, vw2_ref, ov_ref)


def nsa_compress(kc_chunks, vc_chunks, pos_k, pos_v, k_w1, k_w2, v_w1, v_w2):
    BG, n, width = kc_chunks.shape
    hid = k_w1.shape[1]
    dh = k_w2.shape[1]
    full = lambda a: pl.BlockSpec(a.shape, lambda i: (0,) * a.ndim)
    pk = pos_k.reshape(2, width)
    pv = pos_v.reshape(2, width)
    ws = [k_w1.astype(BF16), k_w2.astype(BF16), v_w1.astype(BF16), v_w2.astype(BF16)]
    del hid
    return pl.pallas_call(
        _compress_kernel,
        grid=(BG,),
        in_specs=[pl.BlockSpec((1, n, width), lambda i: (i, 0, 0)),
                  pl.BlockSpec((1, n, width), lambda i: (i, 0, 0)),
                  full(pk), full(pv)] + [full(w) for w in ws],
        out_specs=[pl.BlockSpec((1, n, dh), lambda i: (i, 0, 0)),
                   pl.BlockSpec((1, n, dh), lambda i: (i, 0, 0))],
        out_shape=[jax.ShapeDtypeStruct((BG, n, dh), BF16), jax.ShapeDtypeStruct((BG, n, dh), BF16)],
        compiler_params=_params(("parallel",)),
        name="nsa_compress",
    )(kc_chunks, vc_chunks, pk, pv, *ws)


def _nsa_attn_kernel(q_ref, gt_ref, kc_ref, vcl_ref, vch_ref, ks_ref, vsl_ref, vsh_ref, kw_ref, vwl_ref, vwh_ref,
                     ov_ref, o_ref, qs_ref, m_ref, l_ref, ae_ref, ao_ref, res_ref, sc_ref,
                     *, qt, ck, n_sel, top_n, sel_shift, cmp_len, cmp_step, window):
    t0 = pl.program_id(2) * qt
    R = q_ref.shape[2] // NSA_DH
    rows = R * qt
    half_rows = rows // 2
    lane = lax.broadcasted_iota(I32, (qt, LANES), 1)

    for p in range(R // 2):
        slab = q_ref[0, :, p * LANES:(p + 1) * LANES]
        qs_ref[p * qt:(p + 1) * qt, :] = jnp.where(lane < NSA_DH, slab, jnp.zeros_like(slab))
        qs_ref[half_rows + p * qt:half_rows + (p + 1) * qt, :] = jnp.where(lane >= NSA_DH, slab, jnp.zeros_like(slab))
    qs = qs_ref[...]
    tq = t0 + lax.broadcasted_iota(I32, (qt, 1), 0)

    ncmp = kc_ref.shape[0]
    cmp_end = lax.broadcasted_iota(I32, (qt, ncmp), 1) * cmp_step + (cmp_len - 1)
    s3 = _dot_nt(qs, kc_ref[...]).reshape(R, qt, ncmp) + jnp.where(cmp_end <= tq, 0.0, NEG_BIG)
    e3 = jnp.exp(s3 - jnp.max(s3, axis=-1, keepdims=True))
    row_ok = jnp.where(tq >= cmp_len - 1, 1.0, 0.0)
    pc3 = e3 * (row_ok / jnp.sum(e3, axis=-1, keepdims=True))
    pcb = pc3.reshape(rows, ncmp).astype(BF16)
    res_ref[0, 0:half_rows, :] = _dot(pcb[0:half_rows], vcl_ref[...])
    res_ref[0, half_rows:rows, :] = _dot(pcb[half_rows:], vch_ref[...])
    psum = jnp.sum(pc3, axis=0)
    imp = _dot(psum.astype(BF16), ov_ref[...])

    cur = lax.shift_right_logical(tq, sel_shift)
    dcur = cur - lane
    forced = (lane == 0) | (dcur == 0) | (dcur == 1)
    score = jnp.where(lax.shift_left(lane, sel_shift) <= tq, jnp.where(forced, NSA_FORCED, imp), NEG_BIG)
    score = jnp.where(lane < n_sel, score, -jnp.inf)
    st = score.T
    blk_f = lax.broadcasted_iota(I32, st.shape, 0).astype(F32)
    sel_t = jnp.zeros(st.shape, F32)
    for _ in range(top_n):
        mx = jnp.max(st, axis=0, keepdims=True)
        idx = jnp.min(jnp.where(st == mx, blk_f, float(LANES)), axis=0, keepdims=True)
        hit = blk_f == idx
        sel_t = jnp.where(hit, jnp.where(mx > 0.5 * NEG_BIG, 1.0, sel_t), sel_t)
        st = jnp.where(hit, -jnp.inf, st)
    selb = sel_t.T.astype(BF16)

    def attend(k_ref, vl_ref, vh_ref, c_lo, c_hi, bias_fn, slot):
        m_ref[...] = jnp.full(m_ref.shape, NEG_BIG, F32)
        l_ref[...] = jnp.zeros(l_ref.shape, F32)
        ae_ref[...] = jnp.zeros(ae_ref.shape, F32)
        ao_ref[...] = jnp.zeros(ao_ref.shape, F32)

        def scores(c):
            start = pl.multiple_of(c * ck, ck)
            return _dot_nt(qs, k_ref[pl.ds(start, ck), :]).reshape(R, qt, ck) + bias_fn(start)

        sc_ref[...] = scores(c_lo)

        def body(c, carry):
            start = pl.multiple_of(c * ck, ck)
            sc_next = scores(jnp.minimum(c + 1, c_hi - 1))
            sc3 = sc_ref[...]
            m_old = m_ref[...]
            m_new = jnp.maximum(m_old, jnp.max(sc3, axis=-1, keepdims=True).reshape(rows, 1))
            alpha = jnp.exp(m_old - m_new)
            p3 = jnp.exp(sc3 - m_new.reshape(R, qt, 1))
            l_ref[...] = alpha * l_ref[...] + jnp.sum(p3, axis=-1, keepdims=True).reshape(rows, 1)
            m_ref[...] = m_new
            pb = p3.reshape(rows, ck).astype(BF16)
            ae_ref[...] = alpha[0:half_rows] * ae_ref[...] + _dot(pb[0:half_rows], vl_ref[pl.ds(start, ck), :])
            ao_ref[...] = alpha[half_rows:] * ao_ref[...] + _dot(pb[half_rows:], vh_ref[pl.ds(start, ck), :])
            sc_ref[...] = sc_next
            return carry

        lax.fori_loop(c_lo, c_hi, body, 0)
        inv = 1.0 / l_ref[...]
        res_ref[slot, 0:half_rows, :] = ae_ref[...] * inv[0:half_rows]
        res_ref[slot, half_rows:rows, :] = ao_ref[...] * inv[half_rows:]

    def key_dist(start):
        return tq - (start + lax.broadcasted_iota(I32, (qt, ck), 1))

    def sel_bias(start):
        jrow = lax.broadcasted_iota(I32, (LANES, ck), 0)
        jcol = lax.shift_right_logical(start + lax.broadcasted_iota(I32, (LANES, ck), 1), sel_shift)
        expand = jnp.where(jrow == jcol, 1.0, 0.0).astype(BF16)
        picked = _dot(selb, expand)
        return jnp.where(key_dist(start) >= 0, jnp.where(picked > 0.5, 0.0, NEG_BIG), NEG_BIG)

    def win_bias(start):
        dist = key_dist(start)
        return jnp.where(jnp.where(dist >= 0, dist, window) < window, 0.0, NEG_BIG)

    c_hi = (t0 + qt + ck - 1) // ck
    attend(ks_ref, vsl_ref, vsh_ref, 0, c_hi, sel_bias, 1)
    attend(kw_ref, vwl_ref, vwh_ref, jnp.maximum(t0 - window + 1, 0) // ck, c_hi, win_bias, 2)

    gt = gt_ref[...]
    for p in range(R // 2):
        acc = jnp.zeros((qt, LANES), F32)
        for par in range(2):
            rsl = slice(par * half_rows + p * qt, par * half_rows + (p + 1) * qt)
            for br in range(3):
                col = (2 * p + par) * 3 + br
                acc = acc + gt[:, col:col + 1] * res_ref[br, rsl, :]
        o_ref[0, :, p * LANES:(p + 1) * LANES] = acc.astype(o_ref.dtype)


def nsa_attention(q, gates, kcd, vcl, vch, kvx, *, batch, seq, qt=256, ck=512):
    G = NSA_G
    R = NSA_H // G
    gq = R * NSA_DH
    n_sel = seq // NSA_SEL_LEN
    ncmp = kcd.shape[1]
    nq = seq // qt
    ov = np.zeros((ncmp, LANES), np.float32)
    for n in range(ncmp):
        for j in range(n_sel):
            if n * NSA_CMP_STEP < (j + 1) * NSA_SEL_LEN and n * NSA_CMP_STEP + NSA_CMP_LEN > j * NSA_SEL_LEN:
                ov[n, j] = 1.0
    n_valid = (seq - NSA_CMP_LEN) // NSA_CMP_STEP + 1
    ov[n_valid:] = 0.0
    kern = functools.partial(_nsa_attn_kernel, qt=qt, ck=ck, n_sel=n_sel, top_n=min(NSA_TOPN, n_sel),
                             sel_shift=int(math.log2(NSA_SEL_LEN)), cmp_len=NSA_CMP_LEN, cmp_step=NSA_CMP_STEP, window=NSA_WIN)
    cmp_spec = pl.BlockSpec((None, ncmp, LANES), lambda b, g, i: (b * G + g, 0, 0))
    seq_specs = [pl.BlockSpec((None, None, None, seq, LANES), functools.partial(lambda b, g, i, k: (k, b, g, 0, 0), k=k))
                 for k in range(6)]
    rows = R * qt
    return pl.pallas_call(
        kern,
        grid=(batch, G, nq),
        in_specs=[pl.BlockSpec((1, qt, gq), lambda b, g, i: (b, i, g)),
                  pl.BlockSpec((qt, LANES), lambda b, g, i: (b * nq + i, g)),
                  cmp_spec, cmp_spec, cmp_spec] + seq_specs + [
                  pl.BlockSpec((ncmp, LANES), lambda b, g, i: (0, 0))],
        out_specs=pl.BlockSpec((1, qt, gq), lambda b, g, i: (b, i, g)),
        out_shape=jax.ShapeDtypeStruct((batch, seq, NSA_H * NSA_DH), BF16),
        scratch_shapes=[pltpu.VMEM((rows, LANES), BF16),
                        pltpu.VMEM((rows, 1), F32), pltpu.VMEM((rows, 1), F32),
                        pltpu.VMEM((rows // 2, LANES), F32), pltpu.VMEM((rows // 2, LANES), F32),
                        pltpu.VMEM((3, rows, LANES), F32),
                        pltpu.VMEM((R, qt, ck), F32)],
        compiler_params=_params(("parallel", "parallel", "arbitrary")),
        name="nsa_attention",
    )(q, gates, kcd, vcl, vch, kvx, kvx, kvx, kvx, kvx, kvx, jnp.asarray(ov, BF16))


def _tile_rows(ref, idx, ns):
    return ref.at[pl.ds(pl.multiple_of(idx * ns, ns), ns)]


def _tile_rows_from(ref, row0, ns):
    return ref.at[pl.ds(pl.multiple_of(row0, ns), ns)]


def _router_kernel(x_ref, sc_ref, sh_ref, w_ref, b_ref, h3_ref, meta_ref, gate_ref, cnt_ref, seen_ref, *, top_k):
    @pl.when(pl.program_id(0) == 0)
    def _():
        seen_ref[...] = jnp.zeros_like(seen_ref)

    h = x_ref[...] * (1.0 + sc_ref[0]) + sh_ref[0]
    tm = h.shape[0]
    ns = h.shape[1] // LANES
    for s in range(ns):
        h3_ref[pl.ds(s, tm, stride=ns), :] = h[:, s * LANES:(s + 1) * LANES]
    logits = _dot(h.astype(BF16), w_ref[...]) + b_ref[...]
    lane = lax.broadcasted_iota(I32, logits.shape, 1)
    lane_f = lane.astype(F32)
    val_out = jnp.full(logits.shape, NEG_BIG, F32)
    chosen = jnp.zeros(logits.shape, F32)
    picks = []
    for k in range(top_k):
        mx = jnp.max(logits, axis=-1, keepdims=True)
        idx = jnp.min(jnp.where(logits == mx, lane_f, float(LANES)), axis=-1, keepdims=True)
        hit = lane_f == idx
        picks.append((idx, hit))
        chosen = jnp.where(hit, 1.0, chosen)
        val_out = jnp.where(lane == k, mx, val_out)
        logits = jnp.where(hit, -jnp.inf, logits)
    e = jnp.exp(val_out - jnp.max(val_out, axis=-1, keepdims=True))
    e = jnp.where(lane < top_k, e, 0.0)
    gate_ref[...] = e / jnp.sum(e, axis=-1, keepdims=True)
    lower = jnp.where(lax.broadcasted_iota(I32, (tm, tm), 0) > lax.broadcasted_iota(I32, (tm, tm), 1), 1.0, 0.0)
    before = _dot(lower.astype(BF16), chosen.astype(BF16)) + seen_ref[...]
    meta = jnp.zeros(logits.shape, I32)
    for k, (idx, hit) in enumerate(picks):
        rank = jnp.sum(jnp.where(hit, before, 0.0), axis=-1, keepdims=True)
        meta = jnp.where(lane == k, idx.astype(I32), meta)
        meta = jnp.where(lane == top_k + k, rank.astype(I32), meta)
    meta_ref[...] = meta
    seen_ref[...] = seen_ref[...] + jnp.sum(chosen, axis=0, keepdims=True)
    cnt_ref[...] = seen_ref[...].astype(I32)


def moe_router(x, sc, sh, router_w, router_b, *, seq, tm=512):
    T, D = x.shape
    E = router_w.shape[1]
    nb = seq // tm
    ns = D // LANES
    assert ns == 8
    wp = jnp.zeros((D, LANES), BF16).at[:, :E].set(router_w.astype(BF16))
    bp = jnp.full((1, LANES), NEG_BIG, F32).at[0, :E].set(router_b)
    per_b = pl.BlockSpec((1, 1, D), lambda i: (i // nb, 0, 0))
    return pl.pallas_call(
        functools.partial(_router_kernel, top_k=MOE_K),
        grid=(T // tm,),
        in_specs=[pl.BlockSpec((tm, D), lambda i: (i, 0)), per_b, per_b,
                  pl.BlockSpec((D, LANES), lambda i: (0, 0)),
                  pl.BlockSpec((1, LANES), lambda i: (0, 0))],
        out_specs=[pl.BlockSpec((tm * ns, LANES), lambda i: (i, 0)),
                   pl.BlockSpec((tm, LANES), lambda i: (i, 0)),
                   pl.BlockSpec((tm, LANES), lambda i: (i, 0)),
                   pl.BlockSpec((1, LANES), lambda i: (0, 0))],
        out_shape=[jax.ShapeDtypeStruct((T * ns, LANES), F32),
                   jax.ShapeDtypeStruct((T, LANES), I32),
                   jax.ShapeDtypeStruct((T, LANES), F32),
                   jax.ShapeDtypeStruct((1, LANES), I32)],
        scratch_shapes=[pltpu.VMEM((1, LANES), F32)],
        compiler_params=_params(("arbitrary",)),
        name="moe_router",
    )(x, sc, sh, wp, bp)


def _dispatch_kernel(fill_ref, slots_ref, slots1_ref, slots2_ref, h3_ref, xs3_ref, zero_ref, hbuf_ref, fill_sem,
                     load_sems, row_sems, *, tm, top_k, n_exp, ns):
    i = pl.program_id(0)
    n = pl.num_programs(0)
    nbuf = hbuf_ref.shape[0]

    def load(step):
        src = h3_ref.at[pl.ds(pl.multiple_of(step * (tm * ns), tm * ns), tm * ns)]
        return pltpu.make_async_copy(src, hbuf_ref.at[step % nbuf], load_sems.at[step % nbuf])

    def row_copy(step, slots, r, k):
        return pltpu.make_async_copy(_tile_rows(hbuf_ref.at[step % nbuf], r, ns),
                                     _tile_rows_from(xs3_ref, slots[r * top_k + k], ns), row_sems.at[step % nbuf])

    def issue_rows(step, slots):
        def body(r, carry):
            for k in range(top_k):
                row_copy(step, slots, r, k).start(priority=k % 2)
            return carry
        lax.fori_loop(0, tm, body, 0, unroll=8)

    def drain_rows(step, slots):
        def body(r, carry):
            for k in range(top_k):
                row_copy(step, slots, r, k).wait()
            return carry
        lax.fori_loop(0, tm, body, 0, unroll=8)

    @pl.when(i == 0)
    def _():
        load(0).start()
        zero_ref[...] = jnp.zeros_like(zero_ref)
        for e in range(n_exp):
            dst = xs3_ref.at[pl.ds(pl.multiple_of(fill_ref[e] * ns, ns), zero_ref.shape[0])]
            fill = pltpu.make_async_copy(zero_ref, dst, fill_sem)
            fill.start()
            fill.wait()

    @pl.when(i >= 2)
    def _():
        drain_rows(i - 2, slots2_ref)

    @pl.when(i + 1 < n)
    def _():
        load(i + 1).start()

    load(i).wait()
    issue_rows(i, slots_ref)

    @pl.when(i == n - 1)
    def _():
        @pl.when(i >= 1)
        def _():
            drain_rows(i - 1, slots1_ref)
        drain_rows(i, slots_ref)


def moe_dispatch(h3, slots, fill_start, n_rows, *, tm=256):
    ns = 8
    T = h3.shape[0] // ns
    K = MOE_K
    slot_spec = lambda back: pl.BlockSpec((tm * K,), lambda i, fill: (jnp.maximum(i - back, 0),), memory_space=pltpu.SMEM)
    grid_spec = pltpu.PrefetchScalarGridSpec(
        num_scalar_prefetch=1,
        grid=(T // tm,),
        in_specs=[slot_spec(0), slot_spec(1), slot_spec(2), pl.BlockSpec(memory_space=pl.ANY)],
        out_specs=pl.BlockSpec(memory_space=pl.ANY),
        scratch_shapes=[pltpu.VMEM((MOE_BLK * ns, LANES), F32), pltpu.VMEM((3, tm * ns, LANES), F32),
                        pltpu.SemaphoreType.DMA(()), pltpu.SemaphoreType.DMA((3,)), pltpu.SemaphoreType.DMA((3,))],
    )
    return pl.pallas_call(
        functools.partial(_dispatch_kernel, tm=tm, top_k=K, n_exp=MOE_E, ns=ns),
        grid_spec=grid_spec,
        out_shape=jax.ShapeDtypeStruct((n_rows * ns, LANES), F32),
        compiler_params=_params(("arbitrary",)),
        name="moe_dispatch",
    )(fill_start, slots, slots, slots, h3)


def _expert_kernel(be_ref, nu_ref, nv_ref, x3_ref, w1_ref, b1_ref, w2_ref, b2_ref, o3_ref, w1b_ref, w2b_ref, *, dff, sub):
    i = pl.program_id(0)
    ns = w1b_ref.shape[0] // LANES
    blk = x3_ref.shape[0] // ns

    @pl.when(i < nu_ref[0])
    def _():
        @pl.when(jnp.logical_or(i == 0, be_ref[i] != be_ref[jnp.maximum(i - 1, 0)]))
        def _():
            w1b_ref[...] = w1_ref[0].astype(BF16)
            w2b_ref[...] = w2_ref[0].astype(BF16)

        for part in range(blk // sub):
            @pl.when(nv_ref[i] > part * sub)
            def _():
                r0 = part * sub * ns
                xb = jnp.concatenate([x3_ref[pl.ds(r0 + s, sub, stride=ns), :].astype(BF16) for s in range(ns)], axis=-1)
                gu = _dot(xb, w1b_ref[...]) + b1_ref[0]
                gg = jnp.minimum(gu[:, :dff], GLU_LIMIT)
                up = jnp.clip(gu[:, dff:], -GLU_LIMIT, GLU_LIMIT)
                act = (up + 1.0) * (gg * _sigmoid(GLU_ALPHA * gg))
                y = _dot(act.astype(BF16), w2b_ref[...]) + b2_ref[0]
                for s in range(ns):
                    o3_ref[pl.ds(r0 + s, sub, stride=ns), :] = y[:, s * LANES:(s + 1) * LANES]


def moe_experts(xs3, block_expert, n_used, n_valid, w1, b1, w2, b2, layer):
    L, E, D, two_f = w1.shape
    ns = D // LANES
    nblk = block_expert.shape[0]
    row_map = lambda i, be, nu, nv: (jnp.minimum(i, nu[0] - 1), 0)
    w_map = lambda i, be, nu, nv: (layer, be[i], 0, 0)
    grid_spec = pltpu.PrefetchScalarGridSpec(
        num_scalar_prefetch=3,
        grid=(nblk,),
        in_specs=[pl.BlockSpec((MOE_BLK * ns, LANES), row_map),
                  pl.BlockSpec((None, 1, D, two_f), w_map),
                  pl.BlockSpec((None, 1, 1, two_f), w_map),
                  pl.BlockSpec((None, 1, two_f // 2, D), w_map),
                  pl.BlockSpec((None, 1, 1, D), w_map)],
        out_specs=pl.BlockSpec((MOE_BLK * ns, LANES), row_map),
        scratch_shapes=[pltpu.VMEM((D, two_f), BF16), pltpu.VMEM((two_f // 2, D), BF16)],
    )
    return pl.pallas_call(
        functools.partial(_expert_kernel, dff=two_f // 2, sub=MOE_SUB),
        grid_spec=grid_spec,
        out_shape=jax.ShapeDtypeStruct((nblk * MOE_BLK * ns, LANES), F32),
        compiler_params=_params(("arbitrary",)),
        name="moe_experts",
    )(block_expert, n_used, n_valid, xs3, w1, b1.reshape(L, E, 1, two_f), w2, b2.reshape(L, E, 1, D))


def _combine_kernel(slots_ref, nslots_ref, out3_ref, gt_ref, res_ref, gate_ref, g_ref, bb_ref, o_ref, y4_ref, sems,
                    *, tm, top_k, alpha, ns):
    i = pl.program_id(0)
    buf = i % 2

    def row_copy(slots, b, r, k):
        return pltpu.make_async_copy(_tile_rows_from(out3_ref, slots[r * top_k + k], ns),
                                     _tile_rows(y4_ref.at[b, k], r, ns), sems.at[b])

    def issue(slots, b):
        def body(r, carry):
            for k in range(top_k):
                row_copy(slots, b, r, k).start(priority=k % 2)
            return carry
        lax.fori_loop(0, tm, body, 0, unroll=8)

    def drain(slots, b):
        def body(r, carry):
            for k in range(top_k):
                row_copy(slots, b, r, k).wait()
            return carry
        lax.fori_loop(0, tm, body, 0, unroll=8)

    @pl.when(i == 0)
    def _():
        issue(slots_ref, 0)

    @pl.when(i + 1 < pl.num_programs(0))
    def _():
        issue(nslots_ref, 1 - buf)

    drain(slots_ref, buf)
    gt = gt_ref[...]
    cols = []
    for s in range(ns):
        acc = gt[:, 0:1] * y4_ref[buf, 0, pl.ds(s, tm, stride=ns), :]
        for k in range(1, top_k):
            acc = acc + gt[:, k:k + 1] * y4_ref[buf, k, pl.ds(s, tm, stride=ns), :]
        cols.append(acc)
    y = jnp.concatenate(cols, axis=-1)
    z = alpha * res_ref[...] + (1.0 + gate_ref[0]) * y
    o_ref[...] = _ln_rows(z, g_ref[...], bb_ref[...])


def moe_combine(out3, slots, gates, res, gate, g, b, *, seq, alpha, tm=256):
    T, D = res.shape
    K = MOE_K
    ns = D // LANES
    nb = seq // tm
    nsteps = T // tm
    one = pl.BlockSpec((1, D), lambda i: (0, 0))
    return pl.pallas_call(
        functools.partial(_combine_kernel, tm=tm, top_k=K, alpha=alpha, ns=ns),
        grid=(nsteps,),
        in_specs=[pl.BlockSpec((tm * K,), lambda i: (i,), memory_space=pltpu.SMEM),
                  pl.BlockSpec((tm * K,), lambda i: (jnp.minimum(i + 1, nsteps - 1),), memory_space=pltpu.SMEM),
                  pl.BlockSpec(memory_space=pl.ANY),
                  pl.BlockSpec((tm, LANES), lambda i: (i, 0)),
                  pl.BlockSpec((tm, D), lambda i: (i, 0)),
                  pl.BlockSpec((1, 1, D), lambda i: (i // nb, 0, 0)), one, one],
        out_specs=pl.BlockSpec((tm, D), lambda i: (i, 0)),
        out_shape=jax.ShapeDtypeStruct((T, D), F32),
        scratch_shapes=[pltpu.VMEM((2, K, tm * ns, LANES), F32), pltpu.SemaphoreType.DMA((2,))],
        compiler_params=_params(("arbitrary",)),
        name="moe_combine",
    )(slots, slots, out3, gates, res, gate, g.reshape(1, D), b.reshape(1, D))


def moe_layer(x, sc, sh, gate, ln_g, ln_b, router_w, router_b, w1, b1, w2, b2, layer, *, seq, alpha):
    T, D = x.shape
    E, K = MOE_E, MOE_K
    h3, meta, gates, cnt = moe_router(x, sc, sh, router_w, router_b, seq=seq)
    counts = cnt[0, :E]
    blocks_per = (counts + MOE_BLK - 1) // MOE_BLK
    block_end = jnp.cumsum(blocks_per)
    pad_start = (block_end - blocks_per) * MOE_BLK
    n_blocks = -(-(T * K) // MOE_BLK) + E
    blk = jnp.arange(n_blocks, dtype=I32)
    block_expert = jnp.minimum(jnp.sum(block_end[None, :] <= blk[:, None], axis=1), E - 1).astype(I32)
    onehot = meta[:, :K, None] == jnp.arange(E, dtype=I32)[None, None, :]
    slots = (meta[:, K:2 * K] + jnp.sum(jnp.where(onehot, pad_start[None, None, :], 0), axis=-1)).reshape(-1).astype(I32)
    slots = slots * (D // LANES)
    fill_start = (pad_start + counts).astype(I32)
    first_blk = block_end - blocks_per
    n_valid = jnp.clip(counts[block_expert] - (blk - first_blk[block_expert]) * MOE_BLK, 0, MOE_BLK)
    n_valid = jnp.where(blk < block_end[-1], n_valid, 0).astype(I32)
    xs3 = moe_dispatch(h3, slots, fill_start, (n_blocks + 1) * MOE_BLK)
    out3 = moe_experts(xs3, block_expert, block_end[-1:].astype(I32), n_valid, w1, b1, w2, b2, layer)
    return moe_combine(out3, slots, gates, x, gate, ln_g, ln_b, seq=seq, alpha=alpha)


def _rope_tables(positions, d):
    inv = ROPE_BASE ** (-jnp.arange(0, d, 2, dtype=F32) / d)
    ang = positions.astype(F32).reshape(-1)[:, None] * inv
    return jnp.cos(ang), jnp.sin(ang)


def kernel(x, c, positions, ada_w, ada_b, ln1_g, ln1_b, ln2_g, ln2_b, router_w, router_b, moe_w1, moe_b1, moe_w2, moe_b2, ret_w_in, ret_gn_g, ret_gn_b, ret_w_out, conv_w_pw1, conv_b_pw1, conv_w_dw, conv_b_dw, conv_ln_g, conv_ln_b, conv_w_pw2, conv_b_pw2, nsa_w_in, nsa_gate_b, nsa_cmp_pos_k, nsa_cmp_pos_v, nsa_cmp_k_w1, nsa_cmp_k_w2, nsa_cmp_v_w1, nsa_cmp_v_w2, nsa_w_out, pool_w, pool_b, pool_scale):
    B, S, D = x.shape
    T = B * S
    depth = ada_w.shape[0]
    alpha = (2.0 * depth) ** 0.25
    mod = ada_modulation(c, ada_w, ada_b)
    xf = x.reshape(T, D)
    for i in range(depth):
        kind, j = i % 4, i // 4
        sh1, sc1, g1, sh2, sc2, g2 = (mod[i, :, k * D:(k + 1) * D].reshape(B, 1, D) for k in range(6))
        res1 = (g1, ln1_g[i], ln1_b[i])
        if kind == 0:
            dk = ret_w_in.shape[2] // 6 // RET_H
            cos, sin = _rope_tables(positions, dk)
            p = ret_projection(xf, sc1, sh1, ret_w_in[j].astype(BF16), cos, sin, seq=S)
            o = retention_core(p, ret_gn_g[j], ret_gn_b[j], batch=B, seq=S)
            xf = fused_matmul(o, ret_w_out[j].astype(BF16), seq=S, resln=(xf,) + res1, alpha=alpha, name="ret_out")
        elif kind == 1:
            u = conv_glu(xf, sc1, sh1, conv_w_pw1[j].astype(BF16), conv_b_pw1[j], seq=S)
            xf = conv_tail(u, conv_w_dw[j], conv_b_dw[j], conv_ln_g[j], conv_ln_b[j], conv_w_pw2[j].astype(BF16),
                           conv_b_pw2[j], xf, *res1, seq=S, alpha=alpha)
        elif kind == 2:
            G, dh, H = NSA_G, NSA_DH, NSA_H
            cos, sin = _rope_tables(positions, dh)
            cos_p = jnp.tile(cos, (1, LANES // (dh // 2)))
            sin_p = jnp.tile(jnp.concatenate([-sin, sin], axis=1), (1, LANES // dh))
            n_main = H * dh + 6 * G * dh
            ng = 3 * H // G
            w_g = jnp.zeros((D, G, LANES), F32).at[:, :, :ng].set(nsa_w_in[j][:, n_main:].reshape(D, G, ng))
            w_p = jnp.concatenate([nsa_w_in[j][:, :n_main], w_g.reshape(D, G * LANES)], axis=1).astype(BF16)
            gb_p = jnp.zeros((G, LANES), F32).at[:, :ng].set(nsa_gate_b[j].reshape(G, ng)).reshape(1, G * LANES)
            q, cv, kvx, gates = nsa_projection(xf, sc1, sh1, w_p, cos_p, sin_p, gb_p, seq=S)
            cv = cv.reshape(B, S, 2, G, dh).transpose(2, 0, 3, 1, 4).reshape(2, B * G, S // NSA_CMP_STEP, NSA_CMP_STEP * dh)
            k_cmp, v_cmp = nsa_compress(cv[0], cv[1], nsa_cmp_pos_k[j], nsa_cmp_pos_v[j], nsa_cmp_k_w1[j],
                                        nsa_cmp_k_w2[j], nsa_cmp_v_w1[j], nsa_cmp_v_w2[j])
            dup = lambda a: jnp.concatenate([a, a], axis=-1)
            lo = lambda a: jnp.concatenate([a, jnp.zeros_like(a)], axis=-1)
            hi = lambda a: jnp.concatenate([jnp.zeros_like(a), a], axis=-1)
            o = nsa_attention(q.reshape(B, S, H * dh), gates, dup(k_cmp), lo(v_cmp), hi(v_cmp), kvx, batch=B, seq=S)
            xf = fused_matmul(o.reshape(T, H * dh), nsa_w_out[j].astype(BF16), seq=S, resln=(xf,) + res1, alpha=alpha,
                              name="nsa_out")
        else:
            xf = pool_layer(xf, sc1, sh1, pool_w[j], pool_b[j], pool_scale[j], *res1, seq=S, alpha=alpha)
        xf = moe_layer(xf, sc2, sh2, g2, ln2_g[i], ln2_b[i], router_w[i], router_b[i], moe_w1, moe_b1, moe_w2, moe_b2, i,
                       seq=S, alpha=alpha)
    return xf.reshape(B, S, D)
```

```python
import functools
import math

import jax
import jax.numpy as jnp
import numpy as np
from jax import lax
from jax.experimental import pallas as pl
from jax.experimental.pallas import tpu as pltpu

F32 = jnp.float32
BF16 = jnp.bfloat16
I32 = jnp.int32

ROPE_BASE = 10000.0
EPS = 1e-5
NEG_BIG = -1e30

RET_H = 4
RET_C = 128
CONV_W = 31
NSA_H = 16
NSA_G = 2
NSA_DH = 64
NSA_CMP_LEN = 32
NSA_CMP_STEP = 16
NSA_SEL_LEN = 64
NSA_TOPN = 16
NSA_WIN = 512
NSA_FORCED = 1e6
POOL_WINS = (2, 4, 8, 16)
MOE_E = 32
MOE_K = 4
MOE_BLK = 512
MOE_SUB = 256
GLU_LIMIT = 7.0
GLU_ALPHA = 1.702

V7X_VMEM_LIMIT = 56 * 1024 * 1024
LANES = 128


def _params(sem):
    return pltpu.CompilerParams(dimension_semantics=sem, vmem_limit_bytes=V7X_VMEM_LIMIT)


def _ln_rows(v, g, b):
    mu = jnp.mean(v, axis=-1, keepdims=True)
    d = v - mu
    var = jnp.mean(d * d, axis=-1, keepdims=True)
    return d * lax.rsqrt(var + EPS) * g + b


def _sigmoid(v):
    return 1.0 / (1.0 + jnp.exp(-v))


def _dot(a, b):
    return jnp.dot(a, b, preferred_element_type=F32)


def _dot_nt(a, b):
    return lax.dot_general(a, b, (((1,), (1,)), ((), ())), preferred_element_type=F32)


def _dot_tn(a, b):
    return lax.dot_general(a, b, (((0,), (0,)), ((), ())), preferred_element_type=F32)


def _ada_kernel(c_ref, w_ref, b_ref, o_ref):
    c = c_ref[...]
    ca = c * _sigmoid(c)
    o_ref[0] = _dot(ca.astype(BF16), w_ref[0].astype(BF16)) + b_ref[0]


def ada_modulation(c, ada_w, ada_b):
    L, D, N = ada_w.shape
    B = c.shape[0]
    tn = 2048 if N % 2048 == 0 else N
    return pl.pallas_call(
        _ada_kernel,
        grid=(L, N // tn),
        in_specs=[pl.BlockSpec((B, D), lambda l, j: (0, 0)),
                  pl.BlockSpec((1, D, tn), lambda l, j: (l, 0, j)),
                  pl.BlockSpec((1, 1, tn), lambda l, j: (l, 0, j))],
        out_specs=pl.BlockSpec((1, B, tn), lambda l, j: (l, 0, j)),
        out_shape=jax.ShapeDtypeStruct((L, B, N), F32),
        compiler_params=_params(("parallel", "parallel")),
        name="ada_modulation",
    )(c, ada_w, ada_b.reshape(L, 1, N))


def _mm_kernel(*refs, has_mod, has_bias, epi, alpha):
    it = iter(refs)
    x_ref = next(it)
    if has_mod:
        sc_ref, sh_ref = next(it), next(it)
    w_ref = next(it)
    if has_bias:
        b_ref = next(it)
    if epi == "resln":
        res_ref, gate_ref, g_ref, bb_ref = next(it), next(it), next(it), next(it)
    o_ref = next(it)
    x = x_ref[...]
    if has_mod:
        x = x.astype(F32) * (1.0 + sc_ref[0]) + sh_ref[0]
    acc = _dot(x.astype(BF16), w_ref[...])
    if has_bias:
        acc = acc + b_ref[...]
    if epi == "resln":
        v = alpha * res_ref[...] + (1.0 + gate_ref[0]) * acc
        acc = _ln_rows(v, g_ref[...], bb_ref[...])
    o_ref[...] = acc.astype(o_ref.dtype)


def fused_matmul(x, w, *, seq, mod=None, bias=None, resln=None, alpha=1.0, out_dtype=F32, tm=512, tn=None, name="mm"):
    T, K = x.shape
    N = w.shape[1]
    tn = N if tn is None else tn
    assert T % tm == 0 and seq % tm == 0 and N % tn == 0
    nb = seq // tm
    args, specs = [x], [pl.BlockSpec((tm, K), lambda i, j: (i, 0))]
    if mod is not None:
        for m in mod:
            args.append(m)
            specs.append(pl.BlockSpec((1, 1, K), lambda i, j: (i // nb, 0, 0)))
    args.append(w)
    specs.append(pl.BlockSpec((K, tn), lambda i, j: (0, j)))
    if bias is not None:
        args.append(bias.reshape(1, N))
        specs.append(pl.BlockSpec((1, tn), lambda i, j: (0, j)))
    epi = "none"
    if resln is not None:
        assert tn == N
        res, gate, g, b = resln
        epi = "resln"
        args += [res, gate, g.reshape(1, N), b.reshape(1, N)]
        specs += [pl.BlockSpec((tm, N), lambda i, j: (i, 0)),
                  pl.BlockSpec((1, 1, N), lambda i, j: (i // nb, 0, 0)),
                  pl.BlockSpec((1, N), lambda i, j: (0, 0)),
                  pl.BlockSpec((1, N), lambda i, j: (0, 0))]
    kern = functools.partial(_mm_kernel, has_mod=mod is not None, has_bias=bias is not None, epi=epi, alpha=alpha)
    return pl.pallas_call(
        kern,
        grid=(T // tm, N // tn),
        in_specs=specs,
        out_specs=pl.BlockSpec((tm, tn), lambda i, j: (i, j)),
        out_shape=jax.ShapeDtypeStruct((T, N), out_dtype),
        compiler_params=_params(("parallel", "parallel")),
        name=name,
    )(*args)


def _ret_proj_kernel(x_ref, sc_ref, sh_ref, w_ref, cos_ref, sin_ref, o_ref, *, dk, hdk, tn, kscale):
    h = (x_ref[...] * (1.0 + sc_ref[0]) + sh_ref[0]).astype(BF16)
    cos, sin = cos_ref[...], sin_ref[...]
    half = dk // 2
    for j in range(o_ref.shape[1] // tn):
        c0 = j * tn
        acc = _dot(h, w_ref[:, c0:c0 + tn])
        if c0 < 2 * hdk:
            scale = 1.0 if c0 < hdk else kscale
            for hh in range(tn // dk):
                x1 = acc[:, hh * dk:hh * dk + half]
                x2 = acc[:, hh * dk + half:(hh + 1) * dk]
                o_ref[:, c0 + hh * dk:c0 + hh * dk + half] = ((x1 * cos - x2 * sin) * scale).astype(o_ref.dtype)
                o_ref[:, c0 + hh * dk + half:c0 + (hh + 1) * dk] = ((x1 * sin + x2 * cos) * scale).astype(o_ref.dtype)
        elif c0 < 4 * hdk:
            o_ref[:, c0:c0 + tn] = acc.astype(o_ref.dtype)
        else:
            o_ref[:, c0:c0 + tn] = (acc * _sigmoid(acc)).astype(o_ref.dtype)


def ret_projection(x, sc, sh, w_in, cos, sin, *, seq, tm=512, tn=1024):
    T, K = x.shape
    N = w_in.shape[1]
    hdk = N // 6
    dk = hdk // RET_H
    assert hdk % tn == 0 and tn % dk == 0
    nb = seq // tm
    kern = functools.partial(_ret_proj_kernel, dk=dk, hdk=hdk, tn=tn, kscale=dk ** -0.5)
    return pl.pallas_call(
        kern,
        grid=(T // tm,),
        in_specs=[pl.BlockSpec((tm, K), lambda i: (i, 0)),
                  pl.BlockSpec((1, 1, K), lambda i: (i // nb, 0, 0)),
                  pl.BlockSpec((1, 1, K), lambda i: (i // nb, 0, 0)),
                  pl.BlockSpec((K, N), lambda i: (0, 0), pipeline_mode=pl.Buffered(1)),
                  pl.BlockSpec((tm, dk // 2), lambda i: (i, 0)),
                  pl.BlockSpec((tm, dk // 2), lambda i: (i, 0))],
        out_specs=pl.BlockSpec((tm, N), lambda i: (i, 0)),
        out_shape=jax.ShapeDtypeStruct((T, N), BF16),
        compiler_params=_params(("parallel",)),
        name="ret_projection",
    )(x, sc, sh, w_in, cos, sin)


def _retention_kernel(q_ref, k_ref, v_ref, sg_ref, din_ref, dq_ref, dkk_ref, dch_ref, gg_ref, gb_ref,
                      o_ref, state_ref, *, H, C, dk, dv):
    @pl.when(pl.program_id(1) == 0)
    def _():
        state_ref[...] = jnp.zeros_like(state_ref)

    n_chunks = q_ref.shape[0] // C
    for c in range(n_chunks):
        rows = slice(c * C, (c + 1) * C)
        for hh in range(H):
            qc = q_ref[rows, hh * dk:(hh + 1) * dk]
            kc = k_ref[rows, hh * dk:(hh + 1) * dk]
            vc = v_ref[rows, hh * dv:(hh + 1) * dv]
            st = state_ref[hh]
            a = _dot_nt(qc, kc) * din_ref[hh]
            o = _dot(a.astype(BF16), vc)
            qd = (qc.astype(F32) * dq_ref[hh]).astype(BF16)
            o = o + _dot(qd, st.astype(BF16))
            kd = (kc.astype(F32) * dkk_ref[hh]).astype(BF16)
            state_ref[hh] = st * dch_ref[hh, :, 0:1] + _dot_tn(kd, vc)
            mu = jnp.mean(o, axis=-1, keepdims=True)
            d = o - mu
            var = jnp.mean(d * d, axis=-1, keepdims=True)
            on = d * lax.rsqrt(var + EPS) * gg_ref[:, hh * dv:(hh + 1) * dv] + gb_ref[:, hh * dv:(hh + 1) * dv]
            sg = sg_ref[rows, hh * dv:(hh + 1) * dv].astype(F32)
            o_ref[rows, hh * dv:(hh + 1) * dv] = (sg * on).astype(o_ref.dtype)


def retention_core(p, gn_g, gn_b, *, batch, seq, tb=512):
    T, N = p.shape
    H, C = RET_H, RET_C
    hdk = N // 6
    dk, dv = hdk // H, 2 * hdk // H
    hdv = H * dv
    nb = seq // tb
    log_gamma = jnp.log1p(-(2.0 ** (-5.0 - jnp.arange(H, dtype=F32))))
    i = jnp.arange(C, dtype=F32)
    rel = i[:, None] - i[None, :]
    d_intra = jnp.where(rel >= 0, jnp.exp(log_gamma[:, None, None] * jnp.maximum(rel, 0.0)), 0.0).astype(F32)
    d_q = jnp.exp(log_gamma[:, None] * (i + 1.0)).astype(F32)[..., None]
    d_k = jnp.exp(log_gamma[:, None] * (C - 1.0 - i)).astype(F32)[..., None]
    d_ch = jnp.broadcast_to(jnp.exp(log_gamma * C).astype(F32)[:, None, None], (H, 1, LANES))
    kern = functools.partial(_retention_kernel, H=H, C=C, dk=dk, dv=dv)
    return pl.pallas_call(
        kern,
        grid=(batch, nb),
        in_specs=[pl.BlockSpec((tb, hdk), lambda b, s: (b * nb + s, 0)),
                  pl.BlockSpec((tb, hdk), lambda b, s: (b * nb + s, 1)),
                  pl.BlockSpec((tb, hdv), lambda b, s: (b * nb + s, 1)),
                  pl.BlockSpec((tb, hdv), lambda b, s: (b * nb + s, 2)),
                  pl.BlockSpec((H, C, C), lambda b, s: (0, 0, 0)),
                  pl.BlockSpec((H, C, 1), lambda b, s: (0, 0, 0)),
                  pl.BlockSpec((H, C, 1), lambda b, s: (0, 0, 0)),
                  pl.BlockSpec((H, 1, LANES), lambda b, s: (0, 0, 0)),
                  pl.BlockSpec((1, hdv), lambda b, s: (0, 0)),
                  pl.BlockSpec((1, hdv), lambda b, s: (0, 0))],
        out_specs=pl.BlockSpec((tb, hdv), lambda b, s: (b * nb + s, 0)),
        out_shape=jax.ShapeDtypeStruct((T, hdv), BF16),
        scratch_shapes=[pltpu.VMEM((H, dk, dv), F32)],
        compiler_params=_params(("parallel", "arbitrary")),
        name="retention_core",
    )(p, p, p, p, d_intra, d_q, d_k, d_ch, gn_g.reshape(1, hdv), gn_b.reshape(1, hdv))


def _glu_kernel(x_ref, sc_ref, sh_ref, wa_ref, wg_ref, ba_ref, bg_ref, o_ref):
    h = (x_ref[...] * (1.0 + sc_ref[0]) + sh_ref[0]).astype(BF16)
    a = _dot(h, wa_ref[...]) + ba_ref[...]
    g = _dot(h, wg_ref[...]) + bg_ref[...]
    o_ref[...] = a * _sigmoid(g)


def conv_glu(x, sc, sh, w_pw1, b_pw1, *, seq, tm=512, tn=512):
    T, K = x.shape
    N = w_pw1.shape[1] // 2
    nb = seq // tm
    ng = N // tn
    b2 = b_pw1.reshape(1, 2 * N)
    return pl.pallas_call(
        _glu_kernel,
        grid=(T // tm, ng),
        in_specs=[pl.BlockSpec((tm, K), lambda i, j: (i, 0)),
                  pl.BlockSpec((1, 1, K), lambda i, j: (i // nb, 0, 0)),
                  pl.BlockSpec((1, 1, K), lambda i, j: (i // nb, 0, 0)),
                  pl.BlockSpec((K, tn), lambda i, j: (0, j)),
                  pl.BlockSpec((K, tn), lambda i, j: (0, j + ng)),
                  pl.BlockSpec((1, tn), lambda i, j: (0, j)),
                  pl.BlockSpec((1, tn), lambda i, j: (0, j + ng))],
        out_specs=pl.BlockSpec((tm, tn), lambda i, j: (i, j)),
        out_shape=jax.ShapeDtypeStruct((T, N), F32),
        compiler_params=_params(("parallel", "parallel")),
        name="conv_glu",
    )(x, sc, sh, w_pw1, w_pw1, b2, b2)


def _conv_tail_kernel(u_ref, up_ref, wdw_ref, bdw_ref, lg_ref, lb_ref, w2_ref, b2_ref,
                      res_ref, gate_ref, g_ref, bb_ref, o_ref, win_ref, shift_ref, *, nb, halo, width, alpha):
    tm = u_ref.shape[0]
    first = (pl.program_id(0) % nb) == 0
    prev = up_ref[...]
    win_ref[0:halo, :] = jnp.where(first, jnp.zeros_like(prev), prev)
    win_ref[halo:halo + tm, :] = u_ref[...]
    for b in range(1, 8):
        n = (tm + halo - b) // 8 * 8
        shift_ref[b - 1, 0:n, :] = win_ref[b:b + n, :]
    acc = jnp.zeros(u_ref.shape, F32)
    off = halo - (width - 1)
    for k in range(width):
        b, a = (off + k) % 8, (off + k) // 8 * 8
        rows = win_ref[a:a + tm, :] if b == 0 else shift_ref[b - 1, a:a + tm, :]
        acc = acc + wdw_ref[k:k + 1, :] * rows
    acc = acc + bdw_ref[...]
    v = _ln_rows(acc, lg_ref[...], lb_ref[...])
    v = v * _sigmoid(v)
    y = _dot(v.astype(BF16), w2_ref[...]) + b2_ref[...]
    z = alpha * res_ref[...] + (1.0 + gate_ref[0]) * y
    o_ref[...] = _ln_rows(z, g_ref[...], bb_ref[...])


def conv_tail(u, w_dw, b_dw, ln_g, ln_b, w_pw2, b_pw2, res, gate, g, b, *, seq, alpha, tm=512):
    T, D = u.shape
    nb = seq // tm
    halo = 32
    assert CONV_W - 1 <= halo and tm % halo == 0
    r = tm // halo
    wpad = jnp.zeros((halo, D), F32).at[:CONV_W].set(w_dw)
    kern = functools.partial(_conv_tail_kernel, nb=nb, halo=halo, width=CONV_W, alpha=alpha)
    vec = lambda a: a.reshape(1, D)
    one = pl.BlockSpec((1, D), lambda i: (0, 0))
    return pl.pallas_call(
        kern,
        grid=(T // tm,),
        in_specs=[pl.BlockSpec((tm, D), lambda i: (i, 0)),
                  pl.BlockSpec((halo, D), lambda i: (jnp.maximum(i * r - 1, 0), 0)),
                  pl.BlockSpec((halo, D), lambda i: (0, 0)),
                  one, one, one,
                  pl.BlockSpec((D, D), lambda i: (0, 0)),
                  one,
                  pl.BlockSpec((tm, D), lambda i: (i, 0)),
                  pl.BlockSpec((1, 1, D), lambda i: (i // nb, 0, 0)),
                  one, one],
        out_specs=pl.BlockSpec((tm, D), lambda i: (i, 0)),
        out_shape=jax.ShapeDtypeStruct((T, D), F32),
        scratch_shapes=[pltpu.VMEM((halo + tm, D), F32), pltpu.VMEM((7, halo + tm, D), F32)],
        compiler_params=_params(("parallel",)),
        name="conv_tail",
    )(u, u, wpad, vec(b_dw), vec(ln_g), vec(ln_b), w_pw2, vec(b_pw2), res, gate, vec(g), vec(b))


def _pool_kernel(x_ref, xp_ref, sc_ref, sh_ref, w_ref, pb_ref, ps_ref, gate_ref, g_ref, bb_ref,
                 o_ref, win_ref, *, nb, halo, wins, alpha):
    tm, D = x_ref.shape
    gw = D // len(wins)
    i = pl.program_id(0)
    first = (i % nb) == 0
    x = x_ref[...]
    sc, sh = sc_ref[0], sh_ref[0]
    hp = xp_ref[...] * (1.0 + sc) + sh
    win_ref[0:halo, :] = jnp.where(first, jnp.zeros_like(hp), hp)
    win_ref[halo:halo + tm, :] = x * (1.0 + sc) + sh
    t = (i % nb) * tm + lax.broadcasted_iota(I32, (tm, gw), 0)
    ys = []
    for gi, wn in enumerate(wins):
        cols = slice(gi * gw, (gi + 1) * gw)
        s = win_ref[halo:halo + tm, cols]
        hcur = s
        for k in range(1, wn):
            s = s + win_ref[halo - k:halo - k + tm, cols]
        cnt = jnp.minimum(t + 1, wn).astype(F32)
        pooled = s / cnt - hcur
        ys.append(_dot(pooled.astype(BF16), w_ref[gi]))
    y = (jnp.concatenate(ys, axis=-1) + pb_ref[...]) * ps_ref[...]
    z = alpha * x + (1.0 + gate_ref[0]) * y
    o_ref[...] = _ln_rows(z, g_ref[...], bb_ref[...])


def pool_layer(x, sc, sh, pool_w, pool_b, pool_scale, gate, g, b, *, seq, alpha, tm=512):
    T, D = x.shape
    nb = seq // tm
    halo = 16
    assert max(POOL_WINS) <= halo and tm % halo == 0
    r = tm // halo
    G, gw, _ = pool_w.shape
    kern = functools.partial(_pool_kernel, nb=nb, halo=halo, wins=POOL_WINS, alpha=alpha)
    vec = lambda a: a.reshape(1, D)
    one = pl.BlockSpec((1, D), lambda i: (0, 0))
    per_b = pl.BlockSpec((1, 1, D), lambda i: (i // nb, 0, 0))
    return pl.pallas_call(
        kern,
        grid=(T // tm,),
        in_specs=[pl.BlockSpec((tm, D), lambda i: (i, 0)),
                  pl.BlockSpec((halo, D), lambda i: (jnp.maximum(i * r - 1, 0), 0)),
                  per_b, per_b,
                  pl.BlockSpec((G, gw, gw), lambda i: (0, 0, 0)),
                  one, one, per_b, one, one],
        out_specs=pl.BlockSpec((tm, D), lambda i: (i, 0)),
        out_shape=jax.ShapeDtypeStruct((T, D), F32),
        scratch_shapes=[pltpu.VMEM((halo + tm, D), F32)],
        compiler_params=_params(("parallel",)),
        name="pool_layer",
    )(x, x, sc, sh, pool_w.astype(BF16), vec(pool_b), vec(pool_scale), gate, vec(g), vec(b))


def _rope64(x, cos, sin_signed, lane):
    partner = jnp.where((lane & (NSA_DH - 1)) < NSA_DH // 2, pltpu.roll(x, LANES - NSA_DH // 2, 1), pltpu.roll(x, NSA_DH // 2, 1))
    return x * cos + partner * sin_signed


def _nsa_proj_kernel(x_ref, sc_ref, sh_ref, w_ref, cos_ref, sin_ref, gb_ref, q_ref, cv_ref, kvx_ref, gt_ref, *, nq, qscale):
    h = x_ref[...] * (1.0 + sc_ref[0]) + sh_ref[0]
    acc = _dot(h.astype(BF16), w_ref[...])
    cos, sin = cos_ref[...], sin_ref[...]
    lane = lax.broadcasted_iota(I32, cos.shape, 1)
    for s in range(nq):
        blk = acc[:, s * LANES:(s + 1) * LANES]
        q_ref[:, s * LANES:(s + 1) * LANES] = (_rope64(blk, cos, sin, lane) * qscale).astype(q_ref.dtype)
    low = lane < NSA_DH
    for s in range(6):
        blk = acc[:, (nq + s) * LANES:(nq + s + 1) * LANES]
        if s % 2 == 0:
            blk = _rope64(blk, cos, sin, lane)
        if s < 2:
            cv_ref[:, s * LANES:(s + 1) * LANES] = blk.astype(cv_ref.dtype)
            continue
        swapped = pltpu.roll(blk, NSA_DH, 1)
        for gi in range(2):
            own, other = (blk, swapped) if gi == 0 else (swapped, blk)
            if s % 2 == 0:
                kvx_ref[3 * (s // 2 - 1), 0, gi] = jnp.where(low, own, other).astype(kvx_ref.dtype)
            else:
                v_lo = jnp.where(lane == NSA_DH, 1.0, jnp.where(low, own, 0.0))
                v_hi = jnp.where(lane == 0, 1.0, jnp.where(low, 0.0, other))
                kvx_ref[3 * (s // 2 - 1) + 1, 0, gi] = v_lo.astype(kvx_ref.dtype)
                kvx_ref[3 * (s // 2 - 1) + 2, 0, gi] = v_hi.astype(kvx_ref.dtype)
    for gi in range(gt_ref.shape[1] // LANES):
        gl = acc[:, (nq + 6 + gi) * LANES:(nq + 7 + gi) * LANES]
        gt_ref[:, gi * LANES:(gi + 1) * LANES] = _sigmoid(gl + gb_ref[:, gi * LANES:(gi + 1) * LANES])


def nsa_projection(x, sc, sh, w_in_p, cos, sin_signed, gate_b_p, *, seq, tm=512):
    T, K = x.shape
    Np = w_in_p.shape[1]
    nq = NSA_H * NSA_DH // LANES
    assert NSA_G * NSA_DH == LANES
    ng = NSA_G * LANES
    nb = seq // tm
    kern = functools.partial(_nsa_proj_kernel, nq=nq, qscale=NSA_DH ** -0.5)
    per_b = pl.BlockSpec((1, 1, K), lambda i: (i // nb, 0, 0))
    return pl.pallas_call(
        kern,
        grid=(T // tm,),
        in_specs=[pl.BlockSpec((tm, K), lambda i: (i, 0)), per_b, per_b,
                  pl.BlockSpec((K, Np), lambda i: (0, 0)),
                  pl.BlockSpec((tm, LANES), lambda i: (i, 0)),
                  pl.BlockSpec((tm, LANES), lambda i: (i, 0)),
                  pl.BlockSpec((1, ng), lambda i: (0, 0))],
        out_specs=[pl.BlockSpec((tm, nq * LANES), lambda i: (i, 0)),
                   pl.BlockSpec((tm, 2 * LANES), lambda i: (i, 0)),
                   pl.BlockSpec((6, 1, NSA_G, tm, LANES), lambda i: (0, i // nb, 0, i % nb, 0)),
                   pl.BlockSpec((tm, ng), lambda i: (i, 0))],
        out_shape=[jax.ShapeDtypeStruct((T, nq * LANES), BF16),
                   jax.ShapeDtypeStruct((T, 2 * LANES), BF16),
                   jax.ShapeDtypeStruct((6, T // seq, NSA_G, seq, LANES), BF16),
                   jax.ShapeDtypeStruct((T, ng), F32)],
        compiler_params=_params(("parallel",)),
        name="nsa_projection",
    )(x, sc, sh, w_in_p, cos, sin_signed, gate_b_p)


def _gelu_tanh(v):
    return 0.5 * v * (1.0 + jnp.tanh(math.sqrt(2.0 / math.pi) * (v + 0.044715 * v * v * v)))


def _compress_kernel(xk_ref, xv_ref, pk_ref, pv_ref, kw1_ref, kw2_ref, vw1_ref, vw2_ref, ok_ref, ov_ref):
    def one(x_ref, pe_ref, w1_ref, w2_ref, o_ref):
        x = x_ref[0].astype(F32)
        n = x.shape[0]
        half = x.shape[1]
        a = _dot((x + pe_ref[0:1, :]).astype(BF16), w1_ref[0:half, :])
        b = _dot((x + pe_ref[1:2, :]).astype(BF16), w1_ref[half:2 * half, :])
        pre = a + pltpu.roll(b, n - 1, 0)
        o_ref[0] = _dot(_gelu_tanh(pre).astype(BF16), w2_ref[...]).astype(o_ref.dtype)

    one(xk_ref, pk_ref, kw1_ref, kw2_ref, ok_ref)
    one(xv_ref, pv_ref, vw1_ref, vw2_ref, ov_ref)


def nsa_compress(kc_chunks, vc_chunks, pos_k, pos_v, k_w1, k_w2, v_w1, v_w2):
    BG, n, width = kc_chunks.shape
    hid = k_w1.shape[1]
    dh = k_w2.shape[1]
    full = lambda a: pl.BlockSpec(a.shape, lambda i: (0,) * a.ndim)
    pk = pos_k.reshape(2, width)
    pv = pos_v.reshape(2, width)
    ws = [k_w1.astype(BF16), k_w2.astype(BF16), v_w1.astype(BF16), v_w2.astype(BF16)]
    del hid
    return pl.pallas_call(
        _compress_kernel,
        grid=(BG,),
        in_specs=[pl.BlockSpec((1, n, width), lambda i: (i, 0, 0)),
                  pl.BlockSpec((1, n, width), lambda i: (i, 0, 0)),
                  full(pk), full(pv)] + [full(w) for w in ws],
        out_specs=[pl.BlockSpec((1, n, dh), lambda i: (i, 0, 0)),
                   pl.BlockSpec((1, n, dh), lambda i: (i, 0, 0))],
        out_shape=[jax.ShapeDtypeStruct((BG, n, dh), BF16), jax.ShapeDtypeStruct((BG, n, dh), BF16)],
        compiler_params=_params(("parallel",)),
        name="nsa_compress",
    )(kc_chunks, vc_chunks, pk, pv, *ws)


def _nsa_attn_kernel(q_ref, gt_ref, kc_ref, vcl_ref, vch_ref, ks_ref, vsl_ref, vsh_ref, kw_ref, vwl_ref, vwh_ref,
                     ov_ref, o_ref, qs_ref, m_ref, ae_ref, ao_ref, res_ref, sc_ref,
                     *, qt, ck, n_sel, top_n, sel_shift, cmp_len, cmp_step, window):
    t0 = pl.program_id(2) * qt
    R = q_ref.shape[2] // NSA_DH
    rows = R * qt
    half_rows = rows // 2
    lane = lax.broadcasted_iota(I32, (qt, LANES), 1)

    for p in range(R // 2):
        slab = q_ref[0, :, p * LANES:(p + 1) * LANES]
        qs_ref[p * qt:(p + 1) * qt, :] = jnp.where(lane < NSA_DH, slab, jnp.zeros_like(slab))
        qs_ref[half_rows + p * qt:half_rows + (p + 1) * qt, :] = jnp.where(lane >= NSA_DH, slab, jnp.zeros_like(slab))
    qs = qs_ref[...]
    tq = t0 + lax.broadcasted_iota(I32, (qt, 1), 0)

    ncmp = kc_ref.shape[0]
    cmp_end = lax.broadcasted_iota(I32, (qt, ncmp), 1) * cmp_step + (cmp_len - 1)
    s3 = _dot_nt(qs, kc_ref[...]).reshape(R, qt, ncmp) + jnp.where(cmp_end <= tq, 0.0, NEG_BIG)
    e3 = jnp.exp(s3 - jnp.max(s3, axis=-1, keepdims=True))
    row_ok = jnp.where(tq >= cmp_len - 1, 1.0, 0.0)
    pc3 = e3 * (row_ok / jnp.sum(e3, axis=-1, keepdims=True))
    pcb = pc3.reshape(rows, ncmp).astype(BF16)
    res_ref[0, 0:half_rows, :] = _dot(pcb[0:half_rows], vcl_ref[...])
    res_ref[0, half_rows:rows, :] = _dot(pcb[half_rows:], vch_ref[...])
    psum = jnp.sum(pc3, axis=0)
    imp = _dot(psum.astype(BF16), ov_ref[...])

    cur = lax.shift_right_logical(tq, sel_shift)
    dcur = cur - lane
    forced = (lane == 0) | (dcur == 0) | (dcur == 1)
    score = jnp.where(lax.shift_left(lane, sel_shift) <= tq, jnp.where(forced, NSA_FORCED, imp), NEG_BIG)
    score = jnp.where(lane < n_sel, score, -jnp.inf)
    st = score.T
    blk_f = lax.broadcasted_iota(I32, st.shape, 0).astype(F32)
    sel_t = jnp.zeros(st.shape, F32)
    for _ in range(top_n):
        mx = jnp.max(st, axis=0, keepdims=True)
        idx = jnp.min(jnp.where(st == mx, blk_f, float(LANES)), axis=0, keepdims=True)
        hit = blk_f == idx
        sel_t = jnp.where(hit, jnp.where(mx > 0.5 * NEG_BIG, 1.0, sel_t), sel_t)
        st = jnp.where(hit, -jnp.inf, st)
    selb = sel_t.T.astype(BF16)

    def attend(k_ref, vl_ref, vh_ref, c_lo, c_hi, bias_fn, slot):
        m_ref[...] = jnp.full(m_ref.shape, NEG_BIG, F32)
        ae_ref[...] = jnp.zeros(ae_ref.shape, F32)
        ao_ref[...] = jnp.zeros(ao_ref.shape, F32)

        def scores(c):
            start = pl.multiple_of(c * ck, ck)
            return _dot_nt(qs, k_ref[pl.ds(start, ck), :]).reshape(R, qt, ck) + bias_fn(start)

        sc_ref[...] = scores(c_lo)

        def body(c, carry):
            start = pl.multiple_of(c * ck, ck)
            sc_next = scores(jnp.minimum(c + 1, c_hi - 1))
            sc3 = sc_ref[...]
            m_old = m_ref[...]
            m_new = jnp.maximum(m_old, jnp.max(sc3, axis=-1, keepdims=True).reshape(rows, 1))
            alpha = jnp.exp(m_old - m_new)
            p3 = jnp.exp(sc3 - m_new.reshape(R, qt, 1))
            m_ref[...] = m_new
            pb = p3.reshape(rows, ck).astype(BF16)
            ae_ref[...] = alpha[0:half_rows] * ae_ref[...] + _dot(pb[0:half_rows], vl_ref[pl.ds(start, ck), :])
            ao_ref[...] = alpha[half_rows:] * ao_ref[...] + _dot(pb[half_rows:], vh_ref[pl.ds(start, ck), :])
            sc_ref[...] = sc_next
            return carry

        lax.fori_loop(c_lo, c_hi, body, 0)
        lane_h = lax.broadcasted_iota(I32, (half_rows, LANES), 1)
        ae, ao = ae_ref[...], ao_ref[...]
        res_ref[slot, 0:half_rows, :] = jnp.where(lane_h < NSA_DH, ae / ae[:, NSA_DH:NSA_DH + 1], 0.0)
        res_ref[slot, half_rows:rows, :] = jnp.where(lane_h >= NSA_DH, ao / ao[:, 0:1], 0.0)

    def key_dist(start):
        return tq - (start + lax.broadcasted_iota(I32, (qt, ck), 1))

    def sel_bias(start):
        jrow = lax.broadcasted_iota(I32, (LANES, ck), 0)
        jcol = lax.shift_right_logical(start + lax.broadcasted_iota(I32, (LANES, ck), 1), sel_shift)
        expand = jnp.where(jrow == jcol, 1.0, 0.0).astype(BF16)
        picked = _dot(selb, expand)
        return jnp.where(key_dist(start) >= 0, jnp.where(picked > 0.5, 0.0, NEG_BIG), NEG_BIG)

    def win_bias(start):
        dist = key_dist(start)
        return jnp.where(jnp.where(dist >= 0, dist, window) < window, 0.0, NEG_BIG)

    c_hi = (t0 + qt + ck - 1) // ck
    attend(ks_ref, vsl_ref, vsh_ref, 0, c_hi, sel_bias, 1)
    attend(kw_ref, vwl_ref, vwh_ref, jnp.maximum(t0 - window + 1, 0) // ck, c_hi, win_bias, 2)

    gt = gt_ref[...]
    for p in range(R // 2):
        acc = jnp.zeros((qt, LANES), F32)
        for par in range(2):
            rsl = slice(par * half_rows + p * qt, par * half_rows + (p + 1) * qt)
            for br in range(3):
                col = (2 * p + par) * 3 + br
                acc = acc + gt[:, col:col + 1] * res_ref[br, rsl, :]
        o_ref[0, :, p * LANES:(p + 1) * LANES] = acc.astype(o_ref.dtype)


def nsa_attention(q, gates, kcd, vcl, vch, kvx, *, batch, seq, qt=256, ck=512):
    G = NSA_G
    R = NSA_H // G
    gq = R * NSA_DH
    n_sel = seq // NSA_SEL_LEN
    ncmp = kcd.shape[1]
    nq = seq // qt
    ov = np.zeros((ncmp, LANES), np.float32)
    for n in range(ncmp):
        for j in range(n_sel):
            if n * NSA_CMP_STEP < (j + 1) * NSA_SEL_LEN and n * NSA_CMP_STEP + NSA_CMP_LEN > j * NSA_SEL_LEN:
                ov[n, j] = 1.0
    n_valid = (seq - NSA_CMP_LEN) // NSA_CMP_STEP + 1
    ov[n_valid:] = 0.0
    kern = functools.partial(_nsa_attn_kernel, qt=qt, ck=ck, n_sel=n_sel, top_n=min(NSA_TOPN, n_sel),
                             sel_shift=int(math.log2(NSA_SEL_LEN)), cmp_len=NSA_CMP_LEN, cmp_step=NSA_CMP_STEP, window=NSA_WIN)
    cmp_spec = pl.BlockSpec((None, ncmp, LANES), lambda b, g, i: (b * G + g, 0, 0))
    seq_specs = [pl.BlockSpec((None, None, None, seq, LANES), functools.partial(lambda b, g, i, k: (k, b, g, 0, 0), k=k))
                 for k in range(6)]
    rows = R * qt
    return pl.pallas_call(
        kern,
        grid=(batch, G, nq),
        in_specs=[pl.BlockSpec((1, qt, gq), lambda b, g, i: (b, i, g)),
                  pl.BlockSpec((qt, LANES), lambda b, g, i: (b * nq + i, g)),
                  cmp_spec, cmp_spec, cmp_spec] + seq_specs + [
                  pl.BlockSpec((ncmp, LANES), lambda b, g, i: (0, 0))],
        out_specs=pl.BlockSpec((1, qt, gq), lambda b, g, i: (b, i, g)),
        out_shape=jax.ShapeDtypeStruct((batch, seq, NSA_H * NSA_DH), BF16),
        scratch_shapes=[pltpu.VMEM((rows, LANES), BF16),
                        pltpu.VMEM((rows, 1), F32),
                        pltpu.VMEM((rows // 2, LANES), F32), pltpu.VMEM((rows // 2, LANES), F32),
                        pltpu.VMEM((3, rows, LANES), F32),
                        pltpu.VMEM((R, qt, ck), F32)],
        compiler_params=_params(("parallel", "parallel", "arbitrary")),
        name="nsa_attention",
    )(q, gates, kcd, vcl, vch, kvx, kvx, kvx, kvx, kvx, kvx, jnp.asarray(ov, BF16))


def _tile_rows(ref, idx, ns):
    return ref.at[pl.ds(pl.multiple_of(idx * ns, ns), ns)]


def _tile_rows_from(ref, row0, ns):
    return ref.at[pl.ds(pl.multiple_of(row0, ns), ns)]


def _router_kernel(x_ref, sc_ref, sh_ref, w_ref, b_ref, h3_ref, meta_ref, gate_ref, cnt_ref, seen_ref, *, top_k):
    @pl.when(pl.program_id(0) == 0)
    def _():
        seen_ref[...] = jnp.zeros_like(seen_ref)

    h = x_ref[...] * (1.0 + sc_ref[0]) + sh_ref[0]
    tm = h.shape[0]
    ns = h.shape[1] // LANES
    for s in range(ns):
        h3_ref[pl.ds(s, tm, stride=ns), :] = h[:, s * LANES:(s + 1) * LANES]
    logits = _dot(h.astype(BF16), w_ref[...]) + b_ref[...]
    lane = lax.broadcasted_iota(I32, logits.shape, 1)
    lane_f = lane.astype(F32)
    val_out = jnp.full(logits.shape, NEG_BIG, F32)
    chosen = jnp.zeros(logits.shape, F32)
    picks = []
    for k in range(top_k):
        mx = jnp.max(logits, axis=-1, keepdims=True)
        idx = jnp.min(jnp.where(logits == mx, lane_f, float(LANES)), axis=-1, keepdims=True)
        hit = lane_f == idx
        picks.append((idx, hit))
        chosen = jnp.where(hit, 1.0, chosen)
        val_out = jnp.where(lane == k, mx, val_out)
        logits = jnp.where(hit, -jnp.inf, logits)
    e = jnp.exp(val_out - jnp.max(val_out, axis=-1, keepdims=True))
    e = jnp.where(lane < top_k, e, 0.0)
    gate_ref[...] = e / jnp.sum(e, axis=-1, keepdims=True)
    lower = jnp.where(lax.broadcasted_iota(I32, (tm, tm), 0) > lax.broadcasted_iota(I32, (tm, tm), 1), 1.0, 0.0)
    before = _dot(lower.astype(BF16), chosen.astype(BF16)) + seen_ref[...]
    meta = jnp.zeros(logits.shape, I32)
    for k, (idx, hit) in enumerate(picks):
        rank = jnp.sum(jnp.where(hit, before, 0.0), axis=-1, keepdims=True)
        meta = jnp.where(lane == k, idx.astype(I32), meta)
        meta = jnp.where(lane == top_k + k, rank.astype(I32), meta)
    meta_ref[...] = meta
    seen_ref[...] = seen_ref[...] + jnp.sum(chosen, axis=0, keepdims=True)
    cnt_ref[...] = seen_ref[...].astype(I32)


def moe_router(x, sc, sh, router_w, router_b, *, seq, tm=512):
    T, D = x.shape
    E = router_w.shape[1]
    nb = seq // tm
    ns = D // LANES
    assert ns == 8
    wp = jnp.zeros((D, LANES), BF16).at[:, :E].set(router_w.astype(BF16))
    bp = jnp.full((1, LANES), NEG_BIG, F32).at[0, :E].set(router_b)
    per_b = pl.BlockSpec((1, 1, D), lambda i: (i // nb, 0, 0))
    return pl.pallas_call(
        functools.partial(_router_kernel, top_k=MOE_K),
        grid=(T // tm,),
        in_specs=[pl.BlockSpec((tm, D), lambda i: (i, 0)), per_b, per_b,
                  pl.BlockSpec((D, LANES), lambda i: (0, 0)),
                  pl.BlockSpec((1, LANES), lambda i: (0, 0))],
        out_specs=[pl.BlockSpec((tm * ns, LANES), lambda i: (i, 0)),
                   pl.BlockSpec((tm, LANES), lambda i: (i, 0)),
                   pl.BlockSpec((tm, LANES), lambda i: (i, 0)),
                   pl.BlockSpec((1, LANES), lambda i: (0, 0))],
        out_shape=[jax.ShapeDtypeStruct((T * ns, LANES), F32),
                   jax.ShapeDtypeStruct((T, LANES), I32),
                   jax.ShapeDtypeStruct((T, LANES), F32),
                   jax.ShapeDtypeStruct((1, LANES), I32)],
        scratch_shapes=[pltpu.VMEM((1, LANES), F32)],
        compiler_params=_params(("arbitrary",)),
        name="moe_router",
    )(x, sc, sh, wp, bp)


def _dispatch_kernel(fill_ref, slots_ref, slots1_ref, slots2_ref, h3_ref, xs3_ref, zero_ref, hbuf_ref, fill_sem,
                     load_sems, row_sems, *, tm, top_k, n_exp, ns):
    i = pl.program_id(0)
    n = pl.num_programs(0)
    nbuf = hbuf_ref.shape[0]

    def load(step):
        src = h3_ref.at[pl.ds(pl.multiple_of(step * (tm * ns), tm * ns), tm * ns)]
        return pltpu.make_async_copy(src, hbuf_ref.at[step % nbuf], load_sems.at[step % nbuf])

    def row_copy(step, slots, r, k):
        return pltpu.make_async_copy(_tile_rows(hbuf_ref.at[step % nbuf], r, ns),
                                     _tile_rows_from(xs3_ref, slots[r * top_k + k], ns), row_sems.at[step % nbuf])

    def issue_rows(step, slots):
        def body(r, carry):
            for k in range(top_k):
                row_copy(step, slots, r, k).start(priority=k % 2)
            return carry
        lax.fori_loop(0, tm, body, 0, unroll=8)

    def drain_rows(step, slots):
        def body(r, carry):
            for k in range(top_k):
                row_copy(step, slots, r, k).wait()
            return carry
        lax.fori_loop(0, tm, body, 0, unroll=8)

    @pl.when(i == 0)
    def _():
        load(0).start()
        zero_ref[...] = jnp.zeros_like(zero_ref)

        def fill(e):
            dst = xs3_ref.at[pl.ds(pl.multiple_of(fill_ref[e] * ns, ns), zero_ref.shape[0])]
            return pltpu.make_async_copy(zero_ref, dst, fill_sem)

        for e in range(n_exp):
            fill(e).start()
        for e in range(n_exp):
            fill(e).wait()

    @pl.when(i >= 2)
    def _():
        drain_rows(i - 2, slots2_ref)

    @pl.when(i + 1 < n)
    def _():
        load(i + 1).start()

    load(i).wait()
    issue_rows(i, slots_ref)

    @pl.when(i == n - 1)
    def _():
        @pl.when(i >= 1)
        def _():
            drain_rows(i - 1, slots1_ref)
        drain_rows(i, slots_ref)


def moe_dispatch(h3, slots, fill_start, n_rows, *, tm=256):
    ns = 8
    T = h3.shape[0] // ns
    K = MOE_K
    slot_spec = lambda back: pl.BlockSpec((tm * K,), lambda i, fill: (jnp.maximum(i - back, 0),), memory_space=pltpu.SMEM)
    grid_spec = pltpu.PrefetchScalarGridSpec(
        num_scalar_prefetch=1,
        grid=(T // tm,),
        in_specs=[slot_spec(0), slot_spec(1), slot_spec(2), pl.BlockSpec(memory_space=pl.ANY)],
        out_specs=pl.BlockSpec(memory_space=pl.ANY),
        scratch_shapes=[pltpu.VMEM((MOE_BLK * ns, LANES), F32), pltpu.VMEM((3, tm * ns, LANES), F32),
                        pltpu.SemaphoreType.DMA(()), pltpu.SemaphoreType.DMA((3,)), pltpu.SemaphoreType.DMA((3,))],
    )
    return pl.pallas_call(
        functools.partial(_dispatch_kernel, tm=tm, top_k=K, n_exp=MOE_E, ns=ns),
        grid_spec=grid_spec,
        out_shape=jax.ShapeDtypeStruct((n_rows * ns, LANES), F32),
        compiler_params=_params(("arbitrary",)),
        name="moe_dispatch",
    )(fill_start, slots, slots, slots, h3)


def _expert_kernel(be_ref, nu_ref, nv_ref, x3_ref, w1_ref, b1_ref, w2_ref, b2_ref, o3_ref, w1b_ref, w2b_ref, *, dff, sub):
    i = pl.program_id(0)
    ns = w1b_ref.shape[0] // LANES
    blk = x3_ref.shape[0] // ns

    @pl.when(i < nu_ref[0])
    def _():
        @pl.when(jnp.logical_or(i == 0, be_ref[i] != be_ref[jnp.maximum(i - 1, 0)]))
        def _():
            w1b_ref[...] = w1_ref[0].astype(BF16)
            w2b_ref[...] = w2_ref[0].astype(BF16)

        for part in range(blk // sub):
            @pl.when(nv_ref[i] > part * sub)
            def _():
                r0 = part * sub * ns
                xb = jnp.concatenate([x3_ref[pl.ds(r0 + s, sub, stride=ns), :].astype(BF16) for s in range(ns)], axis=-1)
                gu = _dot(xb, w1b_ref[...]) + b1_ref[0]
                gg = jnp.minimum(gu[:, :dff], GLU_LIMIT)
                up = jnp.clip(gu[:, dff:], -GLU_LIMIT, GLU_LIMIT)
                act = (up + 1.0) * (gg * _sigmoid(GLU_ALPHA * gg))
                y = _dot(act.astype(BF16), w2b_ref[...]) + b2_ref[0]
                for s in range(ns):
                    o3_ref[pl.ds(r0 + s, sub, stride=ns), :] = y[:, s * LANES:(s + 1) * LANES]


def moe_experts(xs3, block_expert, n_used, n_valid, w1, b1, w2, b2, layer):
    L, E, D, two_f = w1.shape
    ns = D // LANES
    nblk = block_expert.shape[0]
    row_map = lambda i, be, nu, nv: (jnp.minimum(i, nu[0] - 1), 0)
    w_map = lambda i, be, nu, nv: (layer, be[i], 0, 0)
    grid_spec = pltpu.PrefetchScalarGridSpec(
        num_scalar_prefetch=3,
        grid=(nblk,),
        in_specs=[pl.BlockSpec((MOE_BLK * ns, LANES), row_map),
                  pl.BlockSpec((None, 1, D, two_f), w_map),
                  pl.BlockSpec((None, 1, 1, two_f), w_map),
                  pl.BlockSpec((None, 1, two_f // 2, D), w_map),
                  pl.BlockSpec((None, 1, 1, D), w_map)],
        out_specs=pl.BlockSpec((MOE_BLK * ns, LANES), row_map),
        scratch_shapes=[pltpu.VMEM((D, two_f), BF16), pltpu.VMEM((two_f // 2, D), BF16)],
    )
    return pl.pallas_call(
        functools.partial(_expert_kernel, dff=two_f // 2, sub=MOE_SUB),
        grid_spec=grid_spec,
        out_shape=jax.ShapeDtypeStruct((nblk * MOE_BLK * ns, LANES), F32),
        compiler_params=_params(("arbitrary",)),
        name="moe_experts",
    )(block_expert, n_used, n_valid, xs3, w1, b1.reshape(L, E, 1, two_f), w2, b2.reshape(L, E, 1, D))


def _combine_kernel(slots_ref, nslots_ref, out3_ref, gt_ref, res_ref, gate_ref, g_ref, bb_ref, o_ref, y4_ref, sems,
                    *, tm, top_k, alpha, ns):
    i = pl.program_id(0)
    buf = i % 2

    def row_copy(slots, b, r, k):
        return pltpu.make_async_copy(_tile_rows_from(out3_ref, slots[r * top_k + k], ns),
                                     _tile_rows(y4_ref.at[b, k], r, ns), sems.at[b])

    def issue(slots, b):
        def body(r, carry):
            for k in range(top_k):
                row_copy(slots, b, r, k).start(priority=k % 2)
            return carry
        lax.fori_loop(0, tm, body, 0, unroll=8)

    def drain(slots, b):
        def body(r, carry):
            for k in range(top_k):
                row_copy(slots, b, r, k).wait()
            return carry
        lax.fori_loop(0, tm, body, 0, unroll=8)

    @pl.when(i == 0)
    def _():
        issue(slots_ref, 0)

    @pl.when(i + 1 < pl.num_programs(0))
    def _():
        issue(nslots_ref, 1 - buf)

    drain(slots_ref, buf)
    gt = gt_ref[...]
    cols = []
    for s in range(ns):
        acc = gt[:, 0:1] * y4_ref[buf, 0, pl.ds(s, tm, stride=ns), :]
        for k in range(1, top_k):
            acc = acc + gt[:, k:k + 1] * y4_ref[buf, k, pl.ds(s, tm, stride=ns), :]
        cols.append(acc)
    y = jnp.concatenate(cols, axis=-1)
    z = alpha * res_ref[...] + (1.0 + gate_ref[0]) * y
    o_ref[...] = _ln_rows(z, g_ref[...], bb_ref[...])


def moe_combine(out3, slots, gates, res, gate, g, b, *, seq, alpha, tm=256):
    T, D = res.shape
    K = MOE_K
    ns = D // LANES
    nb = seq // tm
    nsteps = T // tm
    one = pl.BlockSpec((1, D), lambda i: (0, 0))
    return pl.pallas_call(
        functools.partial(_combine_kernel, tm=tm, top_k=K, alpha=alpha, ns=ns),
        grid=(nsteps,),
        in_specs=[pl.BlockSpec((tm * K,), lambda i: (i,), memory_space=pltpu.SMEM),
                  pl.BlockSpec((tm * K,), lambda i: (jnp.minimum(i + 1, nsteps - 1),), memory_space=pltpu.SMEM),
                  pl.BlockSpec(memory_space=pl.ANY),
                  pl.BlockSpec((tm, LANES), lambda i: (i, 0)),
                  pl.BlockSpec((tm, D), lambda i: (i, 0)),
                  pl.BlockSpec((1, 1, D), lambda i: (i // nb, 0, 0)), one, one],
        out_specs=pl.BlockSpec((tm, D), lambda i: (i, 0)),
        out_shape=jax.ShapeDtypeStruct((T, D), F32),
        scratch_shapes=[pltpu.VMEM((2, K, tm * ns, LANES), F32), pltpu.SemaphoreType.DMA((2,))],
        compiler_params=_params(("arbitrary",)),
        name="moe_combine",
    )(slots, slots, out3, gates, res, gate, g.reshape(1, D), b.reshape(1, D))


def moe_layer(x, sc, sh, gate, ln_g, ln_b, router_w, router_b, w1, b1, w2, b2, layer, *, seq, alpha):
    T, D = x.shape
    E, K = MOE_E, MOE_K
    h3, meta, gates, cnt = moe_router(x, sc, sh, router_w, router_b, seq=seq)
    counts = cnt[0, :E]
    blocks_per = (counts + MOE_BLK - 1) // MOE_BLK
    block_end = jnp.cumsum(blocks_per)
    pad_start = (block_end - blocks_per) * MOE_BLK
    n_blocks = -(-(T * K) // MOE_BLK) + E
    blk = jnp.arange(n_blocks, dtype=I32)
    block_expert = jnp.minimum(jnp.sum(block_end[None, :] <= blk[:, None], axis=1), E - 1).astype(I32)
    onehot = meta[:, :K, None] == jnp.arange(E, dtype=I32)[None, None, :]
    slots = (meta[:, K:2 * K] + jnp.sum(jnp.where(onehot, pad_start[None, None, :], 0), axis=-1)).reshape(-1).astype(I32)
    slots = slots * (D // LANES)
    fill_start = jnp.where(blocks_per > 0, (block_end - 1) * MOE_BLK, (n_blocks + jnp.arange(E, dtype=I32)) * MOE_BLK).astype(I32)
    first_blk = block_end - blocks_per
    n_valid = jnp.clip(counts[block_expert] - (blk - first_blk[block_expert]) * MOE_BLK, 0, MOE_BLK)
    n_valid = jnp.where(blk < block_end[-1], n_valid, 0).astype(I32)
    xs3 = moe_dispatch(h3, slots, fill_start, (n_blocks + E) * MOE_BLK)
    out3 = moe_experts(xs3, block_expert, block_end[-1:].astype(I32), n_valid, w1, b1, w2, b2, layer)
    return moe_combine(out3, slots, gates, x, gate, ln_g, ln_b, seq=seq, alpha=alpha)


def _rope_tables(positions, d):
    inv = ROPE_BASE ** (-jnp.arange(0, d, 2, dtype=F32) / d)
    ang = positions.astype(F32).reshape(-1)[:, None] * inv
    return jnp.cos(ang), jnp.sin(ang)


def kernel(x, c, positions, ada_w, ada_b, ln1_g, ln1_b, ln2_g, ln2_b, router_w, router_b, moe_w1, moe_b1, moe_w2, moe_b2, ret_w_in, ret_gn_g, ret_gn_b, ret_w_out, conv_w_pw1, conv_b_pw1, conv_w_dw, conv_b_dw, conv_ln_g, conv_ln_b, conv_w_pw2, conv_b_pw2, nsa_w_in, nsa_gate_b, nsa_cmp_pos_k, nsa_cmp_pos_v, nsa_cmp_k_w1, nsa_cmp_k_w2, nsa_cmp_v_w1, nsa_cmp_v_w2, nsa_w_out, pool_w, pool_b, pool_scale):
    B, S, D = x.shape
    T = B * S
    depth = ada_w.shape[0]
    alpha = (2.0 * depth) ** 0.25
    mod = ada_modulation(c, ada_w, ada_b)
    xf = x.reshape(T, D)
    for i in range(depth):
        kind, j = i % 4, i // 4
        sh1, sc1, g1, sh2, sc2, g2 = (mod[i, :, k * D:(k + 1) * D].reshape(B, 1, D) for k in range(6))
        res1 = (g1, ln1_g[i], ln1_b[i])
        if kind == 0:
            dk = ret_w_in.shape[2] // 6 // RET_H
            cos, sin = _rope_tables(positions, dk)
            p = ret_projection(xf, sc1, sh1, ret_w_in[j].astype(BF16), cos, sin, seq=S)
            o = retention_core(p, ret_gn_g[j], ret_gn_b[j], batch=B, seq=S)
            xf = fused_matmul(o, ret_w_out[j].astype(BF16), seq=S, resln=(xf,) + res1, alpha=alpha, name="ret_out")
        elif kind == 1:
            u = conv_glu(xf, sc1, sh1, conv_w_pw1[j].astype(BF16), conv_b_pw1[j], seq=S)
            xf = conv_tail(u, conv_w_dw[j], conv_b_dw[j], conv_ln_g[j], conv_ln_b[j], conv_w_pw2[j].astype(BF16),
                           conv_b_pw2[j], xf, *res1, seq=S, alpha=alpha)
        elif kind == 2:
            G, dh, H = NSA_G, NSA_DH, NSA_H
            cos, sin = _rope_tables(positions, dh)
            cos_p = jnp.tile(cos, (1, LANES // (dh // 2)))
            sin_p = jnp.tile(jnp.concatenate([-sin, sin], axis=1), (1, LANES // dh))
            n_main = H * dh + 6 * G * dh
            ng = 3 * H // G
            w_g = jnp.zeros((D, G, LANES), F32).at[:, :, :ng].set(nsa_w_in[j][:, n_main:].reshape(D, G, ng))
            w_p = jnp.concatenate([nsa_w_in[j][:, :n_main], w_g.reshape(D, G * LANES)], axis=1).astype(BF16)
            gb_p = jnp.zeros((G, LANES), F32).at[:, :ng].set(nsa_gate_b[j].reshape(G, ng)).reshape(1, G * LANES)
            q, cv, kvx, gates = nsa_projection(xf, sc1, sh1, w_p, cos_p, sin_p, gb_p, seq=S)
            cv = cv.reshape(B, S, 2, G, dh).transpose(2, 0, 3, 1, 4).reshape(2, B * G, S // NSA_CMP_STEP, NSA_CMP_STEP * dh)
            k_cmp, v_cmp = nsa_compress(cv[0], cv[1], nsa_cmp_pos_k[j], nsa_cmp_pos_v[j], nsa_cmp_k_w1[j],
                                        nsa_cmp_k_w2[j], nsa_cmp_v_w1[j], nsa_cmp_v_w2[j])
            dup = lambda a: jnp.concatenate([a, a], axis=-1)
            lo = lambda a: jnp.concatenate([a, jnp.zeros_like(a)], axis=-1)
            hi = lambda a: jnp.concatenate([jnp.zeros_like(a), a], axis=-1)
            o = nsa_attention(q.reshape(B, S, H * dh), gates, dup(k_cmp), lo(v_cmp), hi(v_cmp), kvx, batch=B, seq=S)
            xf = fused_matmul(o.reshape(T, H * dh), nsa_w_out[j].astype(BF16), seq=S, resln=(xf,) + res1, alpha=alpha,
                              name="nsa_out")
        else:
            xf = pool_layer(xf, sc1, sh1, pool_w[j], pool_b[j], pool_scale[j], *res1, seq=S, alpha=alpha)
        xf = moe_layer(xf, sc2, sh2, g2, ln2_g[i], ln2_b[i], router_w[i], router_b[i], moe_w1, moe_b1, moe_w2, moe_b2, i,
                       seq=S, alpha=alpha)
    return xf.reshape(B, S, D)
```

```python
import functools
import math

import jax
import jax.numpy as jnp
import numpy as np
from jax import lax
from jax.experimental import pallas as pl
from jax.experimental.pallas import tpu as pltpu

F32 = jnp.float32
BF16 = jnp.bfloat16
I32 = jnp.int32

ROPE_BASE = 10000.0
EPS = 1e-5
NEG_BIG = -1e30

RET_H = 4
RET_C = 128
CONV_W = 31
NSA_H = 16
NSA_G = 2
NSA_DH = 64
NSA_CMP_LEN = 32
NSA_CMP_STEP = 16
NSA_SEL_LEN = 64
NSA_TOPN = 16
NSA_WIN = 512
NSA_FORCED = 1e6
POOL_WINS = (2, 4, 8, 16)
MOE_E = 32
MOE_K = 4
MOE_BLK = 512
MOE_SUB = 256
GLU_LIMIT = 7.0
GLU_ALPHA = 1.702

V7X_VMEM_LIMIT = 56 * 1024 * 1024
LANES = 128


def _params(sem):
    return pltpu.CompilerParams(dimension_semantics=sem, vmem_limit_bytes=V7X_VMEM_LIMIT)


def _ln_rows(v, g, b):
    mu = jnp.mean(v, axis=-1, keepdims=True)
    d = v - mu
    var = jnp.mean(d * d, axis=-1, keepdims=True)
    return d * lax.rsqrt(var + EPS) * g + b


def _sigmoid(v):
    return 1.0 / (1.0 + jnp.exp(-v))


def _dot(a, b):
    return jnp.dot(a, b, preferred_element_type=F32)


def _dot_nt(a, b):
    return lax.dot_general(a, b, (((1,), (1,)), ((), ())), preferred_element_type=F32)


def _dot_tn(a, b):
    return lax.dot_general(a, b, (((0,), (0,)), ((), ())), preferred_element_type=F32)


def _ada_kernel(c_ref, w_ref, b_ref, o_ref):
    c = c_ref[...]
    ca = c * _sigmoid(c)
    o_ref[0] = _dot(ca.astype(BF16), w_ref[0].astype(BF16)) + b_ref[0]


def ada_modulation(c, ada_w, ada_b):
    L, D, N = ada_w.shape
    B = c.shape[0]
    tn = 2048 if N % 2048 == 0 else N
    return pl.pallas_call(
        _ada_kernel,
        grid=(L, N // tn),
        in_specs=[pl.BlockSpec((B, D), lambda l, j: (0, 0)),
                  pl.BlockSpec((1, D, tn), lambda l, j: (l, 0, j)),
                  pl.BlockSpec((1, 1, tn), lambda l, j: (l, 0, j))],
        out_specs=pl.BlockSpec((1, B, tn), lambda l, j: (l, 0, j)),
        out_shape=jax.ShapeDtypeStruct((L, B, N), F32),
        compiler_params=_params(("parallel", "parallel")),
        name="ada_modulation",
    )(c, ada_w, ada_b.reshape(L, 1, N))


def _mm_kernel(*refs, has_mod, has_bias, epi, alpha):
    it = iter(refs)
    x_ref = next(it)
    if has_mod:
        sc_ref, sh_ref = next(it), next(it)
    w_ref = next(it)
    if has_bias:
        b_ref = next(it)
    if epi == "resln":
        res_ref, gate_ref, g_ref, bb_ref = next(it), next(it), next(it), next(it)
    o_ref = next(it)
    x = x_ref[...]
    if has_mod:
        x = x.astype(F32) * (1.0 + sc_ref[0]) + sh_ref[0]
    acc = _dot(x.astype(BF16), w_ref[...])
    if has_bias:
        acc = acc + b_ref[...]
    if epi == "resln":
        v = alpha * res_ref[...] + (1.0 + gate_ref[0]) * acc
        acc = _ln_rows(v, g_ref[...], bb_ref[...])
    o_ref[...] = acc.astype(o_ref.dtype)


def fused_matmul(x, w, *, seq, mod=None, bias=None, resln=None, alpha=1.0, out_dtype=F32, tm=512, tn=None, name="mm"):
    T, K = x.shape
    N = w.shape[1]
    tn = N if tn is None else tn
    assert T % tm == 0 and seq % tm == 0 and N % tn == 0
    nb = seq // tm
    args, specs = [x], [pl.BlockSpec((tm, K), lambda i, j: (i, 0))]
    if mod is not None:
        for m in mod:
            args.append(m)
            specs.append(pl.BlockSpec((1, 1, K), lambda i, j: (i // nb, 0, 0)))
    args.append(w)
    specs.append(pl.BlockSpec((K, tn), lambda i, j: (0, j)))
    if bias is not None:
        args.append(bias.reshape(1, N))
        specs.append(pl.BlockSpec((1, tn), lambda i, j: (0, j)))
    epi = "none"
    if resln is not None:
        assert tn == N
        res, gate, g, b = resln
        epi = "resln"
        args += [res, gate, g.reshape(1, N), b.reshape(1, N)]
        specs += [pl.BlockSpec((tm, N), lambda i, j: (i, 0)),
                  pl.BlockSpec((1, 1, N), lambda i, j: (i // nb, 0, 0)),
                  pl.BlockSpec((1, N), lambda i, j: (0, 0)),
                  pl.BlockSpec((1, N), lambda i, j: (0, 0))]
    kern = functools.partial(_mm_kernel, has_mod=mod is not None, has_bias=bias is not None, epi=epi, alpha=alpha)
    return pl.pallas_call(
        kern,
        grid=(T // tm, N // tn),
        in_specs=specs,
        out_specs=pl.BlockSpec((tm, tn), lambda i, j: (i, j)),
        out_shape=jax.ShapeDtypeStruct((T, N), out_dtype),
        compiler_params=_params(("parallel", "parallel")),
        name=name,
    )(*args)


def _ret_proj_kernel(x_ref, sc_ref, sh_ref, w_ref, cos_ref, sin_ref, o_ref, *, dk, hdk, tn, kscale):
    h = (x_ref[...] * (1.0 + sc_ref[0]) + sh_ref[0]).astype(BF16)
    cos, sin = cos_ref[...], sin_ref[...]
    half = dk // 2
    for j in range(o_ref.shape[1] // tn):
        c0 = j * tn
        acc = _dot(h, w_ref[:, c0:c0 + tn])
        if c0 < 2 * hdk:
            scale = 1.0 if c0 < hdk else kscale
            for hh in range(tn // dk):
                x1 = acc[:, hh * dk:hh * dk + half]
                x2 = acc[:, hh * dk + half:(hh + 1) * dk]
                o_ref[:, c0 + hh * dk:c0 + hh * dk + half] = ((x1 * cos - x2 * sin) * scale).astype(o_ref.dtype)
                o_ref[:, c0 + hh * dk + half:c0 + (hh + 1) * dk] = ((x1 * sin + x2 * cos) * scale).astype(o_ref.dtype)
        elif c0 < 4 * hdk:
            o_ref[:, c0:c0 + tn] = acc.astype(o_ref.dtype)
        else:
            o_ref[:, c0:c0 + tn] = (acc * _sigmoid(acc)).astype(o_ref.dtype)


def ret_projection(x, sc, sh, w_in, cos, sin, *, seq, tm=512, tn=1024):
    T, K = x.shape
    N = w_in.shape[1]
    hdk = N // 6
    dk = hdk // RET_H
    assert hdk % tn == 0 and tn % dk == 0
    nb = seq // tm
    kern = functools.partial(_ret_proj_kernel, dk=dk, hdk=hdk, tn=tn, kscale=dk ** -0.5)
    return pl.pallas_call(
        kern,
        grid=(T // tm,),
        in_specs=[pl.BlockSpec((tm, K), lambda i: (i, 0)),
                  pl.BlockSpec((1, 1, K), lambda i: (i // nb, 0, 0)),
                  pl.BlockSpec((1, 1, K), lambda i: (i // nb, 0, 0)),
                  pl.BlockSpec((K, N), lambda i: (0, 0), pipeline_mode=pl.Buffered(1)),
                  pl.BlockSpec((tm, dk // 2), lambda i: (i, 0)),
                  pl.BlockSpec((tm, dk // 2), lambda i: (i, 0))],
        out_specs=pl.BlockSpec((tm, N), lambda i: (i, 0)),
        out_shape=jax.ShapeDtypeStruct((T, N), BF16),
        compiler_params=_params(("parallel",)),
        name="ret_projection",
    )(x, sc, sh, w_in, cos, sin)


def _retention_kernel(q_ref, k_ref, v_ref, sg_ref, din_ref, dq_ref, dkk_ref, dch_ref, gg_ref, gb_ref,
                      o_ref, state_ref, *, H, C, dk, dv):
    @pl.when(pl.program_id(1) == 0)
    def _():
        state_ref[...] = jnp.zeros_like(state_ref)

    n_chunks = q_ref.shape[0] // C
    for c in range(n_chunks):
        rows = slice(c * C, (c + 1) * C)
        for hh in range(H):
            qc = q_ref[rows, hh * dk:(hh + 1) * dk]
            kc = k_ref[rows, hh * dk:(hh + 1) * dk]
            vc = v_ref[rows, hh * dv:(hh + 1) * dv]
            st = state_ref[hh]
            a = _dot_nt(qc, kc) * din_ref[hh]
            o = _dot(a.astype(BF16), vc)
            qd = (qc.astype(F32) * dq_ref[hh]).astype(BF16)
            o = o + _dot(qd, st.astype(BF16))
            kd = (kc.astype(F32) * dkk_ref[hh]).astype(BF16)
            state_ref[hh] = st * dch_ref[hh, :, 0:1] + _dot_tn(kd, vc)
            mu = jnp.mean(o, axis=-1, keepdims=True)
            d = o - mu
            var = jnp.mean(d * d, axis=-1, keepdims=True)
            on = d * lax.rsqrt(var + EPS) * gg_ref[:, hh * dv:(hh + 1) * dv] + gb_ref[:, hh * dv:(hh + 1) * dv]
            sg = sg_ref[rows, hh * dv:(hh + 1) * dv].astype(F32)
            o_ref[rows, hh * dv:(hh + 1) * dv] = (sg * on).astype(o_ref.dtype)


def retention_core(p, gn_g, gn_b, *, batch, seq, tb=512):
    T, N = p.shape
    H, C = RET_H, RET_C
    hdk = N // 6
    dk, dv = hdk // H, 2 * hdk // H
    hdv = H * dv
    nb = seq // tb
    log_gamma = jnp.log1p(-(2.0 ** (-5.0 - jnp.arange(H, dtype=F32))))
    i = jnp.arange(C, dtype=F32)
    rel = i[:, None] - i[None, :]
    d_intra = jnp.where(rel >= 0, jnp.exp(log_gamma[:, None, None] * jnp.maximum(rel, 0.0)), 0.0).astype(F32)
    d_q = jnp.exp(log_gamma[:, None] * (i + 1.0)).astype(F32)[..., None]
    d_k = jnp.exp(log_gamma[:, None] * (C - 1.0 - i)).astype(F32)[..., None]
    d_ch = jnp.broadcast_to(jnp.exp(log_gamma * C).astype(F32)[:, None, None], (H, 1, LANES))
    kern = functools.partial(_retention_kernel, H=H, C=C, dk=dk, dv=dv)
    return pl.pallas_call(
        kern,
        grid=(batch, nb),
        in_specs=[pl.BlockSpec((tb, hdk), lambda b, s: (b * nb + s, 0)),
                  pl.BlockSpec((tb, hdk), lambda b, s: (b * nb + s, 1)),
                  pl.BlockSpec((tb, hdv), lambda b, s: (b * nb + s, 1)),
                  pl.BlockSpec((tb, hdv), lambda b, s: (b * nb + s, 2)),
                  pl.BlockSpec((H, C, C), lambda b, s: (0, 0, 0)),
                  pl.BlockSpec((H, C, 1), lambda b, s: (0, 0, 0)),
                  pl.BlockSpec((H, C, 1), lambda b, s: (0, 0, 0)),
                  pl.BlockSpec((H, 1, LANES), lambda b, s: (0, 0, 0)),
                  pl.BlockSpec((1, hdv), lambda b, s: (0, 0)),
                  pl.BlockSpec((1, hdv), lambda b, s: (0, 0))],
        out_specs=pl.BlockSpec((tb, hdv), lambda b, s: (b * nb + s, 0)),
        out_shape=jax.ShapeDtypeStruct((T, hdv), BF16),
        scratch_shapes=[pltpu.VMEM((H, dk, dv), F32)],
        compiler_params=_params(("parallel", "arbitrary")),
        name="retention_core",
    )(p, p, p, p, d_intra, d_q, d_k, d_ch, gn_g.reshape(1, hdv), gn_b.reshape(1, hdv))


def _glu_kernel(x_ref, sc_ref, sh_ref, wa_ref, wg_ref, ba_ref, bg_ref, o_ref):
    h = (x_ref[...] * (1.0 + sc_ref[0]) + sh_ref[0]).astype(BF16)
    a = _dot(h, wa_ref[...]) + ba_ref[...]
    g = _dot(h, wg_ref[...]) + bg_ref[...]
    o_ref[...] = a * _sigmoid(g)


def conv_glu(x, sc, sh, w_pw1, b_pw1, *, seq, tm=512, tn=512):
    T, K = x.shape
    N = w_pw1.shape[1] // 2
    nb = seq // tm
    ng = N // tn
    b2 = b_pw1.reshape(1, 2 * N)
    return pl.pallas_call(
        _glu_kernel,
        grid=(T // tm, ng),
        in_specs=[pl.BlockSpec((tm, K), lambda i, j: (i, 0)),
                  pl.BlockSpec((1, 1, K), lambda i, j: (i // nb, 0, 0)),
                  pl.BlockSpec((1, 1, K), lambda i, j: (i // nb, 0, 0)),
                  pl.BlockSpec((K, tn), lambda i, j: (0, j)),
                  pl.BlockSpec((K, tn), lambda i, j: (0, j + ng)),
                  pl.BlockSpec((1, tn), lambda i, j: (0, j)),
                  pl.BlockSpec((1, tn), lambda i, j: (0, j + ng))],
        out_specs=pl.BlockSpec((tm, tn), lambda i, j: (i, j)),
        out_shape=jax.ShapeDtypeStruct((T, N), F32),
        compiler_params=_params(("parallel", "parallel")),
        name="conv_glu",
    )(x, sc, sh, w_pw1, w_pw1, b2, b2)


def _conv_tail_kernel(u_ref, up_ref, wdw_ref, bdw_ref, lg_ref, lb_ref, w2_ref, b2_ref,
                      res_ref, gate_ref, g_ref, bb_ref, o_ref, win_ref, shift_ref, *, nb, halo, width, alpha):
    tm = u_ref.shape[0]
    first = (pl.program_id(0) % nb) == 0
    prev = up_ref[...]
    win_ref[0:halo, :] = jnp.where(first, jnp.zeros_like(prev), prev)
    win_ref[halo:halo + tm, :] = u_ref[...]
    for b in range(1, 8):
        n = (tm + halo - b) // 8 * 8
        shift_ref[b - 1, 0:n, :] = win_ref[b:b + n, :]
    acc = jnp.zeros(u_ref.shape, F32)
    off = halo - (width - 1)
    for k in range(width):
        b, a = (off + k) % 8, (off + k) // 8 * 8
        rows = win_ref[a:a + tm, :] if b == 0 else shift_ref[b - 1, a:a + tm, :]
        acc = acc + wdw_ref[k:k + 1, :] * rows
    acc = acc + bdw_ref[...]
    v = _ln_rows(acc, lg_ref[...], lb_ref[...])
    v = v * _sigmoid(v)
    y = _dot(v.astype(BF16), w2_ref[...]) + b2_ref[...]
    z = alpha * res_ref[...] + (1.0 + gate_ref[0]) * y
    o_ref[...] = _ln_rows(z, g_ref[...], bb_ref[...])


def conv_tail(u, w_dw, b_dw, ln_g, ln_b, w_pw2, b_pw2, res, gate, g, b, *, seq, alpha, tm=512):
    T, D = u.shape
    nb = seq // tm
    halo = 32
    assert CONV_W - 1 <= halo and tm % halo == 0
    r = tm // halo
    wpad = jnp.zeros((halo, D), F32).at[:CONV_W].set(w_dw)
    kern = functools.partial(_conv_tail_kernel, nb=nb, halo=halo, width=CONV_W, alpha=alpha)
    vec = lambda a: a.reshape(1, D)
    one = pl.BlockSpec((1, D), lambda i: (0, 0))
    return pl.pallas_call(
        kern,
        grid=(T // tm,),
        in_specs=[pl.BlockSpec((tm, D), lambda i: (i, 0)),
                  pl.BlockSpec((halo, D), lambda i: (jnp.maximum(i * r - 1, 0), 0)),
                  pl.BlockSpec((halo, D), lambda i: (0, 0)),
                  one, one, one,
                  pl.BlockSpec((D, D), lambda i: (0, 0)),
                  one,
                  pl.BlockSpec((tm, D), lambda i: (i, 0)),
                  pl.BlockSpec((1, 1, D), lambda i: (i // nb, 0, 0)),
                  one, one],
        out_specs=pl.BlockSpec((tm, D), lambda i: (i, 0)),
        out_shape=jax.ShapeDtypeStruct((T, D), F32),
        scratch_shapes=[pltpu.VMEM((halo + tm, D), F32), pltpu.VMEM((7, halo + tm, D), F32)],
        compiler_params=_params(("parallel",)),
        name="conv_tail",
    )(u, u, wpad, vec(b_dw), vec(ln_g), vec(ln_b), w_pw2, vec(b_pw2), res, gate, vec(g), vec(b))


def _pool_kernel(x_ref, xp_ref, sc_ref, sh_ref, w_ref, pb_ref, ps_ref, gate_ref, g_ref, bb_ref,
                 o_ref, win_ref, *, nb, halo, wins, alpha):
    tm, D = x_ref.shape
    gw = D // len(wins)
    i = pl.program_id(0)
    first = (i % nb) == 0
    x = x_ref[...]
    sc, sh = sc_ref[0], sh_ref[0]
    hp = xp_ref[...] * (1.0 + sc) + sh
    win_ref[0:halo, :] = jnp.where(first, jnp.zeros_like(hp), hp)
    win_ref[halo:halo + tm, :] = x * (1.0 + sc) + sh
    t = (i % nb) * tm + lax.broadcasted_iota(I32, (tm, gw), 0)
    ys = []
    for gi, wn in enumerate(wins):
        cols = slice(gi * gw, (gi + 1) * gw)
        s = win_ref[halo:halo + tm, cols]
        hcur = s
        for k in range(1, wn):
            s = s + win_ref[halo - k:halo - k + tm, cols]
        cnt = jnp.minimum(t + 1, wn).astype(F32)
        pooled = s / cnt - hcur
        ys.append(_dot(pooled.astype(BF16), w_ref[gi]))
    y = (jnp.concatenate(ys, axis=-1) + pb_ref[...]) * ps_ref[...]
    z = alpha * x + (1.0 + gate_ref[0]) * y
    o_ref[...] = _ln_rows(z, g_ref[...], bb_ref[...])


def pool_layer(x, sc, sh, pool_w, pool_b, pool_scale, gate, g, b, *, seq, alpha, tm=512):
    T, D = x.shape
    nb = seq // tm
    halo = 16
    assert max(POOL_WINS) <= halo and tm % halo == 0
    r = tm // halo
    G, gw, _ = pool_w.shape
    kern = functools.partial(_pool_kernel, nb=nb, halo=halo, wins=POOL_WINS, alpha=alpha)
    vec = lambda a: a.reshape(1, D)
    one = pl.BlockSpec((1, D), lambda i: (0, 0))
    per_b = pl.BlockSpec((1, 1, D), lambda i: (i // nb, 0, 0))
    return pl.pallas_call(
        kern,
        grid=(T // tm,),
        in_specs=[pl.BlockSpec((tm, D), lambda i: (i, 0)),
                  pl.BlockSpec((halo, D), lambda i: (jnp.maximum(i * r - 1, 0), 0)),
                  per_b, per_b,
                  pl.BlockSpec((G, gw, gw), lambda i: (0, 0, 0)),
                  one, one, per_b, one, one],
        out_specs=pl.BlockSpec((tm, D), lambda i: (i, 0)),
        out_shape=jax.ShapeDtypeStruct((T, D), F32),
        scratch_shapes=[pltpu.VMEM((halo + tm, D), F32)],
        compiler_params=_params(("parallel",)),
        name="pool_layer",
    )(x, x, sc, sh, pool_w.astype(BF16), vec(pool_b), vec(pool_scale), gate, vec(g), vec(b))


def _rope64(x, cos, sin_signed, lane):
    partner = jnp.where((lane & (NSA_DH - 1)) < NSA_DH // 2, pltpu.roll(x, LANES - NSA_DH // 2, 1), pltpu.roll(x, NSA_DH // 2, 1))
    return x * cos + partner * sin_signed


def _nsa_proj_kernel(x_ref, sc_ref, sh_ref, w_ref, cos_ref, sin_ref, gb_ref, q_ref, cv_ref, kvx_ref, gt_ref, *, nq, qscale):
    h = x_ref[...] * (1.0 + sc_ref[0]) + sh_ref[0]
    acc = _dot(h.astype(BF16), w_ref[...])
    cos, sin = cos_ref[...], sin_ref[...]
    lane = lax.broadcasted_iota(I32, cos.shape, 1)
    for s in range(nq):
        blk = acc[:, s * LANES:(s + 1) * LANES]
        q_ref[:, s * LANES:(s + 1) * LANES] = (_rope64(blk, cos, sin, lane) * qscale).astype(q_ref.dtype)
    low = lane < NSA_DH
    for s in range(6):
        blk = acc[:, (nq + s) * LANES:(nq + s + 1) * LANES]
        if s % 2 == 0:
            blk = _rope64(blk, cos, sin, lane)
        if s < 2:
            cv_ref[:, s * LANES:(s + 1) * LANES] = blk.astype(cv_ref.dtype)
            continue
        swapped = pltpu.roll(blk, NSA_DH, 1)
        for gi in range(2):
            own, other = (blk, swapped) if gi == 0 else (swapped, blk)
            if s % 2 == 0:
                kvx_ref[3 * (s // 2 - 1), 0, gi] = jnp.where(low, own, other).astype(kvx_ref.dtype)
            else:
                v_lo = jnp.where(lane == NSA_DH, 1.0, jnp.where(low, own, 0.0))
                v_hi = jnp.where(lane == 0, 1.0, jnp.where(low, 0.0, other))
                kvx_ref[3 * (s // 2 - 1) + 1, 0, gi] = v_lo.astype(kvx_ref.dtype)
                kvx_ref[3 * (s // 2 - 1) + 2, 0, gi] = v_hi.astype(kvx_ref.dtype)
    for gi in range(gt_ref.shape[1] // LANES):
        gl = acc[:, (nq + 6 + gi) * LANES:(nq + 7 + gi) * LANES]
        gt_ref[:, gi * LANES:(gi + 1) * LANES] = _sigmoid(gl + gb_ref[:, gi * LANES:(gi + 1) * LANES])


def nsa_projection(x, sc, sh, w_in_p, cos, sin_signed, gate_b_p, *, seq, tm=512):
    T, K = x.shape
    Np = w_in_p.shape[1]
    nq = NSA_H * NSA_DH // LANES
    assert NSA_G * NSA_DH == LANES
    ng = NSA_G * LANES
    nb = seq // tm
    kern = functools.partial(_nsa_proj_kernel, nq=nq, qscale=NSA_DH ** -0.5)
    per_b = pl.BlockSpec((1, 1, K), lambda i: (i // nb, 0, 0))
    return pl.pallas_call(
        kern,
        grid=(T // tm,),
        in_specs=[pl.BlockSpec((tm, K), lambda i: (i, 0)), per_b, per_b,
                  pl.BlockSpec((K, Np), lambda i: (0, 0)),
                  pl.BlockSpec((tm, LANES), lambda i: (i, 0)),
                  pl.BlockSpec((tm, LANES), lambda i: (i, 0)),
                  pl.BlockSpec((1, ng), lambda i: (0, 0))],
        out_specs=[pl.BlockSpec((tm, nq * LANES), lambda i: (i, 0)),
                   pl.BlockSpec((tm, 2 * LANES), lambda i: (i, 0)),
                   pl.BlockSpec((6, 1, NSA_G, tm, LANES), lambda i: (0, i // nb, 0, i % nb, 0)),
                   pl.BlockSpec((tm, ng), lambda i: (i, 0))],
        out_shape=[jax.ShapeDtypeStruct((T, nq * LANES), BF16),
                   jax.ShapeDtypeStruct((T, 2 * LANES), BF16),
                   jax.ShapeDtypeStruct((6, T // seq, NSA_G, seq, LANES), BF16),
                   jax.ShapeDtypeStruct((T, ng), F32)],
        compiler_params=_params(("parallel",)),
        name="nsa_projection",
    )(x, sc, sh, w_in_p, cos, sin_signed, gate_b_p)


def _gelu_tanh(v):
    return 0.5 * v * (1.0 + jnp.tanh(math.sqrt(2.0 / math.pi) * (v + 0.044715 * v * v * v)))


def _compress_kernel(xk_ref, xv_ref, pk_ref, pv_ref, kw1_ref, kw2_ref, vw1_ref, vw2_ref, ok_ref, ov_ref):
    def one(x_ref, pe_ref, w1_ref, w2_ref, o_ref):
        x = x_ref[0].astype(F32)
        n = x.shape[0]
        half = x.shape[1]
        a = _dot((x + pe_ref[0:1, :]).astype(BF16), w1_ref[0:half, :])
        b = _dot((x + pe_ref[1:2, :]).astype(BF16), w1_ref[half:2 * half, :])
        pre = a + pltpu.roll(b, n - 1, 0)
        o_ref[0] = _dot(_gelu_tanh(pre).astype(BF16), w2_ref[...]).astype(o_ref.dtype)

    one(xk_ref, pk_ref, kw1_ref, kw2_ref, ok_ref)
    one(xv_ref, pv_ref, vw1_ref, vw2_ref, ov_ref)


def nsa_compress(kc_chunks, vc_chunks, pos_k, pos_v, k_w1, k_w2, v_w1, v_w2):
    BG, n, width = kc_chunks.shape
    hid = k_w1.shape[1]
    dh = k_w2.shape[1]
    full = lambda a: pl.BlockSpec(a.shape, lambda i: (0,) * a.ndim)
    pk = pos_k.reshape(2, width)
    pv = pos_v.reshape(2, width)
    ws = [k_w1.astype(BF16), k_w2.astype(BF16), v_w1.astype(BF16), v_w2.astype(BF16)]
    del hid
    return pl.pallas_call(
        _compress_kernel,
        grid=(BG,),
        in_specs=[pl.BlockSpec((1, n, width), lambda i: (i, 0, 0)),
                  pl.BlockSpec((1, n, width), lambda i: (i, 0, 0)),
                  full(pk), full(pv)] + [full(w) for w in ws],
        out_specs=[pl.BlockSpec((1, n, dh), lambda i: (i, 0, 0)),
                   pl.BlockSpec((1, n, dh), lambda i: (i, 0, 0))],
        out_shape=[jax.ShapeDtypeStruct((BG, n, dh), BF16), jax.ShapeDtypeStruct((BG, n, dh), BF16)],
        compiler_params=_params(("parallel",)),
        name="nsa_compress",
    )(kc_chunks, vc_chunks, pk, pv, *ws)


def _nsa_attn_kernel(q_ref, gt_ref, kc_ref, vcl_ref, vch_ref, ks_ref, vsl_ref, vsh_ref, kw_ref, vwl_ref, vwh_ref,
                     ov_ref, o_ref, qs_ref, m_ref, ae_ref, ao_ref, res_ref, sc_ref,
                     *, qt, ck, n_sel, top_n, sel_shift, cmp_len, cmp_step, window):
    t0 = pl.program_id(2) * qt
    R = q_ref.shape[2] // NSA_DH
    rows = R * qt
    half_rows = rows // 2
    lane = lax.broadcasted_iota(I32, (qt, LANES), 1)

    for p in range(R // 2):
        slab = q_ref[0, :, p * LANES:(p + 1) * LANES]
        qs_ref[p * qt:(p + 1) * qt, :] = jnp.where(lane < NSA_DH, slab, jnp.zeros_like(slab))
        qs_ref[half_rows + p * qt:half_rows + (p + 1) * qt, :] = jnp.where(lane >= NSA_DH, slab, jnp.zeros_like(slab))
    qs = qs_ref[...]
    tq = t0 + lax.broadcasted_iota(I32, (qt, 1), 0)

    ncmp = kc_ref.shape[0]
    cmp_end = lax.broadcasted_iota(I32, (qt, ncmp), 1) * cmp_step + (cmp_len - 1)
    s3 = _dot_nt(qs, kc_ref[...]).reshape(R, qt, ncmp) + jnp.where(cmp_end <= tq, 0.0, NEG_BIG)
    e3 = jnp.exp(s3 - jnp.max(s3, axis=-1, keepdims=True))
    row_ok = jnp.where(tq >= cmp_len - 1, 1.0, 0.0)
    pc3 = e3 * (row_ok / jnp.sum(e3, axis=-1, keepdims=True))
    pcb = pc3.reshape(rows, ncmp).astype(BF16)
    res_ref[0, 0:half_rows, :] = _dot(pcb[0:half_rows], vcl_ref[...])
    res_ref[0, half_rows:rows, :] = _dot(pcb[half_rows:], vch_ref[...])
    psum = jnp.sum(pc3, axis=0)
    imp = _dot(psum.astype(BF16), ov_ref[...])

    cur = lax.shift_right_logical(tq, sel_shift)
    dcur = cur - lane
    forced = (lane == 0) | (dcur == 0) | (dcur == 1)
    score = jnp.where(lax.shift_left(lane, sel_shift) <= tq, jnp.where(forced, NSA_FORCED, imp), NEG_BIG)
    score = jnp.where(lane < n_sel, score, -jnp.inf)
    st = score.T
    blk_f = lax.broadcasted_iota(I32, st.shape, 0).astype(F32)
    sel_t = jnp.zeros(st.shape, F32)
    for _ in range(top_n):
        mx = jnp.max(st, axis=0, keepdims=True)
        idx = jnp.min(jnp.where(st == mx, blk_f, float(LANES)), axis=0, keepdims=True)
        hit = blk_f == idx
        sel_t = jnp.where(hit, jnp.where(mx > 0.5 * NEG_BIG, 1.0, sel_t), sel_t)
        st = jnp.where(hit, -jnp.inf, st)
    selb = sel_t.T.astype(BF16)

    def attend(k_ref, vl_ref, vh_ref, c_lo, c_hi, bias_fn, slot):
        m_ref[...] = jnp.full(m_ref.shape, NEG_BIG, F32)
        ae_ref[...] = jnp.zeros(ae_ref.shape, F32)
        ao_ref[...] = jnp.zeros(ao_ref.shape, F32)

        def scores(c):
            start = pl.multiple_of(c * ck, ck)
            return _dot_nt(qs, k_ref[pl.ds(start, ck), :]).reshape(R, qt, ck) + bias_fn(start)

        sc_ref[...] = scores(c_lo)

        def body(c, carry):
            start = pl.multiple_of(c * ck, ck)
            sc_next = scores(jnp.minimum(c + 1, c_hi - 1))
            sc3 = sc_ref[...]
            m_old = m_ref[...]
            m_new = jnp.maximum(m_old, jnp.max(sc3, axis=-1, keepdims=True).reshape(rows, 1))
            alpha = jnp.exp(m_old - m_new)
            p3 = jnp.exp(sc3 - m_new.reshape(R, qt, 1))
            m_ref[...] = m_new
            pb = p3.reshape(rows, ck).astype(BF16)
            ae_ref[...] = alpha[0:half_rows] * ae_ref[...] + _dot(pb[0:half_rows], vl_ref[pl.ds(start, ck), :])
            ao_ref[...] = alpha[half_rows:] * ao_ref[...] + _dot(pb[half_rows:], vh_ref[pl.ds(start, ck), :])
            sc_ref[...] = sc_next
            return carry

        lax.fori_loop(c_lo, c_hi, body, 0)
        lane_h = lax.broadcasted_iota(I32, (half_rows, LANES), 1)
        ae, ao = ae_ref[...], ao_ref[...]
        res_ref[slot, 0:half_rows, :] = jnp.where(lane_h < NSA_DH, ae / ae[:, NSA_DH:NSA_DH + 1], 0.0)
        res_ref[slot, half_rows:rows, :] = jnp.where(lane_h >= NSA_DH, ao / ao[:, 0:1], 0.0)

    def key_dist(start):
        return tq - (start + lax.broadcasted_iota(I32, (qt, ck), 1))

    def sel_bias(start):
        jrow = lax.broadcasted_iota(I32, (LANES, ck), 0)
        jcol = lax.shift_right_logical(start + lax.broadcasted_iota(I32, (LANES, ck), 1), sel_shift)
        expand = jnp.where(jrow == jcol, 1.0, 0.0).astype(BF16)
        picked = _dot(selb, expand)
        return jnp.where(key_dist(start) >= 0, jnp.where(picked > 0.5, 0.0, NEG_BIG), NEG_BIG)

    def win_bias(start):
        dist = key_dist(start)
        return jnp.where(jnp.where(dist >= 0, dist, window) < window, 0.0, NEG_BIG)

    c_hi = (t0 + qt + ck - 1) // ck
    attend(ks_ref, vsl_ref, vsh_ref, 0, c_hi, sel_bias, 1)
    attend(kw_ref, vwl_ref, vwh_ref, jnp.maximum(t0 - window + 1, 0) // ck, c_hi, win_bias, 2)

    gt = gt_ref[...]
    for p in range(R // 2):
        acc = jnp.zeros((qt, LANES), F32)
        for par in range(2):
            rsl = slice(par * half_rows + p * qt, par * half_rows + (p + 1) * qt)
            for br in range(3):
                col = (2 * p + par) * 3 + br
                acc = acc + gt[:, col:col + 1] * res_ref[br, rsl, :]
        o_ref[0, :, p * LANES:(p + 1) * LANES] = acc.astype(o_ref.dtype)


def nsa_attention(q, gates, kcd, vcl, vch, kvx, *, batch, seq, qt=256, ck=512):
    G = NSA_G
    R = NSA_H // G
    gq = R * NSA_DH
    n_sel = seq // NSA_SEL_LEN
    ncmp = kcd.shape[1]
    nq = seq // qt
    ov = np.zeros((ncmp, LANES), np.float32)
    for n in range(ncmp):
        for j in range(n_sel):
            if n * NSA_CMP_STEP < (j + 1) * NSA_SEL_LEN and n * NSA_CMP_STEP + NSA_CMP_LEN > j * NSA_SEL_LEN:
                ov[n, j] = 1.0
    n_valid = (seq - NSA_CMP_LEN) // NSA_CMP_STEP + 1
    ov[n_valid:] = 0.0
    kern = functools.partial(_nsa_attn_kernel, qt=qt, ck=ck, n_sel=n_sel, top_n=min(NSA_TOPN, n_sel),
                             sel_shift=int(math.log2(NSA_SEL_LEN)), cmp_len=NSA_CMP_LEN, cmp_step=NSA_CMP_STEP, window=NSA_WIN)
    cmp_spec = pl.BlockSpec((None, ncmp, LANES), lambda b, g, i: (b * G + g, 0, 0))
    seq_specs = [pl.BlockSpec((None, None, None, seq, LANES), functools.partial(lambda b, g, i, k: (k, b, g, 0, 0), k=k))
                 for k in range(6)]
    rows = R * qt
    return pl.pallas_call(
        kern,
        grid=(batch, G, nq),
        in_specs=[pl.BlockSpec((1, qt, gq), lambda b, g, i: (b, i, g)),
                  pl.BlockSpec((qt, LANES), lambda b, g, i: (b * nq + i, g)),
                  cmp_spec, cmp_spec, cmp_spec] + seq_specs + [
                  pl.BlockSpec((ncmp, LANES), lambda b, g, i: (0, 0))],
        out_specs=pl.BlockSpec((1, qt, gq), lambda b, g, i: (b, i, g)),
        out_shape=jax.ShapeDtypeStruct((batch, seq, NSA_H * NSA_DH), BF16),
        scratch_shapes=[pltpu.VMEM((rows, LANES), BF16),
                        pltpu.VMEM((rows, 1), F32),
                        pltpu.VMEM((rows // 2, LANES), F32), pltpu.VMEM((rows // 2, LANES), F32),
                        pltpu.VMEM((3, rows, LANES), F32),
                        pltpu.VMEM((R, qt, ck), F32)],
        compiler_params=_params(("parallel", "parallel", "arbitrary")),
        name="nsa_attention",
    )(q, gates, kcd, vcl, vch, kvx, kvx, kvx, kvx, kvx, kvx, jnp.asarray(ov, BF16))


def _tile_rows(ref, idx, ns):
    return ref.at[pl.ds(pl.multiple_of(idx * ns, ns), ns)]


def _tile_rows_from(ref, row0, ns):
    return ref.at[pl.ds(pl.multiple_of(row0, ns), ns)]


def _router_kernel(x_ref, sc_ref, sh_ref, w_ref, b_ref, h3_ref, meta_ref, gate_ref, cnt_ref, seen_ref, seen_col_ref,
                   *, top_k):
    @pl.when(pl.program_id(0) == 0)
    def _():
        seen_ref[...] = jnp.zeros_like(seen_ref)
        seen_col_ref[...] = jnp.zeros_like(seen_col_ref)

    h = x_ref[...] * (1.0 + sc_ref[0]) + sh_ref[0]
    tm = h.shape[0]
    ns = h.shape[1] // LANES
    for s in range(ns):
        h3_ref[pl.ds(s, tm, stride=ns), :] = h[:, s * LANES:(s + 1) * LANES]
    logits = _dot(h.astype(BF16), w_ref[...]) + b_ref[...]
    lt = logits.T
    exp_f = lax.broadcasted_iota(I32, lt.shape, 0).astype(F32)
    chosen_t = jnp.zeros(lt.shape, F32)
    idxs, hits, vals = [], [], []
    for k in range(top_k):
        mx = jnp.max(lt, axis=0, keepdims=True)
        idx = jnp.min(jnp.where(lt == mx, exp_f, float(LANES)), axis=0, keepdims=True)
        hit = exp_f == idx
        idxs.append(idx)
        hits.append(hit)
        vals.append(mx)
        chosen_t = jnp.where(hit, 1.0, chosen_t)
        lt = jnp.where(hit, -jnp.inf, lt)
    es = [jnp.exp(v - vals[0]) for v in vals]
    den = es[0]
    for e in es[1:]:
        den = den + e
    upper = jnp.where(lax.broadcasted_iota(I32, (tm, tm), 0) < lax.broadcasted_iota(I32, (tm, tm), 1), 1.0, 0.0)
    cb = chosen_t.astype(BF16)
    before_t = _dot(cb, upper.astype(BF16)) + seen_col_ref[...]
    row8 = lax.broadcasted_iota(I32, (2 * top_k, tm), 0)
    meta8 = jnp.zeros((2 * top_k, tm), F32)
    gate8 = jnp.zeros((2 * top_k, tm), F32)
    for k in range(top_k):
        rank = jnp.sum(jnp.where(hits[k], before_t, 0.0), axis=0, keepdims=True)
        meta8 = jnp.where(row8 == k, idxs[k], meta8)
        meta8 = jnp.where(row8 == top_k + k, rank, meta8)
        gate8 = jnp.where(row8 == k, es[k] / den, gate8)
    pad = jnp.zeros((LANES - 2 * top_k, tm), F32)
    meta_ref[...] = jnp.concatenate([meta8, pad], axis=0).T.astype(I32)
    gate_ref[...] = jnp.concatenate([gate8, pad], axis=0).T
    seen_col_ref[...] = seen_col_ref[...] + jnp.sum(chosen_t, axis=1, keepdims=True)
    seen_ref[...] = seen_ref[...] + _dot_nt(jnp.ones((8, tm), BF16), cb)[0:1, :]
    cnt_ref[...] = seen_ref[...].astype(I32)


def moe_router(x, sc, sh, router_w, router_b, *, seq, tm=512):
    T, D = x.shape
    E = router_w.shape[1]
    nb = seq // tm
    ns = D // LANES
    assert ns == 8
    wp = jnp.zeros((D, LANES), BF16).at[:, :E].set(router_w.astype(BF16))
    bp = jnp.full((1, LANES), NEG_BIG, F32).at[0, :E].set(router_b)
    per_b = pl.BlockSpec((1, 1, D), lambda i: (i // nb, 0, 0))
    return pl.pallas_call(
        functools.partial(_router_kernel, top_k=MOE_K),
        grid=(T // tm,),
        in_specs=[pl.BlockSpec((tm, D), lambda i: (i, 0)), per_b, per_b,
                  pl.BlockSpec((D, LANES), lambda i: (0, 0)),
                  pl.BlockSpec((1, LANES), lambda i: (0, 0))],
        out_specs=[pl.BlockSpec((tm * ns, LANES), lambda i: (i, 0)),
                   pl.BlockSpec((tm, LANES), lambda i: (i, 0)),
                   pl.BlockSpec((tm, LANES), lambda i: (i, 0)),
                   pl.BlockSpec((1, LANES), lambda i: (0, 0))],
        out_shape=[jax.ShapeDtypeStruct((T * ns, LANES), F32),
                   jax.ShapeDtypeStruct((T, LANES), I32),
                   jax.ShapeDtypeStruct((T, LANES), F32),
                   jax.ShapeDtypeStruct((1, LANES), I32)],
        scratch_shapes=[pltpu.VMEM((1, LANES), F32), pltpu.VMEM((LANES, 1), F32)],
        compiler_params=_params(("arbitrary",)),
        name="moe_router",
    )(x, sc, sh, wp, bp)


def _dispatch_kernel(fill_ref, slots_ref, slots1_ref, slots2_ref, h3_ref, xs3_ref, zero_ref, hbuf_ref, fill_sem,
                     load_sems, row_sems, *, tm, top_k, n_exp, ns):
    i = pl.program_id(0)
    n = pl.num_programs(0)
    nbuf = hbuf_ref.shape[0]

    def load(step):
        src = h3_ref.at[pl.ds(pl.multiple_of(step * (tm * ns), tm * ns), tm * ns)]
        return pltpu.make_async_copy(src, hbuf_ref.at[step % nbuf], load_sems.at[step % nbuf])

    def row_copy(step, slots, r, k):
        return pltpu.make_async_copy(_tile_rows(hbuf_ref.at[step % nbuf], r, ns),
                                     _tile_rows_from(xs3_ref, slots[r * top_k + k], ns), row_sems.at[step % nbuf])

    def issue_rows(step, slots):
        def body(r, carry):
            for k in range(top_k):
                row_copy(step, slots, r, k).start(priority=k % 2)
            return carry
        lax.fori_loop(0, tm, body, 0, unroll=8)

    def drain_rows(step, slots):
        def body(r, carry):
            for k in range(top_k):
                row_copy(step, slots, r, k).wait()
            return carry
        lax.fori_loop(0, tm, body, 0, unroll=8)

    @pl.when(i == 0)
    def _():
        load(0).start()
        zero_ref[...] = jnp.zeros_like(zero_ref)

        def fill(e):
            dst = xs3_ref.at[pl.ds(pl.multiple_of(fill_ref[e] * ns, ns), zero_ref.shape[0])]
            return pltpu.make_async_copy(zero_ref, dst, fill_sem)

        for e in range(n_exp):
            fill(e).start()
        for e in range(n_exp):
            fill(e).wait()

    @pl.when(i >= 2)
    def _():
        drain_rows(i - 2, slots2_ref)

    @pl.when(i + 1 < n)
    def _():
        load(i + 1).start()

    load(i).wait()
    issue_rows(i, slots_ref)

    @pl.when(i == n - 1)
    def _():
        @pl.when(i >= 1)
        def _():
            drain_rows(i - 1, slots1_ref)
        drain_rows(i, slots_ref)


def moe_dispatch(h3, slots, fill_start, n_rows, *, tm=256):
    ns = 8
    T = h3.shape[0] // ns
    K = MOE_K
    slot_spec = lambda back: pl.BlockSpec((tm * K,), lambda i, fill: (jnp.maximum(i - back, 0),), memory_space=pltpu.SMEM)
    grid_spec = pltpu.PrefetchScalarGridSpec(
        num_scalar_prefetch=1,
        grid=(T // tm,),
        in_specs=[slot_spec(0), slot_spec(1), slot_spec(2), pl.BlockSpec(memory_space=pl.ANY)],
        out_specs=pl.BlockSpec(memory_space=pl.ANY),
        scratch_shapes=[pltpu.VMEM((MOE_BLK * ns, LANES), F32), pltpu.VMEM((3, tm * ns, LANES), F32),
                        pltpu.SemaphoreType.DMA(()), pltpu.SemaphoreType.DMA((3,)), pltpu.SemaphoreType.DMA((3,))],
    )
    return pl.pallas_call(
        functools.partial(_dispatch_kernel, tm=tm, top_k=K, n_exp=MOE_E, ns=ns),
        grid_spec=grid_spec,
        out_shape=jax.ShapeDtypeStruct((n_rows * ns, LANES), F32),
        compiler_params=_params(("arbitrary",)),
        name="moe_dispatch",
    )(fill_start, slots, slots, slots, h3)


def _expert_kernel(be_ref, nu_ref, nv_ref, x3_ref, w1_ref, b1_ref, w2_ref, b2_ref, o3_ref, w1b_ref, w2b_ref, *, dff, sub):
    i = pl.program_id(0)
    ns = w1b_ref.shape[0] // LANES
    blk = x3_ref.shape[0] // ns

    @pl.when(i < nu_ref[0])
    def _():
        @pl.when(jnp.logical_or(i == 0, be_ref[i] != be_ref[jnp.maximum(i - 1, 0)]))
        def _():
            w1b_ref[...] = w1_ref[0].astype(BF16)
            w2b_ref[...] = w2_ref[0].astype(BF16)

        def mlp(part):
            r0 = part * sub * ns
            xb = jnp.concatenate([x3_ref[pl.ds(r0 + s, sub, stride=ns), :].astype(BF16) for s in range(ns)], axis=-1)
            gu = _dot(xb, w1b_ref[...]) + b1_ref[0]
            gg = jnp.minimum(gu[:, :dff], GLU_LIMIT)
            up = jnp.clip(gu[:, dff:], -GLU_LIMIT, GLU_LIMIT)
            act = (up + 1.0) * (gg * _sigmoid(GLU_ALPHA * gg))
            y = _dot(act.astype(BF16), w2b_ref[...]) + b2_ref[0]
            for s in range(ns):
                o3_ref[pl.ds(r0 + s, sub, stride=ns), :] = y[:, s * LANES:(s + 1) * LANES]

        n_sub = blk // sub
        full = nv_ref[i] > (n_sub - 1) * sub

        @pl.when(full)
        def _():
            for part in range(n_sub):
                mlp(part)

        @pl.when(jnp.logical_not(full))
        def _():
            for part in range(n_sub - 1):
                @pl.when(nv_ref[i] > part * sub)
                def _():
                    mlp(part)


def moe_experts(xs3, block_expert, n_used, n_valid, w1, b1, w2, b2, layer):
    L, E, D, two_f = w1.shape
    ns = D // LANES
    nblk = block_expert.shape[0]
    row_map = lambda i, be, nu, nv: (jnp.minimum(i, nu[0] - 1), 0)
    w_map = lambda i, be, nu, nv: (layer, be[i], 0, 0)
    grid_spec = pltpu.PrefetchScalarGridSpec(
        num_scalar_prefetch=3,
        grid=(nblk,),
        in_specs=[pl.BlockSpec((MOE_BLK * ns, LANES), row_map),
                  pl.BlockSpec((None, 1, D, two_f), w_map),
                  pl.BlockSpec((None, 1, 1, two_f), w_map),
                  pl.BlockSpec((None, 1, two_f // 2, D), w_map),
                  pl.BlockSpec((None, 1, 1, D), w_map)],
        out_specs=pl.BlockSpec((MOE_BLK * ns, LANES), row_map),
        scratch_shapes=[pltpu.VMEM((D, two_f), BF16), pltpu.VMEM((two_f // 2, D), BF16)],
    )
    return pl.pallas_call(
        functools.partial(_expert_kernel, dff=two_f // 2, sub=MOE_SUB),
        grid_spec=grid_spec,
        out_shape=jax.ShapeDtypeStruct((nblk * MOE_BLK * ns, LANES), F32),
        compiler_params=_params(("arbitrary",)),
        name="moe_experts",
    )(block_expert, n_used, n_valid, xs3, w1, b1.reshape(L, E, 1, two_f), w2, b2.reshape(L, E, 1, D))


def _combine_kernel(slots_ref, nslots_ref, out3_ref, gt_ref, res_ref, gate_ref, g_ref, bb_ref, o_ref, y4_ref, sems,
                    *, tm, top_k, alpha, ns):
    i = pl.program_id(0)
    buf = i % 2

    def row_copy(slots, b, r, k):
        return pltpu.make_async_copy(_tile_rows_from(out3_ref, slots[r * top_k + k], ns),
                                     _tile_rows(y4_ref.at[b, k], r, ns), sems.at[b])

    def issue(slots, b):
        def body(r, carry):
            for k in range(top_k):
                row_copy(slots, b, r, k).start(priority=k % 2)
            return carry
        lax.fori_loop(0, tm, body, 0, unroll=8)

    def drain(slots, b):
        def body(r, carry):
            for k in range(top_k):
                row_copy(slots, b, r, k).wait()
            return carry
        lax.fori_loop(0, tm, body, 0, unroll=8)

    @pl.when(i == 0)
    def _():
        issue(slots_ref, 0)

    @pl.when(i + 1 < pl.num_programs(0))
    def _():
        issue(nslots_ref, 1 - buf)

    drain(slots_ref, buf)
    gt = gt_ref[...]
    cols = []
    for s in range(ns):
        acc = gt[:, 0:1] * y4_ref[buf, 0, pl.ds(s, tm, stride=ns), :]
        for k in range(1, top_k):
            acc = acc + gt[:, k:k + 1] * y4_ref[buf, k, pl.ds(s, tm, stride=ns), :]
        cols.append(acc)
    y = jnp.concatenate(cols, axis=-1)
    z = alpha * res_ref[...] + (1.0 + gate_ref[0]) * y
    o_ref[...] = _ln_rows(z, g_ref[...], bb_ref[...])


def moe_combine(out3, slots, gates, res, gate, g, b, *, seq, alpha, tm=256):
    T, D = res.shape
    K = MOE_K
    ns = D // LANES
    nb = seq // tm
    nsteps = T // tm
    one = pl.BlockSpec((1, D), lambda i: (0, 0))
    return pl.pallas_call(
        functools.partial(_combine_kernel, tm=tm, top_k=K, alpha=alpha, ns=ns),
        grid=(nsteps,),
        in_specs=[pl.BlockSpec((tm * K,), lambda i: (i,), memory_space=pltpu.SMEM),
                  pl.BlockSpec((tm * K,), lambda i: (jnp.minimum(i + 1, nsteps - 1),), memory_space=pltpu.SMEM),
                  pl.BlockSpec(memory_space=pl.ANY),
                  pl.BlockSpec((tm, LANES), lambda i: (i, 0)),
                  pl.BlockSpec((tm, D), lambda i: (i, 0)),
                  pl.BlockSpec((1, 1, D), lambda i: (i // nb, 0, 0)), one, one],
        out_specs=pl.BlockSpec((tm, D), lambda i: (i, 0)),
        out_shape=jax.ShapeDtypeStruct((T, D), F32),
        scratch_shapes=[pltpu.VMEM((2, K, tm * ns, LANES), F32), pltpu.SemaphoreType.DMA((2,))],
        compiler_params=_params(("arbitrary",)),
        name="moe_combine",
    )(slots, slots, out3, gates, res, gate, g.reshape(1, D), b.reshape(1, D))


def moe_layer(x, sc, sh, gate, ln_g, ln_b, router_w, router_b, w1, b1, w2, b2, layer, *, seq, alpha):
    T, D = x.shape
    E, K = MOE_E, MOE_K
    h3, meta, gates, cnt = moe_router(x, sc, sh, router_w, router_b, seq=seq)
    counts = cnt[0, :E]
    blocks_per = (counts + MOE_BLK - 1) // MOE_BLK
    block_end = jnp.cumsum(blocks_per)
    pad_start = (block_end - blocks_per) * MOE_BLK
    n_blocks = -(-(T * K) // MOE_BLK) + E
    blk = jnp.arange(n_blocks, dtype=I32)
    block_expert = jnp.minimum(jnp.sum(block_end[None, :] <= blk[:, None], axis=1), E - 1).astype(I32)
    onehot = meta[:, :K, None] == jnp.arange(E, dtype=I32)[None, None, :]
    slots = (meta[:, K:2 * K] + jnp.sum(jnp.where(onehot, pad_start[None, None, :], 0), axis=-1)).reshape(-1).astype(I32)
    slots = slots * (D // LANES)
    fill_start = jnp.where(blocks_per > 0, (block_end - 1) * MOE_BLK, (n_blocks + jnp.arange(E, dtype=I32)) * MOE_BLK).astype(I32)
    first_blk = block_end - blocks_per
    n_valid = jnp.clip(counts[block_expert] - (blk - first_blk[block_expert]) * MOE_BLK, 0, MOE_BLK)
    n_valid = jnp.where(blk < block_end[-1], n_valid, 0).astype(I32)
    xs3 = moe_dispatch(h3, slots, fill_start, (n_blocks + E) * MOE_BLK)
    out3 = moe_experts(xs3, block_expert, block_end[-1:].astype(I32), n_valid, w1, b1, w2, b2, layer)
    return moe_combine(out3, slots, gates, x, gate, ln_g, ln_b, seq=seq, alpha=alpha)


def _rope_tables(positions, d):
    inv = ROPE_BASE ** (-jnp.arange(0, d, 2, dtype=F32) / d)
    ang = positions.astype(F32).reshape(-1)[:, None] * inv
    return jnp.cos(ang), jnp.sin(ang)


def kernel(x, c, positions, ada_w, ada_b, ln1_g, ln1_b, ln2_g, ln2_b, router_w, router_b, moe_w1, moe_b1, moe_w2, moe_b2, ret_w_in, ret_gn_g, ret_gn_b, ret_w_out, conv_w_pw1, conv_b_pw1, conv_w_dw, conv_b_dw, conv_ln_g, conv_ln_b, conv_w_pw2, conv_b_pw2, nsa_w_in, nsa_gate_b, nsa_cmp_pos_k, nsa_cmp_pos_v, nsa_cmp_k_w1, nsa_cmp_k_w2, nsa_cmp_v_w1, nsa_cmp_v_w2, nsa_w_out, pool_w, pool_b, pool_scale):
    B, S, D = x.shape
    T = B * S
    depth = ada_w.shape[0]
    alpha = (2.0 * depth) ** 0.25
    mod = ada_modulation(c, ada_w, ada_b)
    xf = x.reshape(T, D)
    for i in range(depth):
        kind, j = i % 4, i // 4
        sh1, sc1, g1, sh2, sc2, g2 = (mod[i, :, k * D:(k + 1) * D].reshape(B, 1, D) for k in range(6))
        res1 = (g1, ln1_g[i], ln1_b[i])
        if kind == 0:
            dk = ret_w_in.shape[2] // 6 // RET_H
            cos, sin = _rope_tables(positions, dk)
            p = ret_projection(xf, sc1, sh1, ret_w_in[j].astype(BF16), cos, sin, seq=S)
            o = retention_core(p, ret_gn_g[j], ret_gn_b[j], batch=B, seq=S)
            xf = fused_matmul(o, ret_w_out[j].astype(BF16), seq=S, resln=(xf,) + res1, alpha=alpha, name="ret_out")
        elif kind == 1:
            u = conv_glu(xf, sc1, sh1, conv_w_pw1[j].astype(BF16), conv_b_pw1[j], seq=S)
            xf = conv_tail(u, conv_w_dw[j], conv_b_dw[j], conv_ln_g[j], conv_ln_b[j], conv_w_pw2[j].astype(BF16),
                           conv_b_pw2[j], xf, *res1, seq=S, alpha=alpha)
        elif kind == 2:
            G, dh, H = NSA_G, NSA_DH, NSA_H
            cos, sin = _rope_tables(positions, dh)
            cos_p = jnp.tile(cos, (1, LANES // (dh // 2)))
            sin_p = jnp.tile(jnp.concatenate([-sin, sin], axis=1), (1, LANES // dh))
            n_main = H * dh + 6 * G * dh
            ng = 3 * H // G
            w_g = jnp.zeros((D, G, LANES), F32).at[:, :, :ng].set(nsa_w_in[j][:, n_main:].reshape(D, G, ng))
            w_p = jnp.concatenate([nsa_w_in[j][:, :n_main], w_g.reshape(D, G * LANES)], axis=1).astype(BF16)
            gb_p = jnp.zeros((G, LANES), F32).at[:, :ng].set(nsa_gate_b[j].reshape(G, ng)).reshape(1, G * LANES)
            q, cv, kvx, gates = nsa_projection(xf, sc1, sh1, w_p, cos_p, sin_p, gb_p, seq=S)
            cv = cv.reshape(B, S, 2, G, dh).transpose(2, 0, 3, 1, 4).reshape(2, B * G, S // NSA_CMP_STEP, NSA_CMP_STEP * dh)
            k_cmp, v_cmp = nsa_compress(cv[0], cv[1], nsa_cmp_pos_k[j], nsa_cmp_pos_v[j], nsa_cmp_k_w1[j],
                                        nsa_cmp_k_w2[j], nsa_cmp_v_w1[j], nsa_cmp_v_w2[j])
            dup = lambda a: jnp.concatenate([a, a], axis=-1)
            lo = lambda a: jnp.concatenate([a, jnp.zeros_like(a)], axis=-1)
            hi = lambda a: jnp.concatenate([jnp.zeros_like(a), a], axis=-1)
            o = nsa_attention(q.reshape(B, S, H * dh), gates, dup(k_cmp), lo(v_cmp), hi(v_cmp), kvx, batch=B, seq=S)
            xf = fused_matmul(o.reshape(T, H * dh), nsa_w_out[j].astype(BF16), seq=S, resln=(xf,) + res1, alpha=alpha,
                              name="nsa_out")
        else:
            xf = pool_layer(xf, sc1, sh1, pool_w[j], pool_b[j], pool_scale[j], *res1, seq=S, alpha=alpha)
        xf = moe_layer(xf, sc2, sh2, g2, ln2_g[i], ln2_b[i], router_w[i], router_b[i], moe_w1, moe_b1, moe_w2, moe_b2, i,
                       seq=S, alpha=alpha)
    return xf.reshape(B, S, D)
```

```python
import functools
import math

import jax
import jax.numpy as jnp
import numpy as np
from jax import lax
from jax.experimental import pallas as pl
from jax.experimental.pallas import tpu as pltpu

F32 = jnp.float32
BF16 = jnp.bfloat16
I32 = jnp.int32

ROPE_BASE = 10000.0
EPS = 1e-5
NEG_BIG = -1e30

RET_H = 4
RET_C = 128
CONV_W = 31
NSA_H = 16
NSA_G = 2
NSA_DH = 64
NSA_CMP_LEN = 32
NSA_CMP_STEP = 16
NSA_SEL_LEN = 64
NSA_TOPN = 16
NSA_WIN = 512
NSA_FORCED = 1e6
POOL_WINS = (2, 4, 8, 16)
MOE_E = 32
MOE_K = 4
MOE_BLK = 512
MOE_SUB = 256
GLU_LIMIT = 7.0
GLU_ALPHA = 1.702

V7X_VMEM_LIMIT = 56 * 1024 * 1024
LANES = 128


def _params(sem):
    return pltpu.CompilerParams(dimension_semantics=sem, vmem_limit_bytes=V7X_VMEM_LIMIT)


def _ln_rows(v, g, b):
    mu = jnp.mean(v, axis=-1, keepdims=True)
    d = v - mu
    var = jnp.mean(d * d, axis=-1, keepdims=True)
    return d * lax.rsqrt(var + EPS) * g + b


def _sigmoid(v):
    return 1.0 / (1.0 + jnp.exp(-v))


def _dot(a, b):
    return jnp.dot(a, b, preferred_element_type=F32)


def _dot_nt(a, b):
    return lax.dot_general(a, b, (((1,), (1,)), ((), ())), preferred_element_type=F32)


def _dot_tn(a, b):
    return lax.dot_general(a, b, (((0,), (0,)), ((), ())), preferred_element_type=F32)


def _ada_kernel(c_ref, w_ref, b_ref, o_ref):
    c = c_ref[...]
    ca = c * _sigmoid(c)
    o_ref[0] = _dot(ca.astype(BF16), w_ref[0].astype(BF16)) + b_ref[0]


def ada_modulation(c, ada_w, ada_b):
    L, D, N = ada_w.shape
    B = c.shape[0]
    tn = 2048 if N % 2048 == 0 else N
    return pl.pallas_call(
        _ada_kernel,
        grid=(L, N // tn),
        in_specs=[pl.BlockSpec((B, D), lambda l, j: (0, 0)),
                  pl.BlockSpec((1, D, tn), lambda l, j: (l, 0, j)),
                  pl.BlockSpec((1, 1, tn), lambda l, j: (l, 0, j))],
        out_specs=pl.BlockSpec((1, B, tn), lambda l, j: (l, 0, j)),
        out_shape=jax.ShapeDtypeStruct((L, B, N), F32),
        compiler_params=_params(("parallel", "parallel")),
        name="ada_modulation",
    )(c, ada_w, ada_b.reshape(L, 1, N))


def _mm_kernel(*refs, has_mod, has_bias, epi, alpha):
    it = iter(refs)
    x_ref = next(it)
    if has_mod:
        sc_ref, sh_ref = next(it), next(it)
    w_ref = next(it)
    if has_bias:
        b_ref = next(it)
    if epi == "resln":
        res_ref, gate_ref, g_ref, bb_ref = next(it), next(it), next(it), next(it)
    o_ref = next(it)
    x = x_ref[...]
    if has_mod:
        x = x.astype(F32) * (1.0 + sc_ref[0]) + sh_ref[0]
    acc = _dot(x.astype(BF16), w_ref[...])
    if has_bias:
        acc = acc + b_ref[...]
    if epi == "resln":
        v = alpha * res_ref[...] + (1.0 + gate_ref[0]) * acc
        acc = _ln_rows(v, g_ref[...], bb_ref[...])
    o_ref[...] = acc.astype(o_ref.dtype)


def fused_matmul(x, w, *, seq, mod=None, bias=None, resln=None, alpha=1.0, out_dtype=F32, tm=512, tn=None, name="mm"):
    T, K = x.shape
    N = w.shape[1]
    tn = N if tn is None else tn
    assert T % tm == 0 and seq % tm == 0 and N % tn == 0
    nb = seq // tm
    args, specs = [x], [pl.BlockSpec((tm, K), lambda i, j: (i, 0))]
    if mod is not None:
        for m in mod:
            args.append(m)
            specs.append(pl.BlockSpec((1, 1, K), lambda i, j: (i // nb, 0, 0)))
    args.append(w)
    specs.append(pl.BlockSpec((K, tn), lambda i, j: (0, j)))
    if bias is not None:
        args.append(bias.reshape(1, N))
        specs.append(pl.BlockSpec((1, tn), lambda i, j: (0, j)))
    epi = "none"
    if resln is not None:
        assert tn == N
        res, gate, g, b = resln
        epi = "resln"
        args += [res, gate, g.reshape(1, N), b.reshape(1, N)]
        specs += [pl.BlockSpec((tm, N), lambda i, j: (i, 0)),
                  pl.BlockSpec((1, 1, N), lambda i, j: (i // nb, 0, 0)),
                  pl.BlockSpec((1, N), lambda i, j: (0, 0)),
                  pl.BlockSpec((1, N), lambda i, j: (0, 0))]
    kern = functools.partial(_mm_kernel, has_mod=mod is not None, has_bias=bias is not None, epi=epi, alpha=alpha)
    return pl.pallas_call(
        kern,
        grid=(T // tm, N // tn),
        in_specs=specs,
        out_specs=pl.BlockSpec((tm, tn), lambda i, j: (i, j)),
        out_shape=jax.ShapeDtypeStruct((T, N), out_dtype),
        compiler_params=_params(("parallel", "parallel")),
        name=name,
    )(*args)


def _ret_proj_kernel(x_ref, sc_ref, sh_ref, w_ref, cos_ref, sin_ref, o_ref, *, dk, hdk, tn, kscale):
    h = (x_ref[...] * (1.0 + sc_ref[0]) + sh_ref[0]).astype(BF16)
    cos, sin = cos_ref[...], sin_ref[...]
    half = dk // 2
    for j in range(o_ref.shape[1] // tn):
        c0 = j * tn
        acc = _dot(h, w_ref[:, c0:c0 + tn])
        if c0 < 2 * hdk:
            scale = 1.0 if c0 < hdk else kscale
            for hh in range(tn // dk):
                x1 = acc[:, hh * dk:hh * dk + half]
                x2 = acc[:, hh * dk + half:(hh + 1) * dk]
                o_ref[:, c0 + hh * dk:c0 + hh * dk + half] = ((x1 * cos - x2 * sin) * scale).astype(o_ref.dtype)
                o_ref[:, c0 + hh * dk + half:c0 + (hh + 1) * dk] = ((x1 * sin + x2 * cos) * scale).astype(o_ref.dtype)
        elif c0 < 4 * hdk:
            o_ref[:, c0:c0 + tn] = acc.astype(o_ref.dtype)
        else:
            o_ref[:, c0:c0 + tn] = (acc * _sigmoid(acc)).astype(o_ref.dtype)


def ret_projection(x, sc, sh, w_in, cos, sin, *, seq, tm=512, tn=1024):
    T, K = x.shape
    N = w_in.shape[1]
    hdk = N // 6
    dk = hdk // RET_H
    assert hdk % tn == 0 and tn % dk == 0
    nb = seq // tm
    kern = functools.partial(_ret_proj_kernel, dk=dk, hdk=hdk, tn=tn, kscale=dk ** -0.5)
    return pl.pallas_call(
        kern,
        grid=(T // tm,),
        in_specs=[pl.BlockSpec((tm, K), lambda i: (i, 0)),
                  pl.BlockSpec((1, 1, K), lambda i: (i // nb, 0, 0)),
                  pl.BlockSpec((1, 1, K), lambda i: (i // nb, 0, 0)),
                  pl.BlockSpec((K, N), lambda i: (0, 0), pipeline_mode=pl.Buffered(1)),
                  pl.BlockSpec((tm, dk // 2), lambda i: (i, 0)),
                  pl.BlockSpec((tm, dk // 2), lambda i: (i, 0))],
        out_specs=pl.BlockSpec((tm, N), lambda i: (i, 0)),
        out_shape=jax.ShapeDtypeStruct((T, N), BF16),
        compiler_params=_params(("parallel",)),
        name="ret_projection",
    )(x, sc, sh, w_in, cos, sin)


def _retention_kernel(q_ref, k_ref, v_ref, sg_ref, din_ref, dq_ref, dkk_ref, dch_ref, gg_ref, gb_ref,
                      o_ref, state_ref, *, H, C, dk, dv):
    @pl.when(pl.program_id(1) == 0)
    def _():
        state_ref[...] = jnp.zeros_like(state_ref)

    n_chunks = q_ref.shape[0] // C
    for c in range(n_chunks):
        rows = slice(c * C, (c + 1) * C)
        for hh in range(H):
            qc = q_ref[rows, hh * dk:(hh + 1) * dk]
            kc = k_ref[rows, hh * dk:(hh + 1) * dk]
            vc = v_ref[rows, hh * dv:(hh + 1) * dv]
            st = state_ref[hh]
            a = _dot_nt(qc, kc) * din_ref[hh]
            o = _dot(a.astype(BF16), vc)
            qd = (qc.astype(F32) * dq_ref[hh]).astype(BF16)
            o = o + _dot(qd, st.astype(BF16))
            kd = (kc.astype(F32) * dkk_ref[hh]).astype(BF16)
            state_ref[hh] = st * dch_ref[hh, :, 0:1] + _dot_tn(kd, vc)
            mu = jnp.mean(o, axis=-1, keepdims=True)
            d = o - mu
            var = jnp.mean(d * d, axis=-1, keepdims=True)
            on = d * lax.rsqrt(var + EPS) * gg_ref[:, hh * dv:(hh + 1) * dv] + gb_ref[:, hh * dv:(hh + 1) * dv]
            sg = sg_ref[rows, hh * dv:(hh + 1) * dv].astype(F32)
            o_ref[rows, hh * dv:(hh + 1) * dv] = (sg * on).astype(o_ref.dtype)


def retention_core(p, gn_g, gn_b, *, batch, seq, tb=512):
    T, N = p.shape
    H, C = RET_H, RET_C
    hdk = N // 6
    dk, dv = hdk // H, 2 * hdk // H
    hdv = H * dv
    nb = seq // tb
    log_gamma = jnp.log1p(-(2.0 ** (-5.0 - jnp.arange(H, dtype=F32))))
    i = jnp.arange(C, dtype=F32)
    rel = i[:, None] - i[None, :]
    d_intra = jnp.where(rel >= 0, jnp.exp(log_gamma[:, None, None] * jnp.maximum(rel, 0.0)), 0.0).astype(F32)
    d_q = jnp.exp(log_gamma[:, None] * (i + 1.0)).astype(F32)[..., None]
    d_k = jnp.exp(log_gamma[:, None] * (C - 1.0 - i)).astype(F32)[..., None]
    d_ch = jnp.broadcast_to(jnp.exp(log_gamma * C).astype(F32)[:, None, None], (H, 1, LANES))
    kern = functools.partial(_retention_kernel, H=H, C=C, dk=dk, dv=dv)
    return pl.pallas_call(
        kern,
        grid=(batch, nb),
        in_specs=[pl.BlockSpec((tb, hdk), lambda b, s: (b * nb + s, 0)),
                  pl.BlockSpec((tb, hdk), lambda b, s: (b * nb + s, 1)),
                  pl.BlockSpec((tb, hdv), lambda b, s: (b * nb + s, 1)),
                  pl.BlockSpec((tb, hdv), lambda b, s: (b * nb + s, 2)),
                  pl.BlockSpec((H, C, C), lambda b, s: (0, 0, 0)),
                  pl.BlockSpec((H, C, 1), lambda b, s: (0, 0, 0)),
                  pl.BlockSpec((H, C, 1), lambda b, s: (0, 0, 0)),
                  pl.BlockSpec((H, 1, LANES), lambda b, s: (0, 0, 0)),
                  pl.BlockSpec((1, hdv), lambda b, s: (0, 0)),
                  pl.BlockSpec((1, hdv), lambda b, s: (0, 0))],
        out_specs=pl.BlockSpec((tb, hdv), lambda b, s: (b * nb + s, 0)),
        out_shape=jax.ShapeDtypeStruct((T, hdv), BF16),
        scratch_shapes=[pltpu.VMEM((H, dk, dv), F32)],
        compiler_params=_params(("parallel", "arbitrary")),
        name="retention_core",
    )(p, p, p, p, d_intra, d_q, d_k, d_ch, gn_g.reshape(1, hdv), gn_b.reshape(1, hdv))


def _glu_kernel(x_ref, sc_ref, sh_ref, wa_ref, wg_ref, ba_ref, bg_ref, o_ref):
    h = (x_ref[...] * (1.0 + sc_ref[0]) + sh_ref[0]).astype(BF16)
    a = _dot(h, wa_ref[...]) + ba_ref[...]
    g = _dot(h, wg_ref[...]) + bg_ref[...]
    o_ref[...] = a * _sigmoid(g)


def conv_glu(x, sc, sh, w_pw1, b_pw1, *, seq, tm=512, tn=512):
    T, K = x.shape
    N = w_pw1.shape[1] // 2
    nb = seq // tm
    ng = N // tn
    b2 = b_pw1.reshape(1, 2 * N)
    return pl.pallas_call(
        _glu_kernel,
        grid=(T // tm, ng),
        in_specs=[pl.BlockSpec((tm, K), lambda i, j: (i, 0)),
                  pl.BlockSpec((1, 1, K), lambda i, j: (i // nb, 0, 0)),
                  pl.BlockSpec((1, 1, K), lambda i, j: (i // nb, 0, 0)),
                  pl.BlockSpec((K, tn), lambda i, j: (0, j)),
                  pl.BlockSpec((K, tn), lambda i, j: (0, j + ng)),
                  pl.BlockSpec((1, tn), lambda i, j: (0, j)),
                  pl.BlockSpec((1, tn), lambda i, j: (0, j + ng))],
        out_specs=pl.BlockSpec((tm, tn), lambda i, j: (i, j)),
        out_shape=jax.ShapeDtypeStruct((T, N), F32),
        compiler_params=_params(("parallel", "parallel")),
        name="conv_glu",
    )(x, sc, sh, w_pw1, w_pw1, b2, b2)


def _conv_tail_kernel(u_ref, up_ref, wdw_ref, bdw_ref, lg_ref, lb_ref, w2_ref, b2_ref,
                      res_ref, gate_ref, g_ref, bb_ref, o_ref, win_ref, shift_ref, *, nb, halo, width, alpha):
    tm = u_ref.shape[0]
    first = (pl.program_id(0) % nb) == 0
    prev = up_ref[...]
    win_ref[0:halo, :] = jnp.where(first, jnp.zeros_like(prev), prev)
    win_ref[halo:halo + tm, :] = u_ref[...]
    for b in range(1, 8):
        n = (tm + halo - b) // 8 * 8
        shift_ref[b - 1, 0:n, :] = win_ref[b:b + n, :]
    acc = jnp.zeros(u_ref.shape, F32)
    off = halo - (width - 1)
    for k in range(width):
        b, a = (off + k) % 8, (off + k) // 8 * 8
        rows = win_ref[a:a + tm, :] if b == 0 else shift_ref[b - 1, a:a + tm, :]
        acc = acc + wdw_ref[k:k + 1, :] * rows
    acc = acc + bdw_ref[...]
    v = _ln_rows(acc, lg_ref[...], lb_ref[...])
    v = v * _sigmoid(v)
    y = _dot(v.astype(BF16), w2_ref[...]) + b2_ref[...]
    z = alpha * res_ref[...] + (1.0 + gate_ref[0]) * y
    o_ref[...] = _ln_rows(z, g_ref[...], bb_ref[...])


def conv_tail(u, w_dw, b_dw, ln_g, ln_b, w_pw2, b_pw2, res, gate, g, b, *, seq, alpha, tm=512):
    T, D = u.shape
    nb = seq // tm
    halo = 32
    assert CONV_W - 1 <= halo and tm % halo == 0
    r = tm // halo
    wpad = jnp.zeros((halo, D), F32).at[:CONV_W].set(w_dw)
    kern = functools.partial(_conv_tail_kernel, nb=nb, halo=halo, width=CONV_W, alpha=alpha)
    vec = lambda a: a.reshape(1, D)
    one = pl.BlockSpec((1, D), lambda i: (0, 0))
    return pl.pallas_call(
        kern,
        grid=(T // tm,),
        in_specs=[pl.BlockSpec((tm, D), lambda i: (i, 0)),
                  pl.BlockSpec((halo, D), lambda i: (jnp.maximum(i * r - 1, 0), 0)),
                  pl.BlockSpec((halo, D), lambda i: (0, 0)),
                  one, one, one,
                  pl.BlockSpec((D, D), lambda i: (0, 0)),
                  one,
                  pl.BlockSpec((tm, D), lambda i: (i, 0)),
                  pl.BlockSpec((1, 1, D), lambda i: (i // nb, 0, 0)),
                  one, one],
        out_specs=pl.BlockSpec((tm, D), lambda i: (i, 0)),
        out_shape=jax.ShapeDtypeStruct((T, D), F32),
        scratch_shapes=[pltpu.VMEM((halo + tm, D), F32), pltpu.VMEM((7, halo + tm, D), F32)],
        compiler_params=_params(("parallel",)),
        name="conv_tail",
    )(u, u, wpad, vec(b_dw), vec(ln_g), vec(ln_b), w_pw2, vec(b_pw2), res, gate, vec(g), vec(b))


def _pool_kernel(x_ref, xp_ref, sc_ref, sh_ref, w_ref, pb_ref, ps_ref, gate_ref, g_ref, bb_ref,
                 o_ref, win_ref, *, nb, halo, wins, alpha):
    tm, D = x_ref.shape
    gw = D // len(wins)
    i = pl.program_id(0)
    first = (i % nb) == 0
    x = x_ref[...]
    sc, sh = sc_ref[0], sh_ref[0]
    hp = xp_ref[...] * (1.0 + sc) + sh
    win_ref[0:halo, :] = jnp.where(first, jnp.zeros_like(hp), hp)
    win_ref[halo:halo + tm, :] = x * (1.0 + sc) + sh
    t = (i % nb) * tm + lax.broadcasted_iota(I32, (tm, gw), 0)
    ys = []
    for gi, wn in enumerate(wins):
        cols = slice(gi * gw, (gi + 1) * gw)
        s = win_ref[halo:halo + tm, cols]
        hcur = s
        for k in range(1, wn):
            s = s + win_ref[halo - k:halo - k + tm, cols]
        cnt = jnp.minimum(t + 1, wn).astype(F32)
        pooled = s / cnt - hcur
        ys.append(_dot(pooled.astype(BF16), w_ref[gi]))
    y = (jnp.concatenate(ys, axis=-1) + pb_ref[...]) * ps_ref[...]
    z = alpha * x + (1.0 + gate_ref[0]) * y
    o_ref[...] = _ln_rows(z, g_ref[...], bb_ref[...])


def pool_layer(x, sc, sh, pool_w, pool_b, pool_scale, gate, g, b, *, seq, alpha, tm=512):
    T, D = x.shape
    nb = seq // tm
    halo = 16
    assert max(POOL_WINS) <= halo and tm % halo == 0
    r = tm // halo
    G, gw, _ = pool_w.shape
    kern = functools.partial(_pool_kernel, nb=nb, halo=halo, wins=POOL_WINS, alpha=alpha)
    vec = lambda a: a.reshape(1, D)
    one = pl.BlockSpec((1, D), lambda i: (0, 0))
    per_b = pl.BlockSpec((1, 1, D), lambda i: (i // nb, 0, 0))
    return pl.pallas_call(
        kern,
        grid=(T // tm,),
        in_specs=[pl.BlockSpec((tm, D), lambda i: (i, 0)),
                  pl.BlockSpec((halo, D), lambda i: (jnp.maximum(i * r - 1, 0), 0)),
                  per_b, per_b,
                  pl.BlockSpec((G, gw, gw), lambda i: (0, 0, 0)),
                  one, one, per_b, one, one],
        out_specs=pl.BlockSpec((tm, D), lambda i: (i, 0)),
        out_shape=jax.ShapeDtypeStruct((T, D), F32),
        scratch_shapes=[pltpu.VMEM((halo + tm, D), F32)],
        compiler_params=_params(("parallel",)),
        name="pool_layer",
    )(x, x, sc, sh, pool_w.astype(BF16), vec(pool_b), vec(pool_scale), gate, vec(g), vec(b))


def _rope64(x, cos, sin_signed, lane):
    partner = jnp.where((lane & (NSA_DH - 1)) < NSA_DH // 2, pltpu.roll(x, LANES - NSA_DH // 2, 1), pltpu.roll(x, NSA_DH // 2, 1))
    return x * cos + partner * sin_signed


def _nsa_proj_kernel(x_ref, sc_ref, sh_ref, w_ref, cos_ref, sin_ref, gb_ref, q_ref, cv_ref, kvx_ref, gt_ref, *, nq, qscale):
    h = x_ref[...] * (1.0 + sc_ref[0]) + sh_ref[0]
    acc = _dot(h.astype(BF16), w_ref[...])
    cos, sin = cos_ref[...], sin_ref[...]
    lane = lax.broadcasted_iota(I32, cos.shape, 1)
    for s in range(nq):
        blk = acc[:, s * LANES:(s + 1) * LANES]
        q_ref[:, s * LANES:(s + 1) * LANES] = (_rope64(blk, cos, sin, lane) * qscale).astype(q_ref.dtype)
    low = lane < NSA_DH
    for s in range(6):
        blk = acc[:, (nq + s) * LANES:(nq + s + 1) * LANES]
        if s % 2 == 0:
            blk = _rope64(blk, cos, sin, lane)
        if s < 2:
            cv_ref[:, s * LANES:(s + 1) * LANES] = blk.astype(cv_ref.dtype)
            continue
        swapped = pltpu.roll(blk, NSA_DH, 1)
        for gi in range(2):
            own, other = (blk, swapped) if gi == 0 else (swapped, blk)
            if s % 2 == 0:
                kvx_ref[3 * (s // 2 - 1), 0, gi] = jnp.where(low, own, other).astype(kvx_ref.dtype)
            else:
                v_lo = jnp.where(lane == NSA_DH, 1.0, jnp.where(low, own, 0.0))
                v_hi = jnp.where(lane == 0, 1.0, jnp.where(low, 0.0, other))
                kvx_ref[3 * (s // 2 - 1) + 1, 0, gi] = v_lo.astype(kvx_ref.dtype)
                kvx_ref[3 * (s // 2 - 1) + 2, 0, gi] = v_hi.astype(kvx_ref.dtype)
    for gi in range(gt_ref.shape[1] // LANES):
        gl = acc[:, (nq + 6 + gi) * LANES:(nq + 7 + gi) * LANES]
        gt_ref[:, gi * LANES:(gi + 1) * LANES] = _sigmoid(gl + gb_ref[:, gi * LANES:(gi + 1) * LANES])


def nsa_projection(x, sc, sh, w_in_p, cos, sin_signed, gate_b_p, *, seq, tm=512):
    T, K = x.shape
    Np = w_in_p.shape[1]
    nq = NSA_H * NSA_DH // LANES
    assert NSA_G * NSA_DH == LANES
    ng = NSA_G * LANES
    nb = seq // tm
    kern = functools.partial(_nsa_proj_kernel, nq=nq, qscale=NSA_DH ** -0.5)
    per_b = pl.BlockSpec((1, 1, K), lambda i: (i // nb, 0, 0))
    return pl.pallas_call(
        kern,
        grid=(T // tm,),
        in_specs=[pl.BlockSpec((tm, K), lambda i: (i, 0)), per_b, per_b,
                  pl.BlockSpec((K, Np), lambda i: (0, 0)),
                  pl.BlockSpec((tm, LANES), lambda i: (i, 0)),
                  pl.BlockSpec((tm, LANES), lambda i: (i, 0)),
                  pl.BlockSpec((1, ng), lambda i: (0, 0))],
        out_specs=[pl.BlockSpec((tm, nq * LANES), lambda i: (i, 0)),
                   pl.BlockSpec((tm, 2 * LANES), lambda i: (i, 0)),
                   pl.BlockSpec((6, 1, NSA_G, tm, LANES), lambda i: (0, i // nb, 0, i % nb, 0)),
                   pl.BlockSpec((tm, ng), lambda i: (i, 0))],
        out_shape=[jax.ShapeDtypeStruct((T, nq * LANES), BF16),
                   jax.ShapeDtypeStruct((T, 2 * LANES), BF16),
                   jax.ShapeDtypeStruct((6, T // seq, NSA_G, seq, LANES), BF16),
                   jax.ShapeDtypeStruct((T, ng), F32)],
        compiler_params=_params(("parallel",)),
        name="nsa_projection",
    )(x, sc, sh, w_in_p, cos, sin_signed, gate_b_p)


def _gelu_tanh(v):
    return 0.5 * v * (1.0 + jnp.tanh(math.sqrt(2.0 / math.pi) * (v + 0.044715 * v * v * v)))


def _compress_kernel(xk_ref, xv_ref, pk_ref, pv_ref, kw1_ref, kw2_ref, vw1_ref, vw2_ref, ok_ref, ov_ref):
    def one(x_ref, pe_ref, w1_ref, w2_ref, o_ref):
        x = x_ref[0].astype(F32)
        n = x.shape[0]
        half = x.shape[1]
        a = _dot((x + pe_ref[0:1, :]).astype(BF16), w1_ref[0:half, :])
        b = _dot((x + pe_ref[1:2, :]).astype(BF16), w1_ref[half:2 * half, :])
        pre = a + pltpu.roll(b, n - 1, 0)
        o_ref[0] = _dot(_gelu_tanh(pre).astype(BF16), w2_ref[...]).astype(o_ref.dtype)

    one(xk_ref, pk_ref, kw1_ref, kw2_ref, ok_ref)
    one(xv_ref, pv_ref, vw1_ref, vw2_ref, ov_ref)


def nsa_compress(kc_chunks, vc_chunks, pos_k, pos_v, k_w1, k_w2, v_w1, v_w2):
    BG, n, width = kc_chunks.shape
    hid = k_w1.shape[1]
    dh = k_w2.shape[1]
    full = lambda a: pl.BlockSpec(a.shape, lambda i: (0,) * a.ndim)
    pk = pos_k.reshape(2, width)
    pv = pos_v.reshape(2, width)
    ws = [k_w1.astype(BF16), k_w2.astype(BF16), v_w1.astype(BF16), v_w2.astype(BF16)]
    del hid
    return pl.pallas_call(
        _compress_kernel,
        grid=(BG,),
        in_specs=[pl.BlockSpec((1, n, width), lambda i: (i, 0, 0)),
                  pl.BlockSpec((1, n, width), lambda i: (i, 0, 0)),
                  full(pk), full(pv)] + [full(w) for w in ws],
        out_specs=[pl.BlockSpec((1, n, dh), lambda i: (i, 0, 0)),
                   pl.BlockSpec((1, n, dh), lambda i: (i, 0, 0))],
        out_shape=[jax.ShapeDtypeStruct((BG, n, dh), BF16), jax.ShapeDtypeStruct((BG, n, dh), BF16)],
        compiler_params=_params(("parallel",)),
        name="nsa_compress",
    )(kc_chunks, vc_chunks, pk, pv, *ws)


def _nsa_attn_kernel(q_ref, gt_ref, kc_ref, vcl_ref, vch_ref, ks_ref, vsl_ref, vsh_ref, kw_ref, vwl_ref, vwh_ref,
                     ov_ref, o_ref, qs_ref, m_ref, ae_ref, ao_ref, res_ref, sc_ref,
                     *, qt, ck, n_sel, top_n, sel_shift, cmp_len, cmp_step, window):
    t0 = pl.program_id(2) * qt
    R = q_ref.shape[2] // NSA_DH
    rows = R * qt
    half_rows = rows // 2
    lane = lax.broadcasted_iota(I32, (qt, LANES), 1)

    for p in range(R // 2):
        slab = q_ref[0, :, p * LANES:(p + 1) * LANES]
        qs_ref[p * qt:(p + 1) * qt, :] = jnp.where(lane < NSA_DH, slab, jnp.zeros_like(slab))
        qs_ref[half_rows + p * qt:half_rows + (p + 1) * qt, :] = jnp.where(lane >= NSA_DH, slab, jnp.zeros_like(slab))
    qs = qs_ref[...]
    tq = t0 + lax.broadcasted_iota(I32, (qt, 1), 0)

    ncmp = kc_ref.shape[0]
    cmp_end = lax.broadcasted_iota(I32, (qt, ncmp), 1) * cmp_step + (cmp_len - 1)
    s3 = _dot_nt(qs, kc_ref[...]).reshape(R, qt, ncmp) + jnp.where(cmp_end <= tq, 0.0, NEG_BIG)
    e3 = jnp.exp(s3 - jnp.max(s3, axis=-1, keepdims=True))
    row_ok = jnp.where(tq >= cmp_len - 1, 1.0, 0.0)
    pc3 = e3 * (row_ok / jnp.sum(e3, axis=-1, keepdims=True))
    pcb = pc3.reshape(rows, ncmp).astype(BF16)
    res_ref[0, 0:half_rows, :] = _dot(pcb[0:half_rows], vcl_ref[...])
    res_ref[0, half_rows:rows, :] = _dot(pcb[half_rows:], vch_ref[...])
    psum = jnp.sum(pc3, axis=0)
    imp = _dot(psum.astype(BF16), ov_ref[...])

    cur = lax.shift_right_logical(tq, sel_shift)
    dcur = cur - lane
    forced = (lane == 0) | (dcur == 0) | (dcur == 1)
    score = jnp.where(lax.shift_left(lane, sel_shift) <= tq, jnp.where(forced, NSA_FORCED, imp), NEG_BIG)
    score = jnp.where(lane < n_sel, score, -jnp.inf)
    st = score.T
    blk_f = lax.broadcasted_iota(I32, st.shape, 0).astype(F32)
    sel_t = jnp.zeros(st.shape, F32)
    for _ in range(top_n):
        mx = jnp.max(st, axis=0, keepdims=True)
        idx = jnp.min(jnp.where(st == mx, blk_f, float(LANES)), axis=0, keepdims=True)
        hit = blk_f == idx
        sel_t = jnp.where(hit, jnp.where(mx > 0.5 * NEG_BIG, 1.0, sel_t), sel_t)
        st = jnp.where(hit, -jnp.inf, st)
    selb = sel_t.T.astype(BF16)

    def attend(k_ref, vl_ref, vh_ref, c_lo, c_hi, bias_fn, slot):
        m_ref[...] = jnp.full(m_ref.shape, NEG_BIG, F32)
        ae_ref[...] = jnp.zeros(ae_ref.shape, F32)
        ao_ref[...] = jnp.zeros(ao_ref.shape, F32)

        def scores(c):
            start = pl.multiple_of(c * ck, ck)
            return _dot_nt(qs, k_ref[pl.ds(start, ck), :]).reshape(R, qt, ck) + bias_fn(start)

        sc_ref[...] = scores(c_lo)

        def body(c, carry):
            start = pl.multiple_of(c * ck, ck)
            sc_next = scores(jnp.minimum(c + 1, c_hi - 1))
            sc3 = sc_ref[...]
            m_old = m_ref[...]
            m_new = jnp.maximum(m_old, jnp.max(sc3, axis=-1, keepdims=True).reshape(rows, 1))
            alpha = jnp.exp(m_old - m_new)
            p3 = jnp.exp(sc3 - m_new.reshape(R, qt, 1))
            m_ref[...] = m_new
            pb = p3.reshape(rows, ck).astype(BF16)
            ae_ref[...] = alpha[0:half_rows] * ae_ref[...] + _dot(pb[0:half_rows], vl_ref[pl.ds(start, ck), :])
            ao_ref[...] = alpha[half_rows:] * ao_ref[...] + _dot(pb[half_rows:], vh_ref[pl.ds(start, ck), :])
            sc_ref[...] = sc_next
            return carry

        lax.fori_loop(c_lo, c_hi, body, 0)
        lane_h = lax.broadcasted_iota(I32, (half_rows, LANES), 1)
        ae, ao = ae_ref[...], ao_ref[...]
        res_ref[slot, 0:half_rows, :] = jnp.where(lane_h < NSA_DH, ae / ae[:, NSA_DH:NSA_DH + 1], 0.0)
        res_ref[slot, half_rows:rows, :] = jnp.where(lane_h >= NSA_DH, ao / ao[:, 0:1], 0.0)

    def key_dist(start):
        return tq - (start + lax.broadcasted_iota(I32, (qt, ck), 1))

    def sel_bias(start):
        jrow = lax.broadcasted_iota(I32, (LANES, ck), 0)
        jcol = lax.shift_right_logical(start + lax.broadcasted_iota(I32, (LANES, ck), 1), sel_shift)
        expand = jnp.where(jrow == jcol, 1.0, 0.0).astype(BF16)
        picked = _dot(selb, expand)
        return jnp.where(key_dist(start) >= 0, jnp.where(picked > 0.5, 0.0, NEG_BIG), NEG_BIG)

    def win_bias(start):
        dist = key_dist(start)
        return jnp.where(jnp.where(dist >= 0, dist, window) < window, 0.0, NEG_BIG)

    c_hi = (t0 + qt + ck - 1) // ck
    attend(ks_ref, vsl_ref, vsh_ref, 0, c_hi, sel_bias, 1)
    attend(kw_ref, vwl_ref, vwh_ref, jnp.maximum(t0 - window + 1, 0) // ck, c_hi, win_bias, 2)

    gt = gt_ref[...]
    for p in range(R // 2):
        acc = jnp.zeros((qt, LANES), F32)
        for par in range(2):
            rsl = slice(par * half_rows + p * qt, par * half_rows + (p + 1) * qt)
            for br in range(3):
                col = (2 * p + par) * 3 + br
                acc = acc + gt[:, col:col + 1] * res_ref[br, rsl, :]
        o_ref[0, :, p * LANES:(p + 1) * LANES] = acc.astype(o_ref.dtype)


def nsa_attention(q, gates, kcd, vcl, vch, kvx, *, batch, seq, qt=256, ck=512):
    G = NSA_G
    R = NSA_H // G
    gq = R * NSA_DH
    n_sel = seq // NSA_SEL_LEN
    ncmp = kcd.shape[1]
    nq = seq // qt
    ov = np.zeros((ncmp, LANES), np.float32)
    for n in range(ncmp):
        for j in range(n_sel):
            if n * NSA_CMP_STEP < (j + 1) * NSA_SEL_LEN and n * NSA_CMP_STEP + NSA_CMP_LEN > j * NSA_SEL_LEN:
                ov[n, j] = 1.0
    n_valid = (seq - NSA_CMP_LEN) // NSA_CMP_STEP + 1
    ov[n_valid:] = 0.0
    kern = functools.partial(_nsa_attn_kernel, qt=qt, ck=ck, n_sel=n_sel, top_n=min(NSA_TOPN, n_sel),
                             sel_shift=int(math.log2(NSA_SEL_LEN)), cmp_len=NSA_CMP_LEN, cmp_step=NSA_CMP_STEP, window=NSA_WIN)
    cmp_spec = pl.BlockSpec((None, ncmp, LANES), lambda b, g, i: (b * G + g, 0, 0))
    seq_specs = [pl.BlockSpec((None, None, None, seq, LANES), functools.partial(lambda b, g, i, k: (k, b, g, 0, 0), k=k))
                 for k in range(6)]
    rows = R * qt
    return pl.pallas_call(
        kern,
        grid=(batch, G, nq),
        in_specs=[pl.BlockSpec((1, qt, gq), lambda b, g, i: (b, i, g)),
                  pl.BlockSpec((qt, LANES), lambda b, g, i: (b * nq + i, g)),
                  cmp_spec, cmp_spec, cmp_spec] + seq_specs + [
                  pl.BlockSpec((ncmp, LANES), lambda b, g, i: (0, 0))],
        out_specs=pl.BlockSpec((1, qt, gq), lambda b, g, i: (b, i, g)),
        out_shape=jax.ShapeDtypeStruct((batch, seq, NSA_H * NSA_DH), BF16),
        scratch_shapes=[pltpu.VMEM((rows, LANES), BF16),
                        pltpu.VMEM((rows, 1), F32),
                        pltpu.VMEM((rows // 2, LANES), F32), pltpu.VMEM((rows // 2, LANES), F32),
                        pltpu.VMEM((3, rows, LANES), F32),
                        pltpu.VMEM((R, qt, ck), F32)],
        compiler_params=_params(("parallel", "parallel", "arbitrary")),
        name="nsa_attention",
    )(q, gates, kcd, vcl, vch, kvx, kvx, kvx, kvx, kvx, kvx, jnp.asarray(ov, BF16))


def _tile_rows(ref, idx, ns):
    return ref.at[pl.ds(pl.multiple_of(idx * ns, ns), ns)]


def _tile_rows_from(ref, row0, ns):
    return ref.at[pl.ds(pl.multiple_of(row0, ns), ns)]


def _router_kernel(x_ref, sc_ref, sh_ref, w_ref, b_ref, h3_ref, meta_ref, gate_ref, cnt_ref, seen_ref, seen_col_ref,
                   *, top_k):
    @pl.when(pl.program_id(0) == 0)
    def _():
        seen_ref[...] = jnp.zeros_like(seen_ref)
        seen_col_ref[...] = jnp.zeros_like(seen_col_ref)

    h = x_ref[...] * (1.0 + sc_ref[0]) + sh_ref[0]
    tm = h.shape[0]
    ns = h.shape[1] // LANES
    for s in range(ns):
        h3_ref[pl.ds(s, tm, stride=ns), :] = h[:, s * LANES:(s + 1) * LANES]
    logits = _dot(h.astype(BF16), w_ref[...]) + b_ref[...]
    lt = logits.T
    exp_f = lax.broadcasted_iota(I32, lt.shape, 0).astype(F32)
    chosen_t = jnp.zeros(lt.shape, F32)
    idxs, hits, vals = [], [], []
    for k in range(top_k):
        mx = jnp.max(lt, axis=0, keepdims=True)
        idx = jnp.min(jnp.where(lt == mx, exp_f, float(LANES)), axis=0, keepdims=True)
        hit = exp_f == idx
        idxs.append(idx)
        hits.append(hit)
        vals.append(mx)
        chosen_t = jnp.where(hit, 1.0, chosen_t)
        lt = jnp.where(hit, -jnp.inf, lt)
    es = [jnp.exp(v - vals[0]) for v in vals]
    den = es[0]
    for e in es[1:]:
        den = den + e
    upper = jnp.where(lax.broadcasted_iota(I32, (tm, tm), 0) < lax.broadcasted_iota(I32, (tm, tm), 1), 1.0, 0.0)
    cb = chosen_t.astype(BF16)
    before_t = _dot(cb, upper.astype(BF16)) + seen_col_ref[...]
    row8 = lax.broadcasted_iota(I32, (2 * top_k, tm), 0)
    meta8 = jnp.zeros((2 * top_k, tm), F32)
    gate8 = jnp.zeros((2 * top_k, tm), F32)
    for k in range(top_k):
        rank = jnp.sum(jnp.where(hits[k], before_t, 0.0), axis=0, keepdims=True)
        meta8 = jnp.where(row8 == k, idxs[k], meta8)
        meta8 = jnp.where(row8 == top_k + k, rank, meta8)
        gate8 = jnp.where(row8 == k, es[k] / den, gate8)
    pad = jnp.zeros((LANES - 2 * top_k, tm), F32)
    meta_ref[...] = jnp.concatenate([meta8, pad], axis=0).T.astype(I32)
    gate_ref[...] = jnp.concatenate([gate8, pad], axis=0).T
    seen_col_ref[...] = seen_col_ref[...] + jnp.sum(chosen_t, axis=1, keepdims=True)
    seen_ref[...] = seen_ref[...] + _dot_nt(jnp.ones((8, tm), BF16), cb)[0:1, :]
    cnt_ref[...] = seen_ref[...].astype(I32)


def moe_router(x, sc, sh, router_w, router_b, *, seq, tm=512):
    T, D = x.shape
    E = router_w.shape[1]
    nb = seq // tm
    ns = D // LANES
    assert ns == 8
    wp = jnp.zeros((D, LANES), BF16).at[:, :E].set(router_w.astype(BF16))
    bp = jnp.full((1, LANES), NEG_BIG, F32).at[0, :E].set(router_b)
    per_b = pl.BlockSpec((1, 1, D), lambda i: (i // nb, 0, 0))
    return pl.pallas_call(
        functools.partial(_router_kernel, top_k=MOE_K),
        grid=(T // tm,),
        in_specs=[pl.BlockSpec((tm, D), lambda i: (i, 0)), per_b, per_b,
                  pl.BlockSpec((D, LANES), lambda i: (0, 0)),
                  pl.BlockSpec((1, LANES), lambda i: (0, 0))],
        out_specs=[pl.BlockSpec((tm * ns, LANES), lambda i: (i, 0)),
                   pl.BlockSpec((tm, LANES), lambda i: (i, 0)),
                   pl.BlockSpec((tm, LANES), lambda i: (i, 0)),
                   pl.BlockSpec((1, LANES), lambda i: (0, 0))],
        out_shape=[jax.ShapeDtypeStruct((T * ns, LANES), F32),
                   jax.ShapeDtypeStruct((T, LANES), I32),
                   jax.ShapeDtypeStruct((T, LANES), F32),
                   jax.ShapeDtypeStruct((1, LANES), I32)],
        scratch_shapes=[pltpu.VMEM((1, LANES), F32), pltpu.VMEM((LANES, 1), F32)],
        compiler_params=_params(("arbitrary",)),
        name="moe_router",
    )(x, sc, sh, wp, bp)


def _dispatch_kernel(fill_ref, slots_ref, slots1_ref, slots2_ref, h3_ref, xs3_ref, zero_ref, hbuf_ref, fill_sem,
                     load_sems, row_sems, *, tm, top_k, n_exp, ns):
    i = pl.program_id(0)
    n = pl.num_programs(0)
    nbuf = hbuf_ref.shape[0]

    def load(step):
        src = h3_ref.at[pl.ds(pl.multiple_of(step * (tm * ns), tm * ns), tm * ns)]
        return pltpu.make_async_copy(src, hbuf_ref.at[step % nbuf], load_sems.at[step % nbuf])

    def row_copy(step, slots, r, k):
        return pltpu.make_async_copy(_tile_rows(hbuf_ref.at[step % nbuf], r, ns),
                                     _tile_rows_from(xs3_ref, slots[r * top_k + k], ns), row_sems.at[step % nbuf])

    def issue_rows(step, slots):
        def body(r, carry):
            for k in range(top_k):
                row_copy(step, slots, r, k).start(priority=k % 2)
            return carry
        lax.fori_loop(0, tm, body, 0, unroll=8)

    def drain_rows(step, slots):
        def body(r, carry):
            for k in range(top_k):
                row_copy(step, slots, r, k).wait()
            return carry
        lax.fori_loop(0, tm, body, 0, unroll=8)

    @pl.when(i == 0)
    def _():
        load(0).start()
        zero_ref[...] = jnp.zeros_like(zero_ref)

        def fill(e):
            dst = xs3_ref.at[pl.ds(pl.multiple_of(fill_ref[e] * ns, ns), zero_ref.shape[0])]
            return pltpu.make_async_copy(zero_ref, dst, fill_sem)

        for e in range(n_exp):
            fill(e).start()
        for e in range(n_exp):
            fill(e).wait()

    @pl.when(i >= 2)
    def _():
        drain_rows(i - 2, slots2_ref)

    @pl.when(i + 1 < n)
    def _():
        load(i + 1).start()

    load(i).wait()
    issue_rows(i, slots_ref)

    @pl.when(i == n - 1)
    def _():
        @pl.when(i >= 1)
        def _():
            drain_rows(i - 1, slots1_ref)
        drain_rows(i, slots_ref)


def moe_dispatch(h3, slots, fill_start, n_rows, *, tm=512):
    ns = 8
    T = h3.shape[0] // ns
    K = MOE_K
    slot_spec = lambda back: pl.BlockSpec((tm * K,), lambda i, fill: (jnp.maximum(i - back, 0),), memory_space=pltpu.SMEM)
    grid_spec = pltpu.PrefetchScalarGridSpec(
        num_scalar_prefetch=1,
        grid=(T // tm,),
        in_specs=[slot_spec(0), slot_spec(1), slot_spec(2), pl.BlockSpec(memory_space=pl.ANY)],
        out_specs=pl.BlockSpec(memory_space=pl.ANY),
        scratch_shapes=[pltpu.VMEM((MOE_BLK * ns, LANES), F32), pltpu.VMEM((3, tm * ns, LANES), F32),
                        pltpu.SemaphoreType.DMA(()), pltpu.SemaphoreType.DMA((3,)), pltpu.SemaphoreType.DMA((3,))],
    )
    return pl.pallas_call(
        functools.partial(_dispatch_kernel, tm=tm, top_k=K, n_exp=MOE_E, ns=ns),
        grid_spec=grid_spec,
        out_shape=jax.ShapeDtypeStruct((n_rows * ns, LANES), F32),
        compiler_params=_params(("arbitrary",)),
        name="moe_dispatch",
    )(fill_start, slots, slots, slots, h3)


def _expert_kernel(be_ref, nu_ref, nv_ref, x3_ref, w1_ref, b1_ref, w2_ref, b2_ref, o3_ref, w1b_ref, w2b_ref, *, dff, sub):
    i = pl.program_id(0)
    ns = w1b_ref.shape[0] // LANES
    blk = x3_ref.shape[0] // ns

    @pl.when(i < nu_ref[0])
    def _():
        @pl.when(jnp.logical_or(i == 0, be_ref[i] != be_ref[jnp.maximum(i - 1, 0)]))
        def _():
            w1b_ref[...] = w1_ref[0].astype(BF16)
            w2b_ref[...] = w2_ref[0].astype(BF16)

        def mlp(part):
            r0 = part * sub * ns
            xb = jnp.concatenate([x3_ref[pl.ds(r0 + s, sub, stride=ns), :].astype(BF16) for s in range(ns)], axis=-1)
            gu = _dot(xb, w1b_ref[...]) + b1_ref[0]
            gg = jnp.minimum(gu[:, :dff], GLU_LIMIT)
            up = jnp.clip(gu[:, dff:], -GLU_LIMIT, GLU_LIMIT)
            act = (up + 1.0) * (gg * _sigmoid(GLU_ALPHA * gg))
            y = _dot(act.astype(BF16), w2b_ref[...]) + b2_ref[0]
            for s in range(ns):
                o3_ref[pl.ds(r0 + s, sub, stride=ns), :] = y[:, s * LANES:(s + 1) * LANES]

        n_sub = blk // sub
        full = nv_ref[i] > (n_sub - 1) * sub

        @pl.when(full)
        def _():
            for part in range(n_sub):
                mlp(part)

        @pl.when(jnp.logical_not(full))
        def _():
            for part in range(n_sub - 1):
                @pl.when(nv_ref[i] > part * sub)
                def _():
                    mlp(part)


def moe_experts(xs3, block_expert, n_used, n_valid, w1, b1, w2, b2, layer):
    L, E, D, two_f = w1.shape
    ns = D // LANES
    nblk = block_expert.shape[0]
    row_map = lambda i, be, nu, nv: (jnp.minimum(i, nu[0] - 1), 0)
    w_map = lambda i, be, nu, nv: (layer, be[i], 0, 0)
    grid_spec = pltpu.PrefetchScalarGridSpec(
        num_scalar_prefetch=3,
        grid=(nblk,),
        in_specs=[pl.BlockSpec((MOE_BLK * ns, LANES), row_map),
                  pl.BlockSpec((None, 1, D, two_f), w_map),
                  pl.BlockSpec((None, 1, 1, two_f), w_map),
                  pl.BlockSpec((None, 1, two_f // 2, D), w_map),
                  pl.BlockSpec((None, 1, 1, D), w_map)],
        out_specs=pl.BlockSpec((MOE_BLK * ns, LANES), row_map),
        scratch_shapes=[pltpu.VMEM((D, two_f), BF16), pltpu.VMEM((two_f // 2, D), BF16)],
    )
    return pl.pallas_call(
        functools.partial(_expert_kernel, dff=two_f // 2, sub=MOE_SUB),
        grid_spec=grid_spec,
        out_shape=jax.ShapeDtypeStruct((nblk * MOE_BLK * ns, LANES), F32),
        compiler_params=_params(("arbitrary",)),
        name="moe_experts",
    )(block_expert, n_used, n_valid, xs3, w1, b1.reshape(L, E, 1, two_f), w2, b2.reshape(L, E, 1, D))


def _combine_kernel(slots_ref, nslots_ref, out3_ref, gt_ref, res_ref, gate_ref, g_ref, bb_ref, o_ref, y4_ref, sems,
                    *, tm, top_k, alpha, ns):
    i = pl.program_id(0)
    buf = i % 2

    def row_copy(slots, b, r, k):
        return pltpu.make_async_copy(_tile_rows_from(out3_ref, slots[r * top_k + k], ns),
                                     _tile_rows(y4_ref.at[b, k], r, ns), sems.at[b])

    def issue(slots, b):
        def body(r, carry):
            for k in range(top_k):
                row_copy(slots, b, r, k).start(priority=k % 2)
            return carry
        lax.fori_loop(0, tm, body, 0, unroll=8)

    def drain(slots, b):
        def body(r, carry):
            for k in range(top_k):
                row_copy(slots, b, r, k).wait()
            return carry
        lax.fori_loop(0, tm, body, 0, unroll=8)

    @pl.when(i == 0)
    def _():
        issue(slots_ref, 0)

    @pl.when(i + 1 < pl.num_programs(0))
    def _():
        issue(nslots_ref, 1 - buf)

    drain(slots_ref, buf)
    gt = gt_ref[...]
    cols = []
    for s in range(ns):
        acc = gt[:, 0:1] * y4_ref[buf, 0, pl.ds(s, tm, stride=ns), :]
        for k in range(1, top_k):
            acc = acc + gt[:, k:k + 1] * y4_ref[buf, k, pl.ds(s, tm, stride=ns), :]
        cols.append(acc)
    y = jnp.concatenate(cols, axis=-1)
    z = alpha * res_ref[...] + (1.0 + gate_ref[0]) * y
    o_ref[...] = _ln_rows(z, g_ref[...], bb_ref[...])


def moe_combine(out3, slots, gates, res, gate, g, b, *, seq, alpha, tm=512):
    T, D = res.shape
    K = MOE_K
    ns = D // LANES
    nb = seq // tm
    nsteps = T // tm
    one = pl.BlockSpec((1, D), lambda i: (0, 0))
    return pl.pallas_call(
        functools.partial(_combine_kernel, tm=tm, top_k=K, alpha=alpha, ns=ns),
        grid=(nsteps,),
        in_specs=[pl.BlockSpec((tm * K,), lambda i: (i,), memory_space=pltpu.SMEM),
                  pl.BlockSpec((tm * K,), lambda i: (jnp.minimum(i + 1, nsteps - 1),), memory_space=pltpu.SMEM),
                  pl.BlockSpec(memory_space=pl.ANY),
                  pl.BlockSpec((tm, LANES), lambda i: (i, 0)),
                  pl.BlockSpec((tm, D), lambda i: (i, 0)),
                  pl.BlockSpec((1, 1, D), lambda i: (i // nb, 0, 0)), one, one],
        out_specs=pl.BlockSpec((tm, D), lambda i: (i, 0)),
        out_shape=jax.ShapeDtypeStruct((T, D), F32),
        scratch_shapes=[pltpu.VMEM((2, K, tm * ns, LANES), F32), pltpu.SemaphoreType.DMA((2,))],
        compiler_params=_params(("arbitrary",)),
        name="moe_combine",
    )(slots, slots, out3, gates, res, gate, g.reshape(1, D), b.reshape(1, D))


def moe_layer(x, sc, sh, gate, ln_g, ln_b, router_w, router_b, w1, b1, w2, b2, layer, *, seq, alpha):
    T, D = x.shape
    E, K = MOE_E, MOE_K
    h3, meta, gates, cnt = moe_router(x, sc, sh, router_w, router_b, seq=seq)
    counts = cnt[0, :E]
    blocks_per = (counts + MOE_BLK - 1) // MOE_BLK
    block_end = jnp.cumsum(blocks_per)
    pad_start = (block_end - blocks_per) * MOE_BLK
    n_blocks = -(-(T * K) // MOE_BLK) + E
    blk = jnp.arange(n_blocks, dtype=I32)
    block_expert = jnp.minimum(jnp.sum(block_end[None, :] <= blk[:, None], axis=1), E - 1).astype(I32)
    onehot = meta[:, :K, None] == jnp.arange(E, dtype=I32)[None, None, :]
    slots = (meta[:, K:2 * K] + jnp.sum(jnp.where(onehot, pad_start[None, None, :], 0), axis=-1)).reshape(-1).astype(I32)
    slots = slots * (D // LANES)
    fill_start = jnp.where(blocks_per > 0, (block_end - 1) * MOE_BLK, (n_blocks + jnp.arange(E, dtype=I32)) * MOE_BLK).astype(I32)
    first_blk = block_end - blocks_per
    n_valid = jnp.clip(counts[block_expert] - (blk - first_blk[block_expert]) * MOE_BLK, 0, MOE_BLK)
    n_valid = jnp.where(blk < block_end[-1], n_valid, 0).astype(I32)
    xs3 = moe_dispatch(h3, slots, fill_start, (n_blocks + E) * MOE_BLK)
    out3 = moe_experts(xs3, block_expert, block_end[-1:].astype(I32), n_valid, w1, b1, w2, b2, layer)
    return moe_combine(out3, slots, gates, x, gate, ln_g, ln_b, seq=seq, alpha=alpha)


def _rope_tables(positions, d):
    inv = ROPE_BASE ** (-jnp.arange(0, d, 2, dtype=F32) / d)
    ang = positions.astype(F32).reshape(-1)[:, None] * inv
    return jnp.cos(ang), jnp.sin(ang)


def kernel(x, c, positions, ada_w, ada_b, ln1_g, ln1_b, ln2_g, ln2_b, router_w, router_b, moe_w1, moe_b1, moe_w2, moe_b2, ret_w_in, ret_gn_g, ret_gn_b, ret_w_out, conv_w_pw1, conv_b_pw1, conv_w_dw, conv_b_dw, conv_ln_g, conv_ln_b, conv_w_pw2, conv_b_pw2, nsa_w_in, nsa_gate_b, nsa_cmp_pos_k, nsa_cmp_pos_v, nsa_cmp_k_w1, nsa_cmp_k_w2, nsa_cmp_v_w1, nsa_cmp_v_w2, nsa_w_out, pool_w, pool_b, pool_scale):
    B, S, D = x.shape
    T = B * S
    depth = ada_w.shape[0]
    alpha = (2.0 * depth) ** 0.25
    mod = ada_modulation(c, ada_w, ada_b)
    xf = x.reshape(T, D)
    for i in range(depth):
        kind, j = i % 4, i // 4
        sh1, sc1, g1, sh2, sc2, g2 = (mod[i, :, k * D:(k + 1) * D].reshape(B, 1, D) for k in range(6))
        res1 = (g1, ln1_g[i], ln1_b[i])
        if kind == 0:
            dk = ret_w_in.shape[2] // 6 // RET_H
            cos, sin = _rope_tables(positions, dk)
            p = ret_projection(xf, sc1, sh1, ret_w_in[j].astype(BF16), cos, sin, seq=S)
            o = retention_core(p, ret_gn_g[j], ret_gn_b[j], batch=B, seq=S)
            xf = fused_matmul(o, ret_w_out[j].astype(BF16), seq=S, resln=(xf,) + res1, alpha=alpha, name="ret_out")
        elif kind == 1:
            u = conv_glu(xf, sc1, sh1, conv_w_pw1[j].astype(BF16), conv_b_pw1[j], seq=S)
            xf = conv_tail(u, conv_w_dw[j], conv_b_dw[j], conv_ln_g[j], conv_ln_b[j], conv_w_pw2[j].astype(BF16),
                           conv_b_pw2[j], xf, *res1, seq=S, alpha=alpha)
        elif kind == 2:
            G, dh, H = NSA_G, NSA_DH, NSA_H
            cos, sin = _rope_tables(positions, dh)
            cos_p = jnp.tile(cos, (1, LANES // (dh // 2)))
            sin_p = jnp.tile(jnp.concatenate([-sin, sin], axis=1), (1, LANES // dh))
            n_main = H * dh + 6 * G * dh
            ng = 3 * H // G
            w_g = jnp.zeros((D, G, LANES), F32).at[:, :, :ng].set(nsa_w_in[j][:, n_main:].reshape(D, G, ng))
            w_p = jnp.concatenate([nsa_w_in[j][:, :n_main], w_g.reshape(D, G * LANES)], axis=1).astype(BF16)
            gb_p = jnp.zeros((G, LANES), F32).at[:, :ng].set(nsa_gate_b[j].reshape(G, ng)).reshape(1, G * LANES)
            q, cv, kvx, gates = nsa_projection(xf, sc1, sh1, w_p, cos_p, sin_p, gb_p, seq=S)
            cv = cv.reshape(B, S, 2, G, dh).transpose(2, 0, 3, 1, 4).reshape(2, B * G, S // NSA_CMP_STEP, NSA_CMP_STEP * dh)
            k_cmp, v_cmp = nsa_compress(cv[0], cv[1], nsa_cmp_pos_k[j], nsa_cmp_pos_v[j], nsa_cmp_k_w1[j],
                                        nsa_cmp_k_w2[j], nsa_cmp_v_w1[j], nsa_cmp_v_w2[j])
            dup = lambda a: jnp.concatenate([a, a], axis=-1)
            lo = lambda a: jnp.concatenate([a, jnp.zeros_like(a)], axis=-1)
            hi = lambda a: jnp.concatenate([jnp.zeros_like(a), a], axis=-1)
            o = nsa_attention(q.reshape(B, S, H * dh), gates, dup(k_cmp), lo(v_cmp), hi(v_cmp), kvx, batch=B, seq=S)
            xf = fused_matmul(o.reshape(T, H * dh), nsa_w_out[j].astype(BF16), seq=S, resln=(xf,) + res1, alpha=alpha,
                              name="nsa_out")
        else:
            xf = pool_layer(xf, sc1, sh1, pool_w[j], pool_b[j], pool_scale[j], *res1, seq=S, alpha=alpha)
        xf = moe_layer(xf, sc2, sh2, g2, ln2_g[i], ln2_b[i], router_w[i], router_b[i], moe_w1, moe_b1, moe_w2, moe_b2, i,
                       seq=S, alpha=alpha)
    return xf.reshape(B, S, D)
```
